```python
import jax, jax.numpy as jnp
from jax import lax
import numpy as np

D_MODEL = 1024
BATCH = 8
SEQ = 2048
DEPTH = 2
DEC_BATCH = 32
DEC_SEQ = 8
PAST_LEN = 16384
PAGE_SIZE = 128

EPS = 1e-6
POOL_WINDOWS = (2, 4, 8, 16)
POOL_GROUPS = 4
POOL_GROUP_DIM = D_MODEL // 8
POOL_WIDTH = POOL_GROUPS * POOL_GROUP_DIM
POOL_BUF = 15
MLA_HEADS = 8
QK_NOPE = 64
QK_ROPE = 32
V_HEAD = 64
Q_LORA = 384
KV_LORA = 256
MLA_WIDTH = MLA_HEADS * V_HEAD
MLA_SCALE = (QK_NOPE + QK_ROPE) ** -0.5
ROPE_THETA = 10000.0
Q_BLOCK = 128
DN_HEADS = 8
DN_DK = 64
DN_DV = 64
DN_KEY_WIDTH = DN_HEADS * DN_DK
DN_WIDTH = DN_HEADS * DN_DV
CONV_W = 4
CONV_CH = 2 * DN_KEY_WIDTH + DN_WIDTH
DN_CHUNK = 64
N_BRANCH = 3
IN_SPLITS = (POOL_WIDTH, POOL_WIDTH,
             Q_LORA, KV_LORA, QK_ROPE, MLA_WIDTH,
             CONV_CH, DN_WIDTH, DN_HEADS, DN_HEADS,
             N_BRANCH * D_MODEL)
IN_WIDTH = sum(IN_SPLITS)

kernel_name = 'hybrid_pool_mla_gdn_step'


def _rmsnorm(x, g):
    xf = x.astype(jnp.float32)
    y = xf * lax.rsqrt(jnp.mean(xf * xf, axis=-1, keepdims=True) + EPS)
    return (y * g.astype(jnp.float32)).astype(x.dtype)


def _l2norm(x):
    xf = x.astype(jnp.float32)
    return xf * lax.rsqrt(jnp.sum(xf * xf, axis=-1, keepdims=True) + EPS)


def _rope(x, pos):
    half = QK_ROPE // 2
    inv = jnp.power(ROPE_THETA, -jnp.arange(half, dtype=jnp.float32) / half)
    ang = pos.astype(jnp.float32)[:, None] * inv[None, :]
    cos = jnp.cos(ang)[None, :, None, :]
    sin = jnp.sin(ang)[None, :, None, :]
    xf = x.astype(jnp.float32)
    x1, x2 = xf[..., :half], xf[..., half:]
    return jnp.concatenate([x1 * cos - x2 * sin, x2 * cos + x1 * sin], axis=-1).astype(x.dtype)


def _pool_mixer(u, buf, pos, mix, scale):
    B, L, _ = u.shape
    xx = jnp.concatenate([buf.astype(u.dtype), u], axis=1)
    xf = xx.astype(jnp.float32)
    prefix = jnp.concatenate([jnp.zeros_like(xf[:, :1]), lax.cumsum(xf, axis=1)], axis=1)
    hi = prefix[:, POOL_BUF + 1:]
    outs = []
    for gi, w in enumerate(POOL_WINDOWS):
        c0 = gi * POOL_GROUP_DIM
        c1 = c0 + POOL_GROUP_DIM
        lo = prefix[:, POOL_BUF + 1 - w:POOL_BUF + 1 - w + L, c0:c1]
        cnt = jnp.minimum(pos + 1, w).astype(jnp.float32)[None, :, None]
        outs.append((hi[..., c0:c1] - lo) / cnt)
    d = jnp.concatenate(outs, axis=-1) - u.astype(jnp.float32)
    d = d.reshape(B, L, POOL_GROUPS, POOL_GROUP_DIM).astype(u.dtype)
    y = jnp.einsum('blgc,gcd->blgd', d, mix).reshape(B, L, POOL_WIDTH) * scale
    return y, xx[:, -POOL_BUF:]


def _mla_attend(q_lat, q_rope, c_kv, k_rope, q_pos, k_pos):
    B, Lq, H, C = q_lat.shape
    qb = min(Q_BLOCK, Lq)
    nb = -(-Lq // qb)
    pad = nb * qb - Lq
    q_lat = jnp.pad(q_lat, ((0, 0), (0, pad), (0, 0), (0, 0)))
    q_rope = jnp.pad(q_rope, ((0, 0), (0, pad), (0, 0), (0, 0)))
    q_pos = jnp.pad(q_pos, (0, pad))
    to_blocks = lambda t: jnp.moveaxis(t.reshape(B, nb, qb, *t.shape[2:]), 1, 0)
    ql_b, qr_b = to_blocks(q_lat), to_blocks(q_rope)
    qp_b = q_pos.reshape(nb, qb)

    def block(args):
        ql, qr, qp = args
        s = (jnp.einsum('bqhc,bkc->bhqk', ql, c_kv, preferred_element_type=jnp.float32)
             + jnp.einsum('bqhr,bkr->bhqk', qr, k_rope, preferred_element_type=jnp.float32)) * MLA_SCALE
        mask = k_pos[None, :] <= qp[:, None]
        p = jax.nn.softmax(jnp.where(mask[None, None], s, -jnp.inf), axis=-1)
        o = jnp.einsum('bhqk,bkc->bqhc', p.astype(c_kv.dtype), c_kv, preferred_element_type=jnp.float32)
        return o.astype(c_kv.dtype)

    o = lax.map(block, (ql_b, qr_b, qp_b))
    return jnp.moveaxis(o, 0, 1).reshape(B, nb * qb, H, C)[:, :Lq]


def _short_conv(u, buf, w):
    L = u.shape[1]
    xx = jnp.concatenate([buf.astype(u.dtype), u], axis=1)
    y = xx[:, 0:L] * w[0]
    for j in range(1, CONV_W):
        y = y + xx[:, j:j + L] * w[j]
    return jax.nn.silu(y), xx[:, -(CONV_W - 1):]


def _gated_delta(q, k, v, g, beta, s0):
    B, L, H, DK = q.shape
    C = min(DN_CHUNK, L)
    n = -(-L // C)
    pad = n * C - L
    padf = lambda t: jnp.pad(t, [(0, 0), (0, pad)] + [(0, 0)] * (t.ndim - 2))
    chunks = lambda t: jnp.moveaxis(padf(t).reshape(B, n, C, *t.shape[2:]), (1, 3), (0, 2))
    q = chunks(q * (DK ** -0.5))
    k, v, g, beta = chunks(k), chunks(v), chunks(g), chunks(beta)
    gc = jnp.cumsum(g, axis=-1)
    tri_incl = jnp.tril(jnp.ones((C, C), dtype=bool))
    tri_strict = jnp.tril(jnp.ones((C, C), dtype=bool), -1)
    diff = gc[..., :, None] - gc[..., None, :]
    decay = jnp.where(tri_incl, jnp.exp(jnp.where(tri_incl, diff, 0.0)), 0.0)
    kb = k * beta[..., None]
    a_mat = jnp.where(tri_strict, jnp.einsum('...id,...jd->...ij', kb, k) * decay, 0.0)
    eye = jnp.eye(C, dtype=a_mat.dtype)
    t_inv = lax.linalg.triangular_solve(eye + a_mat, jnp.broadcast_to(eye, a_mat.shape),
                                        left_side=True, lower=True, unit_diagonal=True)
    u = t_inv @ (v * beta[..., None])
    w = t_inv @ (kb * jnp.exp(gc)[..., None])
    qk = jnp.where(tri_incl, jnp.einsum('...id,...jd->...ij', q, k) * decay, 0.0)

    def step(s, xs):
        q_i, k_i, u_i, w_i, gc_i, qk_i = xs
        v_new = u_i - w_i @ s
        o = (q_i * jnp.exp(gc_i)[..., None]) @ s + qk_i @ v_new
        g_last = gc_i[..., -1]
        s = s * jnp.exp(g_last)[..., None, None] + jnp.einsum(
            'bhcd,bhce->bhde', k_i * jnp.exp(g_last[..., None] - gc_i)[..., None], v_new)
        return s, o

    s_fin, o = lax.scan(step, s0, (q, k, u, w, gc, qk))
    o = jnp.moveaxis(o, (0, 2), (1, 3)).reshape(B, n * C, H, -1)[:, :L]
    return o, s_fin


def _layer(x, start, kv_past, kr_past, pool_buf, conv_buf, s0,
           norm_g, w_in, pool_mix, pool_scale, q_norm_g, w_uq, kv_norm_g, w_uk, w_uv,
           conv_w, a_log, dt_bias, dn_norm_g, w_br_pool, w_br_mla, w_br_dn, w_out):
    B, L, _ = x.shape
    f32 = jnp.float32
    pos = start + jnp.arange(L, dtype=jnp.int32)
    xn = _rmsnorm(x, norm_g)
    h = xn @ w_in
    split_pts = [int(c) for c in np.cumsum(IN_SPLITS)[:-1]]
    (h_pool, z_pool, h_q, h_kv, h_kr, z_mla, h_qkv, z_dn, h_beta, h_alpha, h_gate) = jnp.split(h, split_pts, axis=-1)

    y_pool, pool_buf_new = _pool_mixer(h_pool, pool_buf, pos, pool_mix, pool_scale)
    y_a = y_pool * jax.nn.silu(z_pool)

    c_q = _rmsnorm(h_q, q_norm_g)
    q = (c_q @ w_uq).reshape(B, L, MLA_HEADS, QK_NOPE + QK_ROPE)
    q_nope = q[..., :QK_NOPE]
    q_rope = _rope(q[..., QK_NOPE:], pos)
    c_kv = _rmsnorm(h_kv, kv_norm_g)
    k_r = _rope(h_kr[:, :, None, :], pos)[:, :, 0]
    q_lat = jnp.einsum('blhd,chd->blhc', q_nope, w_uk)
    keys_lat = jnp.concatenate([kv_past.astype(c_kv.dtype), c_kv], axis=1)
    keys_rope = jnp.concatenate([kr_past.astype(k_r.dtype), k_r], axis=1)
    k_pos = jnp.arange(keys_lat.shape[1], dtype=jnp.int32)
    o_lat = _mla_attend(q_lat, q_rope, keys_lat, keys_rope, pos, k_pos)
    o_mla = jnp.einsum('blhc,chd->blhd', o_lat, w_uv).reshape(B, L, MLA_WIDTH)
    y_b = o_mla * jax.nn.silu(z_mla)

    qkv, conv_buf_new = _short_conv(h_qkv, conv_buf, conv_w)
    q_d = _l2norm(qkv[..., :DN_KEY_WIDTH].reshape(B, L, DN_HEADS, DN_DK))
    k_d = _l2norm(qkv[..., DN_KEY_WIDTH:2 * DN_KEY_WIDTH].reshape(B, L, DN_HEADS, DN_DK))
    v_d = qkv[..., 2 * DN_KEY_WIDTH:].reshape(B, L, DN_HEADS, DN_DV).astype(f32)
    beta = jax.nn.sigmoid(h_beta.astype(f32))
    g = -jnp.exp(a_log.astype(f32)) * jax.nn.softplus(h_alpha.astype(f32) + dt_bias.astype(f32))
    o_dn, s_new = _gated_delta(q_d, k_d, v_d, g, beta, s0.astype(f32))
    y_c = _rmsnorm(o_dn, dn_norm_g).astype(x.dtype).reshape(B, L, DN_WIDTH) * jax.nn.silu(z_dn)

    gates = jax.nn.sigmoid(h_gate.astype(f32)).astype(x.dtype).reshape(B, L, N_BRANCH, D_MODEL)
    merged = (gates[:, :, 0] * (y_a @ w_br_pool) + gates[:, :, 1] * (y_b @ w_br_mla)
              + gates[:, :, 2] * (y_c @ w_br_dn))
    x = x + merged @ w_out
    return x, c_kv, k_r, pool_buf_new, conv_buf_new, s_new.astype(x.dtype)


def setup_inputs(seed: int = 0) -> dict:
    key = jax.random.key(seed)
    ks = jax.random.split(key, 32)
    f32 = jnp.float32
    nrm = lambda i, shape, scale: jax.random.normal(ks[i], shape, f32) * scale
    n_pages = PAST_LEN // PAGE_SIZE
    n_used = DEC_BATCH * n_pages
    n_pool = n_used + max(1, n_used // 4)
    perm = jax.random.permutation(ks[7], n_pool).astype(jnp.int32)
    page_table = perm[:n_used].reshape(DEC_BATCH, n_pages)
    dt = jnp.exp(jax.random.uniform(ks[19], (DEPTH, DN_HEADS), f32, np.log(1e-3), np.log(1e-1)))
    return {
        'x_prompt': nrm(0, (BATCH, SEQ, D_MODEL), 1.0),
        'x_sample': nrm(1, (DEC_BATCH, DEC_SEQ, D_MODEL), 1.0),
        'cache_kv_latent': nrm(2, (DEPTH, n_pool, PAGE_SIZE, KV_LORA), 1.0),
        'cache_k_rope': nrm(3, (DEPTH, n_pool, PAGE_SIZE, QK_ROPE), 1.0),
        'state_pool': nrm(4, (DEPTH, DEC_BATCH, POOL_BUF, POOL_WIDTH), 1.0),
        'state_conv': nrm(5, (DEPTH, DEC_BATCH, CONV_W - 1, CONV_CH), 1.0),
        'state_delta': nrm(6, (DEPTH, DEC_BATCH, DN_HEADS, DN_DK, DN_DV), 0.1),
        'page_table': page_table,
        'norm_g': 1.0 + nrm(8, (DEPTH, D_MODEL), 0.02),
        'w_in': nrm(9, (DEPTH, D_MODEL, IN_WIDTH), D_MODEL ** -0.5),
        'pool_mix': nrm(10, (DEPTH, POOL_GROUPS, POOL_GROUP_DIM, POOL_GROUP_DIM), POOL_GROUP_DIM ** -0.5),
        'pool_scale': 1.0 + nrm(11, (DEPTH, POOL_WIDTH), 0.1),
        'q_norm_g': 1.0 + nrm(12, (DEPTH, Q_LORA), 0.02),
        'w_uq': nrm(13, (DEPTH, Q_LORA, MLA_HEADS * (QK_NOPE + QK_ROPE)), Q_LORA ** -0.5),
        'kv_norm_g': 1.0 + nrm(14, (DEPTH, KV_LORA), 0.02),
        'w_uk': nrm(15, (DEPTH, KV_LORA, MLA_HEADS, QK_NOPE), KV_LORA ** -0.5),
        'w_uv': nrm(16, (DEPTH, KV_LORA, MLA_HEADS, V_HEAD), KV_LORA ** -0.5),
        'conv_w': nrm(17, (DEPTH, CONV_W, CONV_CH), CONV_W ** -0.5),
        'a_log': jnp.log(jax.random.uniform(ks[18], (DEPTH, DN_HEADS), f32, 1.0, 16.0)),
        'dt_bias': jnp.log(jnp.expm1(dt)),
        'dn_norm_g': 1.0 + nrm(20, (DEPTH, DN_DV), 0.02),
        'w_br_pool': nrm(21, (DEPTH, POOL_WIDTH, D_MODEL), POOL_WIDTH ** -0.5),
        'w_br_mla': nrm(22, (DEPTH, MLA_WIDTH, D_MODEL), MLA_WIDTH ** -0.5),
        'w_br_dn': nrm(23, (DEPTH, DN_WIDTH, D_MODEL), DN_WIDTH ** -0.5),
        'w_out': nrm(24, (DEPTH, D_MODEL, D_MODEL), D_MODEL ** -0.5),
        'final_norm_g': 1.0 + nrm(25, (D_MODEL,), 0.02),
    }


def reference(x_prompt, x_sample, cache_kv_latent, cache_k_rope, state_pool, state_conv, state_delta,
              page_table, norm_g, w_in, pool_mix, pool_scale, q_norm_g, w_uq, kv_norm_g, w_uk, w_uv,
              conv_w, a_log, dt_bias, dn_norm_g, w_br_pool, w_br_mla, w_br_dn, w_out, final_norm_g):
    bp = x_prompt.shape[0]
    db, n_pages = page_table.shape
    past_len = n_pages * PAGE_SIZE
    dt_ = x_prompt.dtype
    xp, xs = x_prompt, x_sample
    p_kv, p_kr, p_pool, p_conv, p_delta = [], [], [], [], []
    s_kv, s_kr, s_pool, s_conv, s_delta = [], [], [], [], []
    for l in range(DEPTH):
        params = (norm_g[l], w_in[l], pool_mix[l], pool_scale[l], q_norm_g[l], w_uq[l], kv_norm_g[l],
                  w_uk[l], w_uv[l], conv_w[l], a_log[l], dt_bias[l], dn_norm_g[l],
                  w_br_pool[l], w_br_mla[l], w_br_dn[l], w_out[l])
        xp, ckv, kr, pb, cb, sd = _layer(
            xp, 0,
            jnp.zeros((bp, 0, KV_LORA), dt_), jnp.zeros((bp, 0, QK_ROPE), dt_),
            jnp.zeros((bp, POOL_BUF, POOL_WIDTH), dt_), jnp.zeros((bp, CONV_W - 1, CONV_CH), dt_),
            jnp.zeros((bp, DN_HEADS, DN_DK, DN_DV), jnp.float32), *params)
        p_kv.append(ckv); p_kr.append(kr); p_pool.append(pb); p_conv.append(cb); p_delta.append(sd)
        kv_past = cache_kv_latent[l][page_table].reshape(db, past_len, KV_LORA)
        kr_past = cache_k_rope[l][page_table].reshape(db, past_len, QK_ROPE)
        xs, ckv, kr, pb, cb, sd = _layer(
            xs, past_len, kv_past, kr_past, state_pool[l], state_conv[l], state_delta[l], *params)
        s_kv.append(ckv); s_kr.append(kr); s_pool.append(pb); s_conv.append(cb); s_delta.append(sd)
    y_prompt = _rmsnorm(xp, final_norm_g)
    y_sample = _rmsnorm(xs, final_norm_g)
    return (y_prompt, y_sample,
            jnp.stack(p_kv), jnp.stack(p_kr), jnp.stack(p_pool), jnp.stack(p_conv), jnp.stack(p_delta),
            jnp.stack(s_kv), jnp.stack(s_kr), jnp.stack(s_pool), jnp.stack(s_conv), jnp.stack(s_delta))
```

```python
import functools

import numpy as np
import jax
import jax.numpy as jnp
from jax import lax
from jax.experimental import pallas as pl
from jax.experimental.pallas import tpu as pltpu

F32 = jnp.float32
BF16 = jnp.bfloat16
HIGHEST = lax.Precision.HIGHEST

D_MODEL = 1024
EPS = 1e-6
POOL_WINDOWS = (2, 4, 8, 16)
POOL_GROUP_DIM = 128
POOL_WIDTH = 512
POOL_BUF = 15
MLA_HEADS = 8
QK_NOPE = 64
QK_ROPE = 32
V_HEAD = 64
Q_LORA = 384
KV_LORA = 256
MLA_WIDTH = 512
MLA_SCALE = (QK_NOPE + QK_ROPE) ** -0.5
ROPE_THETA = 10000.0
DN_HEADS = 8
DN_DK = 64
DN_DV = 64
DN_WIDTH = 512
CONV_W = 4
CONV_CH = 1536
DN_CHUNK = 64
PAGE_SIZE = 128
N_BRANCH = 3

LANES = 128
SUBLANES = 8
HEAD_BLOCK = LANES
ROPE_LANE0 = QK_NOPE
ROPE_HALF = QK_ROPE // 2
POOL_HALO = 16
CONV_HALO = 8
VMEM_LIMIT = 48 * 1024 * 1024
NEG = -1e30


def _cparams(*sem):
    return pltpu.CompilerParams(dimension_semantics=sem, vmem_limit_bytes=VMEM_LIMIT)


def _const_spec(shape):
    nd = len(shape)
    return pl.BlockSpec(shape, lambda *_: (0,) * nd, pipeline_mode=pl.Buffered(1))


def _dot(a, b):
    return jnp.dot(a, b, preferred_element_type=F32)


def _dot_nt(a, b):
    return lax.dot_general(a, b, (((1,), (1,)), ((), ())), preferred_element_type=F32)


def _dot_f32(a, b):
    return jnp.dot(a, b, preferred_element_type=F32, precision=HIGHEST)


def _silu(x):
    return x * jax.nn.sigmoid(x)


def _rms(x, g):
    return x * lax.rsqrt(jnp.mean(x * x, axis=-1, keepdims=True) + EPS) * g


IN_SEG_WIDTHS = (POOL_WIDTH, POOL_WIDTH, Q_LORA, KV_LORA, HEAD_BLOCK, MLA_WIDTH,
                 CONV_CH, DN_WIDTH, LANES, N_BRANCH * D_MODEL)
IN_DOT_CHUNK = 512


def _inproj_kernel(x_ref, g_ref, *refs):
    n = len(IN_SEG_WIDTHS)
    w_refs, o_refs = refs[:n], refs[n:]
    xn = _rms(x_ref[...], g_ref[...]).astype(BF16)
    for w_ref, o_ref, width in zip(w_refs, o_refs, IN_SEG_WIDTHS):
        for c0 in range(0, width, IN_DOT_CHUNK):
            c1 = min(c0 + IN_DOT_CHUNK, width)
            o_ref[:, c0:c1] = _dot(xn, w_ref[:, c0:c1])


def _inproj(x2d, norm_g, w_segs, tm):
    t = x2d.shape[0]
    row = lambda w: pl.BlockSpec((tm, w), lambda i: (i, 0))
    return pl.pallas_call(
        _inproj_kernel,
        grid=(t // tm,),
        in_specs=[row(D_MODEL), _const_spec((1, D_MODEL))] + [_const_spec(w.shape) for w in w_segs],
        out_specs=[row(w) for w in IN_SEG_WIDTHS],
        out_shape=[jax.ShapeDtypeStruct((t, w), F32) for w in IN_SEG_WIDTHS],
        compiler_params=_cparams("parallel"),
        name="inproj",
    )(x2d, norm_g, *w_segs)


def _pool_kernel(u_ref, z_ref, buf_ref, mix_ref, scale_ref, y_ref, st_ref, xx_ref, *, tl, start, n_l):
    li = pl.program_id(1)

    @pl.when(li == 0)
    def _():
        xx_ref[0:POOL_HALO, :] = buf_ref[...]

    u = u_ref[...]
    xx_ref[POOL_HALO:POOL_HALO + tl, :] = u
    row = lax.broadcasted_iota(jnp.int32, (tl, 1), 0)
    pos1 = start + li * tl + row + 1
    for gi, w in enumerate(POOL_WINDOWS):
        lanes = slice(POOL_GROUP_DIM * gi, POOL_GROUP_DIM * (gi + 1))
        s = u[:, lanes]
        for o in range(1, w):
            s = s + xx_ref[POOL_HALO - o:POOL_HALO - o + tl, lanes]
        cnt = jnp.minimum(pos1, w).astype(F32)
        d = s / cnt - u[:, lanes]
        y = _dot(d.astype(BF16), mix_ref[gi]) * scale_ref[:, lanes]
        y_ref[:, lanes] = (y * _silu(z_ref[:, lanes])).astype(y_ref.dtype)

    tail = xx_ref[tl:tl + POOL_HALO, :]

    @pl.when(li == n_l - 1)
    def _():
        st_ref[...] = tail

    xx_ref[0:POOL_HALO, :] = tail


def _pool(h_pool, z_pool, buf16, mix, scale, b, l, tl, start, y_dtype):
    n_l = l // tl
    row = lambda w: pl.BlockSpec((tl, w), lambda bi, li: (bi * n_l + li, 0))
    st = pl.BlockSpec((None, POOL_HALO, POOL_WIDTH), lambda bi, li: (bi, 0, 0))
    return pl.pallas_call(
        functools.partial(_pool_kernel, tl=tl, start=start, n_l=n_l),
        grid=(b, n_l),
        in_specs=[row(POOL_WIDTH), row(POOL_WIDTH), st, _const_spec(mix.shape), _const_spec(scale.shape)],
        out_specs=[row(POOL_WIDTH), st],
        out_shape=[jax.ShapeDtypeStruct((b * l, POOL_WIDTH), y_dtype),
                   jax.ShapeDtypeStruct((b, POOL_HALO, POOL_WIDTH), F32)],
        scratch_shapes=[pltpu.VMEM((POOL_HALO + tl, POOL_WIDTH), F32)],
        compiler_params=_cparams("parallel", "arbitrary"),
        name="pool_mixer",
    )(h_pool, z_pool, buf16, mix, scale)


def _rope_table_kernel(inv_ref, c_ref, s1_ref, s2_ref, *, tl, start):
    i = pl.program_id(0)
    shape = (tl, LANES)
    pos = (start + i * tl + lax.broadcasted_iota(jnp.int32, shape, 0)).astype(F32)
    lane = lax.broadcasted_iota(jnp.int32, shape, 1)
    ang = pos * inv_ref[...]
    cos, sin = jnp.cos(ang), jnp.sin(ang)
    first = (lane >= ROPE_LANE0) & (lane < ROPE_LANE0 + ROPE_HALF)
    second = (lane >= ROPE_LANE0 + ROPE_HALF) & (lane < ROPE_LANE0 + QK_ROPE)
    c_ref[...] = jnp.where(first | second, cos, 1.0)
    s1_ref[...] = jnp.where(first, -sin, 0.0)
    s2_ref[...] = jnp.where(second, sin, 0.0)


def _rope_tables(l, tl, start):
    half = ROPE_HALF
    inv = jnp.power(ROPE_THETA, -jnp.arange(half, dtype=F32) / half)
    inv_lane = jnp.zeros((1, LANES), F32)
    inv_lane = inv_lane.at[0, ROPE_LANE0:ROPE_LANE0 + half].set(inv)
    inv_lane = inv_lane.at[0, ROPE_LANE0 + half:ROPE_LANE0 + QK_ROPE].set(inv)
    blk = pl.BlockSpec((tl, LANES), lambda i: (i, 0))
    return pl.pallas_call(
        functools.partial(_rope_table_kernel, tl=tl, start=start),
        grid=(l // tl,),
        in_specs=[_const_spec((1, LANES))],
        out_specs=[blk, blk, blk],
        out_shape=[jax.ShapeDtypeStruct((l, LANES), F32)] * 3,
        compiler_params=_cparams("parallel"),
        name="rope_tables",
    )(inv_lane)


def _rope(x, c, s1, s2):
    return x * c + pltpu.roll(x, LANES - ROPE_HALF, 1) * s1 + pltpu.roll(x, ROPE_HALF, 1) * s2


def _mla_prep_kernel(hq_ref, hkv_ref, hkr_ref, c_ref, s1_ref, s2_ref, qg_ref, wuq_ref, kvg_ref, *rest,
                     absorbed):
    c, s1, s2 = c_ref[...], s1_ref[...], s2_ref[...]
    cq = _rms(hq_ref[...], qg_ref[...]).astype(BF16)
    q = _dot(cq, wuq_ref[...])
    ckv = _rms(hkv_ref[...], kvg_ref[...])
    kr = _rope(hkr_ref[...], c, s1, s2)
    if absorbed:
        wcat_ref, ckv_ref, kr_ref, qcat_ref = rest
    else:
        wuk_ref, wuv_ref, ckv_ref, kr_ref, q_ref, k_ref, v_ref = rest
        ckv16 = ckv.astype(BF16)
        knope = _dot(ckv16, wuk_ref[...])
        v_ref[...] = _dot(ckv16, wuv_ref[...]).astype(BF16)
    ckv_ref[...] = ckv
    kr_ref[...] = kr
    for h in range(MLA_HEADS):
        blk = slice(HEAD_BLOCK * h, HEAD_BLOCK * (h + 1))
        qh = (_rope(q[:, blk], c, s1, s2) * MLA_SCALE).astype(BF16)
        if absorbed:
            w = wcat_ref.shape[-1]
            qcat_ref[:, w * h:w * (h + 1)] = _dot(qh, wcat_ref[h]).astype(BF16)
        else:
            q_ref[:, blk] = qh
            k_ref[:, blk] = (knope[:, blk] + kr).astype(BF16)


def _mla_prep(h_q, h_kv, h_kr, tabs, q_norm_g, wuq_p, kv_norm_g, extra_w, b, l, tm, absorbed):
    n_l = l // tm
    t = b * l
    row = lambda w: pl.BlockSpec((tm, w), lambda bi, li: (bi * n_l + li, 0))
    tab = pl.BlockSpec((tm, LANES), lambda bi, li: (li, 0))
    hw = MLA_HEADS * HEAD_BLOCK
    in_specs = [row(Q_LORA), row(KV_LORA), row(HEAD_BLOCK), tab, tab, tab,
                _const_spec(q_norm_g.shape), _const_spec(wuq_p.shape), _const_spec(kv_norm_g.shape)]
    in_specs += [_const_spec(w.shape) for w in extra_w]
    out_specs = [row(KV_LORA), row(HEAD_BLOCK)]
    out_shape = [jax.ShapeDtypeStruct((t, KV_LORA), F32), jax.ShapeDtypeStruct((t, HEAD_BLOCK), F32)]
    if absorbed:
        wc = MLA_HEADS * extra_w[0].shape[-1]
        out_specs += [row(wc)]
        out_shape += [jax.ShapeDtypeStruct((t, wc), BF16)]
    else:
        out_specs += [row(hw), row(hw), row(MLA_WIDTH)]
        out_shape += [jax.ShapeDtypeStruct((t, hw), BF16), jax.ShapeDtypeStruct((t, hw), BF16),
                      jax.ShapeDtypeStruct((t, MLA_WIDTH), BF16)]
    return pl.pallas_call(
        functools.partial(_mla_prep_kernel, absorbed=absorbed),
        grid=(b, n_l),
        in_specs=in_specs, out_specs=out_specs, out_shape=out_shape,
        compiler_params=_cparams("parallel", "parallel"),
        name="mla_prep_absorbed" if absorbed else "mla_prep",
    )(h_q, h_kv, h_kr, *tabs, q_norm_g, wuq_p, kv_norm_g, *extra_w)


def _attn_kernel(q_ref, k_ref, v_ref, z_ref, y_ref, *, ta):
    qi = pl.program_id(1)
    low_half = lax.broadcasted_iota(jnp.int32, (ta, LANES), 1) < V_HEAD
    causal = (lax.broadcasted_iota(jnp.int32, (ta, ta), 1)
              <= lax.broadcasted_iota(jnp.int32, (ta, ta), 0))
    for p in range(MLA_HEADS // 2):
        pair = slice(LANES * p, LANES * (p + 1))
        outs = []
        for hh in range(2):
            h = 2 * p + hh
            blk = slice(HEAD_BLOCK * h, HEAD_BLOCK * (h + 1))
            qh = q_ref[:, blk]

            def step(kb, carry, masked, qh=qh, blk=blk, pair=pair):
                m, l, acc = carry
                r0 = pl.multiple_of(kb * ta, ta)
                s = _dot_nt(qh, k_ref[pl.ds(r0, ta), blk])
                if masked:
                    s = jnp.where(causal, s, NEG)
                m_new = jnp.maximum(m, jnp.max(s, axis=-1, keepdims=True))
                a = jnp.exp(m - m_new)
                pm = jnp.exp(s - m_new)
                l = a * l + jnp.sum(pm, axis=-1, keepdims=True)
                acc = a * acc + _dot(pm.astype(BF16), v_ref[pl.ds(r0, ta), pair])
                return m_new, l, acc

            carry = (jnp.full((ta, 1), NEG, F32), jnp.zeros((ta, 1), F32), jnp.zeros((ta, LANES), F32))
            carry = lax.fori_loop(0, qi, functools.partial(step, masked=False), carry)
            _, l, acc = step(qi, carry, True)
            outs.append(acc / l)
        o = jnp.where(low_half, outs[0], outs[1])
        y_ref[:, pair] = (o * _silu(z_ref[:, pair])).astype(y_ref.dtype)


def _attn_prompt(q, k, v, z_mla, b, l, ta):
    n_q = l // ta
    hw = MLA_HEADS * HEAD_BLOCK
    row = lambda w: pl.BlockSpec((ta, w), lambda bi, qi: (bi * n_q + qi, 0))
    seq = lambda w: pl.BlockSpec((l, w), lambda bi, qi: (bi, 0))
    return pl.pallas_call(
        functools.partial(_attn_kernel, ta=ta),
        grid=(b, n_q),
        in_specs=[row(hw), seq(hw), seq(MLA_WIDTH), row(MLA_WIDTH)],
        out_specs=row(MLA_WIDTH),
        out_shape=jax.ShapeDtypeStruct((b * l, MLA_WIDTH), BF16),
        compiler_params=_cparams("parallel", "arbitrary"),
        name="mla_attention_prompt",
    )(q, k, v, z_mla)


def _attn_paged_kernel(pt_ref, qcat_ref, *refs, n_grp, l_new, n_steps):
    kv_refs = refs[:n_grp]
    kr_refs = refs[n_grp:2 * n_grp]
    nkv_ref, nkr_ref, wuv_ref, z_ref, y_ref, m_ref, l_ref, acc_ref = refs[2 * n_grp:]
    del pt_ref
    j = pl.program_id(1)
    rows = MLA_HEADS * l_new

    @pl.when(j == 0)
    def _():
        m_ref[...] = jnp.full(m_ref.shape, NEG, F32)
        l_ref[...] = jnp.zeros(l_ref.shape, F32)
        acc_ref[...] = jnp.zeros(acc_ref.shape, F32)

    qc = qcat_ref[...]
    ql, qr = qc[:, :KV_LORA], qc[:, KV_LORA:KV_LORA + QK_ROPE]

    def accumulate(kvs, krs, mask):
        s = jnp.concatenate([_dot_nt(ql, kv) + _dot_nt(qr, kr) for kv, kr in zip(kvs, krs)], axis=1)
        if mask is not None:
            s = jnp.where(mask, s, NEG)
        m = m_ref[...]
        m_new = jnp.maximum(m, jnp.max(s, axis=-1, keepdims=True))
        a = jnp.exp(m - m_new)
        pm = jnp.exp(s - m_new)
        l_ref[...] = a * l_ref[...] + jnp.sum(pm, axis=-1, keepdims=True)
        pm = pm.astype(BF16)
        pv = None
        for g, kv in enumerate(kvs):
            t = _dot(pm[:, PAGE_SIZE * g:PAGE_SIZE * (g + 1)], kv)
            pv = t if pv is None else pv + t
        acc_ref[...] = a * acc_ref[...] + pv
        m_ref[...] = m_new

    accumulate([r[...].astype(BF16) for r in kv_refs], [r[...].astype(BF16) for r in kr_refs], None)

    @pl.when(j == n_steps - 1)
    def _():
        pad = PAGE_SIZE - l_new
        nkv = jnp.concatenate([nkv_ref[...], jnp.zeros((pad, KV_LORA), F32)], axis=0).astype(BF16)
        nkr = jnp.concatenate([nkr_ref[...], jnp.zeros((pad, QK_ROPE), F32)], axis=0).astype(BF16)
        tok = lax.broadcasted_iota(jnp.int32, (rows, PAGE_SIZE), 0) % l_new
        key = lax.broadcasted_iota(jnp.int32, (rows, PAGE_SIZE), 1)
        accumulate([nkv], [nkr], key <= tok)
        o = (acc_ref[...] / l_ref[...]).astype(BF16)
        full = _dot(o, wuv_ref[...])
        rh = lax.broadcasted_iota(jnp.int32, full.shape, 0) // l_new
        ch = lax.broadcasted_iota(jnp.int32, full.shape, 1) // V_HEAD
        full = jnp.where(rh == ch, full, 0.0)
        out = full[0:l_new]
        for h in range(1, MLA_HEADS):
            out = out + full[l_new * h:l_new * (h + 1)]
        y_ref[...] = (out * _silu(z_ref[...])).astype(y_ref.dtype)


def _attn_paged(page_table, qcat, cache_kv, cache_kr, layer, new_kv, new_kr, wuv, z_mla, n_grp):
    db, n_pages = page_table.shape
    l_new = new_kv.shape[1]
    rows = MLA_HEADS * l_new
    n_steps = n_pages // n_grp
    wq = qcat.shape[-1]

    def page(width, g):
        return pl.BlockSpec((None, None, PAGE_SIZE, width),
                            lambda b, j, pt: (layer, pt[b * n_pages + j * n_grp + g], 0, 0))

    per_b = lambda r, w: pl.BlockSpec((None, r, w), lambda b, j, pt: (b, 0, 0))
    in_specs = [per_b(rows, wq)]
    in_specs += [page(KV_LORA, g) for g in range(n_grp)]
    in_specs += [page(QK_ROPE, g) for g in range(n_grp)]
    in_specs += [per_b(l_new, KV_LORA), per_b(l_new, QK_ROPE),
                 pl.BlockSpec(wuv.shape, lambda b, j, pt: (0, 0)), per_b(l_new, MLA_WIDTH)]
    grid_spec = pltpu.PrefetchScalarGridSpec(
        num_scalar_prefetch=1, grid=(db, n_steps), in_specs=in_specs,
        out_specs=per_b(l_new, MLA_WIDTH),
        scratch_shapes=[pltpu.VMEM((rows, 1), F32), pltpu.VMEM((rows, 1), F32),
                        pltpu.VMEM((rows, KV_LORA), F32)])
    return pl.pallas_call(
        functools.partial(_attn_paged_kernel, n_grp=n_grp, l_new=l_new, n_steps=n_steps),
        grid_spec=grid_spec,
        out_shape=jax.ShapeDtypeStruct((db, l_new, MLA_WIDTH), F32),
        compiler_params=_cparams("parallel", "arbitrary"),
        name="mla_attention_paged",
    )(page_table.reshape(-1), qcat, *([cache_kv] * n_grp), *([cache_kr] * n_grp), new_kv, new_kr, wuv, z_mla)


PAIR = 2 * DN_CHUNK
N_PAIRS = DN_HEADS // 2
NEUMANN_STEPS = 5


def _delta_consts(r):
    w = DN_WIDTH
    hid = np.arange(w) // DN_DK
    bd = (hid[:, None] == hid[None, :]).astype(np.float32)
    rr = np.arange(r)
    tri = ((rr[:, None] // DN_CHUNK == rr[None, :] // DN_CHUNK) & (rr[:, None] >= rr[None, :])).astype(np.float32)
    src = np.arange(LANES)
    eb = (src[:, None] == hid[None, :]).astype(np.float32)
    eg = (src[:, None] == hid[None, :] + DN_HEADS).astype(np.float32)
    hf = np.arange(DN_HEADS * LANES) // LANES
    ef = (src[:, None] == hf[None, :] + DN_HEADS).astype(np.float32)
    return tuple(jnp.asarray(a) for a in (bd, tri, eb, eg, ef))


def _delta_kernel(u_ref, z_ref, ba_ref, cbuf_ref, cw_ref, alog_ref, dtb_ref, sp0_ref, gdn_ref,
                  bd_ref, tri_ref, eb_ref, eg_ref, ef_ref,
                  y_ref, cst_ref, spo_ref,
                  xx_ref, sp_ref, q_s, k_s, kb_s, qe_s, vb_s, kbe_s, gcl_s, gf_s, o_s,
                  *, tl, r, n_l):
    li = pl.program_id(1)

    @pl.when(li == 0)
    def _():
        xx_ref[0:CONV_HALO, :] = cbuf_ref[...]
        sp_ref[...] = sp0_ref[...]

    xx_ref[CONV_HALO:CONV_HALO + tl, :] = u_ref[...]
    base = CONV_HALO - (CONV_W - 1)
    acc = xx_ref[base:base + tl, :] * cw_ref[0:1, :]
    for j in range(1, CONV_W):
        acc = acc + xx_ref[base + j:base + j + tl, :] * cw_ref[j:j + 1, :]
    qkv = _silu(acc)
    tail = xx_ref[tl:tl + CONV_HALO, :]

    @pl.when(li == n_l - 1)
    def _():
        cst_ref[...] = tail

    xx_ref[0:CONV_HALO, :] = tail

    def pad_rows(a):
        if tl == r:
            return a
        return jnp.concatenate([a, jnp.zeros((r - tl, a.shape[1]), a.dtype)], axis=0)

    bd = bd_ref[...]
    q = qkv[:, 0:DN_WIDTH]
    k = qkv[:, DN_WIDTH:2 * DN_WIDTH]
    v = qkv[:, 2 * DN_WIDTH:3 * DN_WIDTH]
    qn = q * lax.rsqrt(_dot_f32(q * q, bd) + EPS) * (DN_DK ** -0.5)
    kn = k * lax.rsqrt(_dot_f32(k * k, bd) + EPS)
    ba = ba_ref[...]
    lane = lax.broadcasted_iota(jnp.int32, ba.shape, 1)
    beta = jnp.where(lane < DN_HEADS, jax.nn.sigmoid(ba), 0.0)
    xa = ba + dtb_ref[...]
    softplus = jnp.maximum(xa, 0.0) + jnp.log1p(jnp.exp(-jnp.abs(xa)))
    g = jnp.where((lane >= DN_HEADS) & (lane < 2 * DN_HEADS), -jnp.exp(alog_ref[...]) * softplus, 0.0)
    qn, kn, v, beta, g = (pad_rows(a) for a in (qn, kn, v, beta, g))
    gc = _dot_f32(tri_ref[...], g)
    beta_l = _dot_f32(beta, eb_ref[...])
    gc_l = _dot_f32(gc, eg_ref[...])
    egc = jnp.exp(gc_l)
    kb = kn * beta_l
    q_s[...] = qn
    k_s[...] = kn
    kb_s[...] = kb
    qe_s[...] = qn * egc
    vb_s[...] = v * beta_l
    kbe_s[...] = kb * egc
    gcl_s[...] = gc_l
    gf_s[...] = _dot_f32(gc, ef_ref[...])

    ri = lax.broadcasted_iota(jnp.int32, (PAIR, PAIR), 0)
    ci = lax.broadcasted_iota(jnp.int32, (PAIR, PAIR), 1)
    same = (ri // DN_CHUNK) == (ci // DN_CHUNK)
    incl = same & (ri >= ci)
    strict = same & (ri > ci)
    eye = (ri == ci).astype(F32)
    first = lax.broadcasted_iota(jnp.int32, (DN_CHUNK, LANES), 1) < DN_DK

    def stack(a):
        return jnp.concatenate([jnp.where(first, a, 0.0), jnp.where(first, 0.0, a)], axis=0)

    def fold(a):
        return a[0:DN_CHUNK] + a[DN_CHUNK:PAIR]

    def chunk(c, carry):
        r0 = pl.multiple_of(c * DN_CHUNK, DN_CHUNK)
        rows = pl.ds(r0, DN_CHUNK)
        for p in range(N_PAIRS):
            lanes = slice(LANES * p, LANES * (p + 1))
            kst = stack(k_s[rows, lanes]).astype(BF16)
            a_mat = _dot_nt(stack(kb_s[rows, lanes]).astype(BF16), kst)
            qk = _dot_nt(stack(q_s[rows, lanes]).astype(BF16), kst)
            g_col = jnp.concatenate([gf_s[rows, LANES * 2 * p:LANES * (2 * p + 1)],
                                     gf_s[rows, LANES * (2 * p + 1):LANES * (2 * p + 2)]], axis=0)
            diff = g_col - g_col.T
            dec = jnp.where(incl, jnp.exp(jnp.where(incl, diff, 0.0)), 0.0)
            nm = jnp.where(strict, -(a_mat * dec), 0.0)
            t_inv = eye + nm
            pw = nm
            for _ in range(NEUMANN_STEPS):
                pw16 = pw.astype(BF16)
                pw = _dot(pw16, pw16)
                t_inv = t_inv + _dot(t_inv.astype(BF16), pw.astype(BF16))
            rhs = jnp.concatenate([stack(vb_s[rows, lanes]), stack(kbe_s[rows, lanes])], axis=1)
            uw = fold(_dot(t_inv.astype(BF16), rhs.astype(BF16)))
            sp = sp_ref[p]
            sp16 = sp.astype(BF16)
            v_new = uw[:, 0:LANES] - _dot(uw[:, LANES:2 * LANES].astype(BF16), sp16)
            intra = fold(_dot((qk * dec).astype(BF16), stack(v_new).astype(BF16)))
            o_s[rows, lanes] = _dot(qe_s[rows, lanes].astype(BF16), sp16) + intra
            gcl = gcl_s[rows, lanes]
            g_last = gcl[DN_CHUNK - 1:DN_CHUNK, :]
            k_dec = k_s[rows, lanes] * jnp.exp(g_last - gcl)
            upd = _dot(k_dec.T.astype(BF16), v_new.astype(BF16))
            sp_ref[p] = sp * jnp.exp(g_last) + jnp.where(same, upd, 0.0)
        return carry

    n_chunks = r // DN_CHUNK
    if n_chunks == 1:
        chunk(0, 0)
    else:
        lax.fori_loop(0, n_chunks, chunk, 0)

    o = o_s[0:tl, :]
    y = o * lax.rsqrt(_dot_f32(o * o, bd) * (1.0 / DN_DV) + EPS) * gdn_ref[...]
    y_ref[...] = (y * _silu(z_ref[...])).astype(y_ref.dtype)

    @pl.when(li == n_l - 1)
    def _():
        spo_ref[...] = sp_ref[...]


def _delta(h_qkv, z_dn, h_ba, cbuf8, cw8, alog_row, dtb_row, sp0, gdn_row, b, l, tl, y_dtype):
    n_l = l // tl
    r = max(tl, DN_CHUNK)
    consts = _delta_consts(r)
    row = lambda w: pl.BlockSpec((tl, w), lambda bi, li: (bi * n_l + li, 0))
    cst = pl.BlockSpec((None, CONV_HALO, CONV_CH), lambda bi, li: (bi, 0, 0))
    spb = pl.BlockSpec((None, N_PAIRS, PAIR, PAIR), lambda bi, li: (bi, 0, 0, 0))
    wide = lambda n: pltpu.VMEM((r, n), F32)
    return pl.pallas_call(
        functools.partial(_delta_kernel, tl=tl, r=r, n_l=n_l),
        grid=(b, n_l),
        in_specs=[row(CONV_CH), row(DN_WIDTH), row(LANES), cst, _const_spec(cw8.shape),
                  _const_spec(alog_row.shape), _const_spec(dtb_row.shape), spb, _const_spec(gdn_row.shape)]
                 + [_const_spec(c.shape) for c in consts],
        out_specs=[row(DN_WIDTH), cst, spb],
        out_shape=[jax.ShapeDtypeStruct((b * l, DN_WIDTH), y_dtype),
                   jax.ShapeDtypeStruct((b, CONV_HALO, CONV_CH), F32),
                   jax.ShapeDtypeStruct((b, N_PAIRS, PAIR, PAIR), F32)],
        scratch_shapes=[pltpu.VMEM((CONV_HALO + tl, CONV_CH), F32), pltpu.VMEM((N_PAIRS, PAIR, PAIR), F32)]
                       + [wide(DN_WIDTH)] * 7 + [wide(DN_HEADS * LANES), wide(DN_WIDTH)],
        compiler_params=_cparams("parallel", "arbitrary"),
        name="gated_deltanet",
    )(h_qkv, z_dn, h_ba, cbuf8, cw8, alog_row, dtb_row, sp0, gdn_row, *consts)


def _merge_kernel(x_ref, ya_ref, yb_ref, yc_ref, g_ref, wa_ref, wb_ref, wc_ref, wo_ref, fg_ref, o_ref, *, final):
    merged = None
    for j, (y_ref, w_ref) in enumerate(((ya_ref, wa_ref), (yb_ref, wb_ref), (yc_ref, wc_ref))):
        br = _dot(y_ref[...].astype(BF16), w_ref[...])
        t = jax.nn.sigmoid(g_ref[:, D_MODEL * j:D_MODEL * (j + 1)]) * br
        merged = t if merged is None else merged + t
    x = x_ref[...] + _dot(merged.astype(BF16), wo_ref[...])
    o_ref[...] = _rms(x, fg_ref[...]) if final else x


def _merge(x2d, ya, yb, yc, h_gate, wa, wb, wc, wo, final_g, tm, final):
    t = x2d.shape[0]
    row = lambda w: pl.BlockSpec((tm, w), lambda i: (i, 0))
    return pl.pallas_call(
        functools.partial(_merge_kernel, final=final),
        grid=(t // tm,),
        in_specs=[row(D_MODEL), row(POOL_WIDTH), row(MLA_WIDTH), row(DN_WIDTH), row(N_BRANCH * D_MODEL)]
                 + [_const_spec(w.shape) for w in (wa, wb, wc, wo, final_g)],
        out_specs=row(D_MODEL),
        out_shape=jax.ShapeDtypeStruct((t, D_MODEL), F32),
        compiler_params=_cparams("parallel"),
        name="merge_out",
    )(x2d, ya, yb, yc, h_gate, wa, wb, wc, wo, final_g)


def _layer_weights(l, norm_g, w_in, pool_mix, pool_scale, q_norm_g, w_uq, kv_norm_g, w_uk, w_uv,
                   conv_w, a_log, dt_bias, dn_norm_g, w_br_pool, w_br_mla, w_br_dn, w_out):
    wi = w_in[l]
    d = wi.shape[0]
    o = np.cumsum((0, POOL_WIDTH, POOL_WIDTH, Q_LORA, KV_LORA, QK_ROPE, MLA_WIDTH, CONV_CH, DN_WIDTH,
                   DN_HEADS, DN_HEADS, N_BRANCH * D_MODEL))
    seg = lambda i: wi[:, o[i]:o[i + 1]]
    kr_blk = jnp.zeros((d, HEAD_BLOCK), F32).at[:, ROPE_LANE0:ROPE_LANE0 + QK_ROPE].set(seg(4))
    ba_blk = jnp.zeros((d, LANES), F32).at[:, 0:2 * DN_HEADS].set(wi[:, o[8]:o[10]])
    w_segs = [seg(0), seg(1), seg(2), seg(3), kr_blk, seg(5), seg(6), seg(7), ba_blk, seg(10)]
    w_segs = [w.astype(BF16) for w in w_segs]

    dq = QK_NOPE + QK_ROPE
    wuq = w_uq[l].reshape(Q_LORA, MLA_HEADS, dq)
    wuq_p = jnp.pad(wuq, ((0, 0), (0, 0), (0, HEAD_BLOCK - dq))).reshape(Q_LORA, -1).astype(BF16)
    wuk_p = jnp.pad(w_uk[l], ((0, 0), (0, 0), (0, HEAD_BLOCK - QK_NOPE))).reshape(KV_LORA, -1).astype(BF16)
    wuv_f = w_uv[l].reshape(KV_LORA, MLA_WIDTH).astype(BF16)
    wcat = jnp.zeros((MLA_HEADS, HEAD_BLOCK, KV_LORA + LANES), F32)
    wcat = wcat.at[:, 0:QK_NOPE, 0:KV_LORA].set(jnp.transpose(w_uk[l], (1, 2, 0)))
    sel = jnp.eye(QK_ROPE, dtype=F32)
    wcat = wcat.at[:, ROPE_LANE0:ROPE_LANE0 + QK_ROPE, KV_LORA:KV_LORA + QK_ROPE].set(sel)
    lanes16 = slice(DN_HEADS, 2 * DN_HEADS)
    return dict(
        norm_g=norm_g[l][None, :], w_segs=w_segs,
        pool_mix=pool_mix[l].astype(BF16), pool_scale=pool_scale[l][None, :],
        q_norm_g=q_norm_g[l][None, :], wuq_p=wuq_p, kv_norm_g=kv_norm_g[l][None, :],
        wuk_p=wuk_p, wuv_f=wuv_f, wcat=wcat.astype(BF16),
        cw8=jnp.pad(conv_w[l], ((0, SUBLANES - CONV_W), (0, 0))),
        alog_row=jnp.zeros((1, LANES), F32).at[0, lanes16].set(a_log[l]),
        dtb_row=jnp.zeros((1, LANES), F32).at[0, lanes16].set(dt_bias[l]),
        gdn_row=jnp.tile(dn_norm_g[l], DN_HEADS)[None, :],
        wa=w_br_pool[l].astype(BF16), wb=w_br_mla[l].astype(BF16), wc=w_br_dn[l].astype(BF16),
        wo=w_out[l].astype(BF16))


def _tile(n, target):
    t = min(n, target)
    while n % t:
        t -= SUBLANES
    return t


def _group_layer(x2d, b, l, start, w, tabs, final_g, final, pool_buf, conv_buf, sp0, paged):
    t = b * l
    small = l < 2 * SUBLANES
    y_dtype = F32 if small else BF16
    h_pool, z_pool, h_q, h_kv, h_kr, z_mla, h_qkv, z_dn, h_ba, h_gate = _inproj(
        x2d, w["norm_g"], w["w_segs"], _tile(t, 256))

    ya, pool_st = _pool(h_pool, z_pool, pool_buf, w["pool_mix"], w["pool_scale"], b, l, _tile(l, 512), start, y_dtype)

    tm = _tile(l, 512)
    if paged is None:
        ckv, kr_blk, q, k, v = _mla_prep(h_q, h_kv, h_kr, tabs, w["q_norm_g"], w["wuq_p"], w["kv_norm_g"],
                                         (w["wuk_p"], w["wuv_f"]), b, l, tm, absorbed=False)
        k_r = kr_blk[:, ROPE_LANE0:ROPE_LANE0 + QK_ROPE]
        yb = _attn_prompt(q, k, v, z_mla, b, l, _tile(l, 512))
    else:
        page_table, cache_kv, cache_kr, layer = paged
        ckv, kr_blk, qcat = _mla_prep(h_q, h_kv, h_kr, tabs, w["q_norm_g"], w["wuq_p"], w["kv_norm_g"],
                                      (w["wcat"],), b, l, tm, absorbed=True)
        k_r = kr_blk[:, ROPE_LANE0:ROPE_LANE0 + QK_ROPE]
        wq = qcat.shape[-1] // MLA_HEADS
        qcat = qcat.reshape(b, l, MLA_HEADS, wq).transpose(0, 2, 1, 3).reshape(b, MLA_HEADS * l, wq)
        n_pages = page_table.shape[1]
        n_grp = 16
        while n_pages % n_grp:
            n_grp //= 2
        yb = _attn_paged(page_table, qcat, cache_kv, cache_kr, layer, ckv.reshape(b, l, KV_LORA),
                         k_r.reshape(b, l, QK_ROPE), w["wuv_f"], z_mla.reshape(b, l, MLA_WIDTH), n_grp)
        yb = yb.reshape(t, MLA_WIDTH)

    yc, conv_st, sp = _delta(h_qkv, z_dn, h_ba, conv_buf, w["cw8"], w["alog_row"], w["dtb_row"], sp0,
                             w["gdn_row"], b, l, _tile(l, 256), y_dtype)

    x_out = _merge(x2d, ya, yb, yc, h_gate, w["wa"], w["wb"], w["wc"], w["wo"], final_g, _tile(t, 256), final)
    return x_out, ckv, k_r, pool_st[:, 1:], conv_st[:, CONV_HALO - (CONV_W - 1):], sp


def _pairs_to_blockdiag(s):
    b = s.shape[0]
    s = s.reshape(b, N_PAIRS, 2, DN_DK, DN_DV)
    z = jnp.zeros_like(s[:, :, 0])
    top = jnp.concatenate([s[:, :, 0], z], axis=-1)
    bot = jnp.concatenate([z, s[:, :, 1]], axis=-1)
    return jnp.concatenate([top, bot], axis=-2)


def _blockdiag_to_pairs(sp):
    b = sp.shape[0]
    s = jnp.stack([sp[:, :, :DN_DK, :DN_DV], sp[:, :, DN_DK:, DN_DV:]], axis=2)
    return s.reshape(b, DN_HEADS, DN_DK, DN_DV)


def kernel(x_prompt, x_sample, cache_kv_latent, cache_k_rope, state_pool, state_conv, state_delta,
           page_table, norm_g, w_in, pool_mix, pool_scale, q_norm_g, w_uq, kv_norm_g, w_uk, w_uv,
           conv_w, a_log, dt_bias, dn_norm_g, w_br_pool, w_br_mla, w_br_dn, w_out, final_norm_g):
    bp, lp, d = x_prompt.shape
    db, ls, _ = x_sample.shape
    depth = w_in.shape[0]
    past_len = page_table.shape[1] * PAGE_SIZE
    final_g = final_norm_g[None, :]

    tabs_p = _rope_tables(lp, _tile(lp, 512), 0)
    tabs_s = _rope_tables(ls, ls, past_len)
    zero_pool = jnp.zeros((bp, POOL_HALO, POOL_WIDTH), F32)
    zero_conv = jnp.zeros((bp, CONV_HALO, CONV_CH), F32)
    zero_sp = jnp.zeros((bp, N_PAIRS, PAIR, PAIR), F32)

    xp = x_prompt.reshape(bp * lp, d)
    xs = x_sample.reshape(db * ls, d)
    outs_p, outs_s = [], []
    for l in range(depth):
        w = _layer_weights(l, norm_g, w_in, pool_mix, pool_scale, q_norm_g, w_uq, kv_norm_g, w_uk, w_uv,
                           conv_w, a_log, dt_bias, dn_norm_g, w_br_pool, w_br_mla, w_br_dn, w_out)
        final = l == depth - 1
        xp, ckv, kr, pst, cst, sp = _group_layer(xp, bp, lp, 0, w, tabs_p, final_g, final,
                                                 zero_pool, zero_conv, zero_sp, None)
        outs_p.append((ckv.reshape(bp, lp, KV_LORA), kr.reshape(bp, lp, QK_ROPE), pst, cst,
                       _blockdiag_to_pairs(sp)))
        pool_buf = jnp.pad(state_pool[l], ((0, 0), (POOL_HALO - POOL_BUF, 0), (0, 0)))
        conv_buf = jnp.pad(state_conv[l], ((0, 0), (CONV_HALO - (CONV_W - 1), 0), (0, 0)))
        xs, ckv, kr, pst, cst, sp = _group_layer(xs, db, ls, past_len, w, tabs_s, final_g, final,
                                                 pool_buf, conv_buf, _pairs_to_blockdiag(state_delta[l]),
                                                 (page_table, cache_kv_latent, cache_k_rope, l))
        outs_s.append((ckv.reshape(db, ls, KV_LORA), kr.reshape(db, ls, QK_ROPE), pst, cst,
                       _blockdiag_to_pairs(sp)))
    stack = lambda outs, i: jnp.stack([o[i] for o in outs])
    return (xp.reshape(bp, lp, d), xs.reshape(db, ls, d),
            *(stack(outs_p, i) for i in range(5)), *(stack(outs_s, i) for i in range(5)))
```

```python
import functools

import numpy as np
import jax
import jax.numpy as jnp
from jax import lax
from jax.experimental import pallas as pl
from jax.experimental.pallas import tpu as pltpu

F32 = jnp.float32
BF16 = jnp.bfloat16

D_MODEL = 1024
EPS = 1e-6
POOL_WINDOWS = (2, 4, 8, 16)
POOL_GROUP_DIM = 128
POOL_WIDTH = 512
POOL_BUF = 15
MLA_HEADS = 8
QK_NOPE = 64
QK_ROPE = 32
V_HEAD = 64
Q_LORA = 384
KV_LORA = 256
MLA_WIDTH = 512
MLA_SCALE = (QK_NOPE + QK_ROPE) ** -0.5
ROPE_THETA = 10000.0
DN_HEADS = 8
DN_DK = 64
DN_DV = 64
DN_WIDTH = 512
CONV_W = 4
CONV_CH = 1536
DN_CHUNK = 64
PAGE_SIZE = 128
N_BRANCH = 3

LANES = 128
SUBLANES = 8
HEAD_BLOCK = LANES
ROPE_LANE0 = QK_NOPE
ROPE_HALF = QK_ROPE // 2
POOL_HALO = 16
CONV_HALO = 8
VMEM_LIMIT = 48 * 1024 * 1024
NEG = -1e30


def _cparams(*sem):
    return pltpu.CompilerParams(dimension_semantics=sem, vmem_limit_bytes=VMEM_LIMIT)


def _const_spec(shape):
    nd = len(shape)
    return pl.BlockSpec(shape, lambda *_: (0,) * nd, pipeline_mode=pl.Buffered(1))


def _dot(a, b):
    return jnp.dot(a, b, preferred_element_type=F32)


def _dot_nt(a, b):
    return lax.dot_general(a, b, (((1,), (1,)), ((), ())), preferred_element_type=F32)


def _silu(x):
    return x * jax.nn.sigmoid(x)


def _rms(x, g):
    return x * lax.rsqrt(jnp.mean(x * x, axis=-1, keepdims=True) + EPS) * g


IN_SEG_WIDTHS = (POOL_WIDTH, POOL_WIDTH, Q_LORA, KV_LORA, HEAD_BLOCK, MLA_WIDTH,
                 CONV_CH, DN_WIDTH, LANES, N_BRANCH * D_MODEL)
IN_DOT_CHUNK = 512


def _inproj_kernel(x_ref, g_ref, *refs):
    n = len(IN_SEG_WIDTHS)
    w_refs, o_refs = refs[:n], refs[n:]
    xn = _rms(x_ref[...], g_ref[...]).astype(BF16)
    for w_ref, o_ref, width in zip(w_refs, o_refs, IN_SEG_WIDTHS):
        for c0 in range(0, width, IN_DOT_CHUNK):
            c1 = min(c0 + IN_DOT_CHUNK, width)
            o_ref[:, c0:c1] = _dot(xn, w_ref[:, c0:c1])


def _inproj(x2d, norm_g, w_segs, tm):
    t = x2d.shape[0]
    row = lambda w: pl.BlockSpec((tm, w), lambda i: (i, 0))
    return pl.pallas_call(
        _inproj_kernel,
        grid=(t // tm,),
        in_specs=[row(D_MODEL), _const_spec((1, D_MODEL))] + [_const_spec(w.shape) for w in w_segs],
        out_specs=[row(w) for w in IN_SEG_WIDTHS],
        out_shape=[jax.ShapeDtypeStruct((t, w), F32) for w in IN_SEG_WIDTHS],
        compiler_params=_cparams("parallel"),
        name="inproj",
    )(x2d, norm_g, *w_segs)


def _pool_kernel(u_ref, z_ref, buf_ref, mix_ref, scale_ref, y_ref, st_ref, xx_ref, *, tl, start, n_l):
    li = pl.program_id(1)

    @pl.when(li == 0)
    def _():
        xx_ref[0:POOL_HALO, :] = buf_ref[...]

    u = u_ref[...]
    xx_ref[POOL_HALO:POOL_HALO + tl, :] = u
    row = lax.broadcasted_iota(jnp.int32, (tl, 1), 0)
    pos1 = start + li * tl + row + 1
    for gi, w in enumerate(POOL_WINDOWS):
        lanes = slice(POOL_GROUP_DIM * gi, POOL_GROUP_DIM * (gi + 1))
        s = u[:, lanes]
        for o in range(1, w):
            s = s + xx_ref[POOL_HALO - o:POOL_HALO - o + tl, lanes]
        cnt = jnp.minimum(pos1, w).astype(F32)
        d = s / cnt - u[:, lanes]
        y = _dot(d.astype(BF16), mix_ref[gi]) * scale_ref[:, lanes]
        y_ref[:, lanes] = (y * _silu(z_ref[:, lanes])).astype(y_ref.dtype)

    tail = xx_ref[tl:tl + POOL_HALO, :]

    @pl.when(li == n_l - 1)
    def _():
        st_ref[...] = tail

    xx_ref[0:POOL_HALO, :] = tail


def _pool(h_pool, z_pool, buf16, mix, scale, b, l, tl, start, y_dtype):
    n_l = l // tl
    row = lambda w: pl.BlockSpec((tl, w), lambda bi, li: (bi * n_l + li, 0))
    st = pl.BlockSpec((None, POOL_HALO, POOL_WIDTH), lambda bi, li: (bi, 0, 0))
    return pl.pallas_call(
        functools.partial(_pool_kernel, tl=tl, start=start, n_l=n_l),
        grid=(b, n_l),
        in_specs=[row(POOL_WIDTH), row(POOL_WIDTH), st, _const_spec(mix.shape), _const_spec(scale.shape)],
        out_specs=[row(POOL_WIDTH), st],
        out_shape=[jax.ShapeDtypeStruct((b * l, POOL_WIDTH), y_dtype),
                   jax.ShapeDtypeStruct((b, POOL_HALO, POOL_WIDTH), F32)],
        scratch_shapes=[pltpu.VMEM((POOL_HALO + tl, POOL_WIDTH), F32)],
        compiler_params=_cparams("parallel", "arbitrary"),
        name="pool_mixer",
    )(h_pool, z_pool, buf16, mix, scale)


def _rope_table_kernel(inv_ref, c_ref, s1_ref, s2_ref, *, tl, start):
    i = pl.program_id(0)
    shape = (tl, LANES)
    pos = (start + i * tl + lax.broadcasted_iota(jnp.int32, shape, 0)).astype(F32)
    lane = lax.broadcasted_iota(jnp.int32, shape, 1)
    ang = pos * inv_ref[...]
    cos, sin = jnp.cos(ang), jnp.sin(ang)
    first = (lane >= ROPE_LANE0) & (lane < ROPE_LANE0 + ROPE_HALF)
    second = (lane >= ROPE_LANE0 + ROPE_HALF) & (lane < ROPE_LANE0 + QK_ROPE)
    c_ref[...] = jnp.where(first | second, cos, 1.0)
    s1_ref[...] = jnp.where(first, -sin, 0.0)
    s2_ref[...] = jnp.where(second, sin, 0.0)


def _rope_tables(l, tl, start):
    half = ROPE_HALF
    inv = jnp.power(ROPE_THETA, -jnp.arange(half, dtype=F32) / half)
    inv_lane = jnp.zeros((1, LANES), F32)
    inv_lane = inv_lane.at[0, ROPE_LANE0:ROPE_LANE0 + half].set(inv)
    inv_lane = inv_lane.at[0, ROPE_LANE0 + half:ROPE_LANE0 + QK_ROPE].set(inv)
    blk = pl.BlockSpec((tl, LANES), lambda i: (i, 0))
    return pl.pallas_call(
        functools.partial(_rope_table_kernel, tl=tl, start=start),
        grid=(l // tl,),
        in_specs=[_const_spec((1, LANES))],
        out_specs=[blk, blk, blk],
        out_shape=[jax.ShapeDtypeStruct((l, LANES), F32)] * 3,
        compiler_params=_cparams("parallel"),
        name="rope_tables",
    )(inv_lane)


def _rope(x, c, s1, s2):
    return x * c + pltpu.roll(x, LANES - ROPE_HALF, 1) * s1 + pltpu.roll(x, ROPE_HALF, 1) * s2


def _mla_prep_kernel(hq_ref, hkv_ref, hkr_ref, c_ref, s1_ref, s2_ref, qg_ref, wuq_ref, kvg_ref, *rest,
                     absorbed):
    c, s1, s2 = c_ref[...], s1_ref[...], s2_ref[...]
    cq = _rms(hq_ref[...], qg_ref[...]).astype(BF16)
    q = _dot(cq, wuq_ref[...])
    ckv = _rms(hkv_ref[...], kvg_ref[...])
    kr = _rope(hkr_ref[...], c, s1, s2)
    if absorbed:
        wcat_ref, ckv_ref, kr_ref, qcat_ref = rest
    else:
        wuk_ref, wuv_ref, ckv_ref, kr_ref, q_ref, k_ref, v_ref = rest
        ckv16 = ckv.astype(BF16)
        knope = _dot(ckv16, wuk_ref[...])
        v_ref[...] = _dot(ckv16, wuv_ref[...]).astype(BF16)
    ckv_ref[...] = ckv
    kr_ref[...] = kr
    for h in range(MLA_HEADS):
        blk = slice(HEAD_BLOCK * h, HEAD_BLOCK * (h + 1))
        qh = (_rope(q[:, blk], c, s1, s2) * MLA_SCALE).astype(BF16)
        if absorbed:
            w = wcat_ref.shape[-1]
            qcat_ref[:, w * h:w * (h + 1)] = _dot(qh, wcat_ref[h]).astype(BF16)
        else:
            q_ref[:, blk] = qh
            k_ref[:, blk] = (knope[:, blk] + kr).astype(BF16)


def _mla_prep(h_q, h_kv, h_kr, tabs, q_norm_g, wuq_p, kv_norm_g, extra_w, b, l, tm, absorbed):
    n_l = l // tm
    t = b * l
    row = lambda w: pl.BlockSpec((tm, w), lambda bi, li: (bi * n_l + li, 0))
    tab = pl.BlockSpec((tm, LANES), lambda bi, li: (li, 0))
    hw = MLA_HEADS * HEAD_BLOCK
    in_specs = [row(Q_LORA), row(KV_LORA), row(HEAD_BLOCK), tab, tab, tab,
                _const_spec(q_norm_g.shape), _const_spec(wuq_p.shape), _const_spec(kv_norm_g.shape)]
    in_specs += [_const_spec(w.shape) for w in extra_w]
    out_specs = [row(KV_LORA), row(HEAD_BLOCK)]
    out_shape = [jax.ShapeDtypeStruct((t, KV_LORA), F32), jax.ShapeDtypeStruct((t, HEAD_BLOCK), F32)]
    if absorbed:
        wc = MLA_HEADS * extra_w[0].shape[-1]
        out_specs += [row(wc)]
        out_shape += [jax.ShapeDtypeStruct((t, wc), BF16)]
    else:
        out_specs += [row(hw), row(hw), row(MLA_WIDTH)]
        out_shape += [jax.ShapeDtypeStruct((t, hw), BF16), jax.ShapeDtypeStruct((t, hw), BF16),
                      jax.ShapeDtypeStruct((t, MLA_WIDTH), BF16)]
    return pl.pallas_call(
        functools.partial(_mla_prep_kernel, absorbed=absorbed),
        grid=(b, n_l),
        in_specs=in_specs, out_specs=out_specs, out_shape=out_shape,
        compiler_params=_cparams("parallel", "parallel"),
        name="mla_prep_absorbed" if absorbed else "mla_prep",
    )(h_q, h_kv, h_kr, *tabs, q_norm_g, wuq_p, kv_norm_g, *extra_w)


def _attn_kernel(q_ref, k_ref, v_ref, z_ref, y_ref, *, ta):
    qi = pl.program_id(1)
    low_half = lax.broadcasted_iota(jnp.int32, (ta, LANES), 1) < V_HEAD
    causal = (lax.broadcasted_iota(jnp.int32, (ta, ta), 1)
              <= lax.broadcasted_iota(jnp.int32, (ta, ta), 0))
    for p in range(MLA_HEADS // 2):
        pair = slice(LANES * p, LANES * (p + 1))
        outs = []
        for hh in range(2):
            h = 2 * p + hh
            blk = slice(HEAD_BLOCK * h, HEAD_BLOCK * (h + 1))
            qh = q_ref[:, blk]

            def step(kb, carry, masked, qh=qh, blk=blk, pair=pair):
                m, l, acc = carry
                r0 = pl.multiple_of(kb * ta, ta)
                s = _dot_nt(qh, k_ref[pl.ds(r0, ta), blk])
                if masked:
                    s = jnp.where(causal, s, NEG)
                m_new = jnp.maximum(m, jnp.max(s, axis=-1, keepdims=True))
                a = jnp.exp(m - m_new)
                pm = jnp.exp(s - m_new)
                l = a * l + jnp.sum(pm, axis=-1, keepdims=True)
                acc = a * acc + _dot(pm.astype(BF16), v_ref[pl.ds(r0, ta), pair])
                return m_new, l, acc

            carry = (jnp.full((ta, 1), NEG, F32), jnp.zeros((ta, 1), F32), jnp.zeros((ta, LANES), F32))
            carry = lax.fori_loop(0, qi, functools.partial(step, masked=False), carry)
            _, l, acc = step(qi, carry, True)
            outs.append(acc / l)
        o = jnp.where(low_half, outs[0], outs[1])
        y_ref[:, pair] = (o * _silu(z_ref[:, pair])).astype(y_ref.dtype)


def _attn_prompt(q, k, v, z_mla, b, l, ta):
    n_q = l // ta
    hw = MLA_HEADS * HEAD_BLOCK
    row = lambda w: pl.BlockSpec((ta, w), lambda bi, qi: (bi * n_q + qi, 0))
    seq = lambda w: pl.BlockSpec((l, w), lambda bi, qi: (bi, 0))
    return pl.pallas_call(
        functools.partial(_attn_kernel, ta=ta),
        grid=(b, n_q),
        in_specs=[row(hw), seq(hw), seq(MLA_WIDTH), row(MLA_WIDTH)],
        out_specs=row(MLA_WIDTH),
        out_shape=jax.ShapeDtypeStruct((b * l, MLA_WIDTH), BF16),
        compiler_params=_cparams("parallel", "arbitrary"),
        name="mla_attention_prompt",
    )(q, k, v, z_mla)


def _attn_paged_kernel(pt_ref, qcat_ref, *refs, n_grp, l_new, n_steps):
    kv_refs = refs[:n_grp]
    krt_refs = refs[n_grp:2 * n_grp]
    nkv_ref, nkr_ref, wuv_ref, z_ref, y_ref, m_ref, l_ref, acc_ref = refs[2 * n_grp:]
    del pt_ref
    j = pl.program_id(1)
    rows = MLA_HEADS * l_new

    @pl.when(j == 0)
    def _():
        m_ref[...] = jnp.full(m_ref.shape, NEG, F32)
        l_ref[...] = jnp.zeros(l_ref.shape, F32)
        acc_ref[...] = jnp.zeros(acc_ref.shape, F32)

    qc = qcat_ref[...]
    ql, qr = qc[:, :KV_LORA], qc[:, KV_LORA:KV_LORA + QK_ROPE]

    def accumulate(s, kvs):
        m = m_ref[...]
        m_new = jnp.maximum(m, jnp.max(s, axis=-1, keepdims=True))
        a = jnp.exp(m - m_new)
        pm = jnp.exp(s - m_new)
        l_ref[...] = a * l_ref[...] + jnp.sum(pm, axis=-1, keepdims=True)
        pm = pm.astype(BF16)
        pv = None
        for g, kv in enumerate(kvs):
            t = _dot(pm[:, PAGE_SIZE * g:PAGE_SIZE * (g + 1)], kv)
            pv = t if pv is None else pv + t
        acc_ref[...] = a * acc_ref[...] + pv
        m_ref[...] = m_new

    kvs = [r[...].astype(BF16) for r in kv_refs]
    s = jnp.concatenate([_dot_nt(ql, kv) + _dot(qr, krt[...].astype(BF16)) for kv, krt in zip(kvs, krt_refs)],
                        axis=1)
    accumulate(s, kvs)

    @pl.when(j == n_steps - 1)
    def _():
        pad = PAGE_SIZE - l_new
        nkv = jnp.concatenate([nkv_ref[...], jnp.zeros((pad, KV_LORA), F32)], axis=0).astype(BF16)
        nkr = jnp.concatenate([nkr_ref[...], jnp.zeros((pad, QK_ROPE), F32)], axis=0).astype(BF16)
        tok = lax.broadcasted_iota(jnp.int32, (rows, PAGE_SIZE), 0) % l_new
        key = lax.broadcasted_iota(jnp.int32, (rows, PAGE_SIZE), 1)
        s_new = jnp.where(key <= tok, _dot_nt(ql, nkv) + _dot_nt(qr, nkr), NEG)
        accumulate(s_new, [nkv])
        o = (acc_ref[...] / l_ref[...]).astype(BF16)
        full = _dot(o, wuv_ref[...])
        rh = lax.broadcasted_iota(jnp.int32, full.shape, 0) // l_new
        ch = lax.broadcasted_iota(jnp.int32, full.shape, 1) // V_HEAD
        full = jnp.where(rh == ch, full, 0.0)
        out = full[0:l_new]
        for h in range(1, MLA_HEADS):
            out = out + full[l_new * h:l_new * (h + 1)]
        y_ref[...] = (out * _silu(z_ref[...])).astype(y_ref.dtype)


def _attn_paged(page_table, qcat, cache_kv, cache_krt, layer, new_kv, new_kr, wuv, z_mla, n_grp):
    db, n_pages = page_table.shape
    l_new = new_kv.shape[1]
    rows = MLA_HEADS * l_new
    n_steps = n_pages // n_grp
    wq = qcat.shape[-1]

    def page(shape2, g):
        return pl.BlockSpec((None, None) + shape2,
                            lambda b, j, pt: (layer, pt[b * n_pages + j * n_grp + g], 0, 0))

    per_b = lambda r, w: pl.BlockSpec((None, r, w), lambda b, j, pt: (b, 0, 0))
    in_specs = [per_b(rows, wq)]
    in_specs += [page((PAGE_SIZE, KV_LORA), g) for g in range(n_grp)]
    in_specs += [page((QK_ROPE, PAGE_SIZE), g) for g in range(n_grp)]
    in_specs += [per_b(l_new, KV_LORA), per_b(l_new, QK_ROPE),
                 pl.BlockSpec(wuv.shape, lambda b, j, pt: (0, 0)), per_b(l_new, MLA_WIDTH)]
    grid_spec = pltpu.PrefetchScalarGridSpec(
        num_scalar_prefetch=1, grid=(db, n_steps), in_specs=in_specs,
        out_specs=per_b(l_new, MLA_WIDTH),
        scratch_shapes=[pltpu.VMEM((rows, 1), F32), pltpu.VMEM((rows, 1), F32),
                        pltpu.VMEM((rows, KV_LORA), F32)])
    return pl.pallas_call(
        functools.partial(_attn_paged_kernel, n_grp=n_grp, l_new=l_new, n_steps=n_steps),
        grid_spec=grid_spec,
        out_shape=jax.ShapeDtypeStruct((db, l_new, MLA_WIDTH), F32),
        compiler_params=_cparams("parallel", "arbitrary"),
        name="mla_attention_paged",
    )(page_table.reshape(-1), qcat, *([cache_kv] * n_grp), *([cache_krt] * n_grp), new_kv, new_kr, wuv, z_mla)


PAIR = 2 * DN_CHUNK
N_PAIRS = DN_HEADS // 2
NEUMANN_STEPS = 5
SLOT_GROUP = 2
SAMPLE_SEQS_PER_STEP = 4


def _delta_consts(r):
    w = DN_WIDTH
    hid = np.arange(w) // DN_DK
    bd = hid[:, None] == hid[None, :]
    rr = np.arange(r)
    tri = (rr[:, None] // DN_CHUNK == rr[None, :] // DN_CHUNK) & (rr[:, None] >= rr[None, :])
    src = np.arange(LANES)
    eb = src[:, None] == hid[None, :]
    eg = src[:, None] == hid[None, :] + DN_HEADS
    hf = np.arange(DN_HEADS * LANES) // LANES
    ef = src[:, None] == hf[None, :] + DN_HEADS
    return tuple(jnp.asarray(a.astype(np.float32), dtype=BF16) for a in (bd, tri, eb, eg, ef))


def _split3(x):
    hi = x.astype(BF16)
    r1 = x - hi.astype(F32)
    mid = r1.astype(BF16)
    lo = (r1 - mid.astype(F32)).astype(BF16)
    return hi, mid, lo


def _dot_sel(x, sel):
    hi, mid, lo = _split3(x)
    return _dot(hi, sel) + _dot(mid, sel) + _dot(lo, sel)


def _sel_dot(sel, x):
    hi, mid, lo = _split3(x)
    return _dot(sel, hi) + _dot(sel, mid) + _dot(sel, lo)


def _delta_kernel(u_ref, z_ref, ba_ref, cbuf_ref, cw_ref, alog_ref, dtb_ref, sp0_ref, gdn_ref,
                  bd_ref, tri_ref, eb_ref, eg_ref, ef_ref,
                  y_ref, cst_ref, spo_ref,
                  xx_ref, sp_ref, q_s, k_s, kb_s, qe_s, vb_s, kbe_s, gcl_s, gf_s, u_s, w_s, o_s, qkd_s, kdt_s,
                  *, ns, tl, n_l):
    li = pl.program_id(1)
    cps = max(tl, DN_CHUNK) // DN_CHUNK
    seq_rows = cps * DN_CHUNK
    n_slots = ns * cps

    @pl.when(li == 0)
    def _():
        xx_ref[:, 0:CONV_HALO, :] = cbuf_ref[...]
        sp_ref[...] = sp0_ref[...]

    def pad_rows(a):
        if tl == seq_rows:
            return a
        return jnp.concatenate([a, jnp.zeros((seq_rows - tl, a.shape[1]), a.dtype)], axis=0)

    def per_seq(a):
        if ns == 1:
            return pad_rows(a)
        return jnp.concatenate([pad_rows(a[s * tl:(s + 1) * tl]) for s in range(ns)], axis=0)

    base = CONV_HALO - (CONV_W - 1)
    pieces = []
    for s in range(ns):
        xx_ref[s, CONV_HALO:CONV_HALO + tl, :] = u_ref[s * tl:(s + 1) * tl, :]
        acc = xx_ref[s, base:base + tl, :] * cw_ref[0:1, :]
        for j in range(1, CONV_W):
            acc = acc + xx_ref[s, base + j:base + j + tl, :] * cw_ref[j:j + 1, :]
        pieces.append(pad_rows(_silu(acc)))
    qkv = pieces[0] if ns == 1 else jnp.concatenate(pieces, axis=0)
    tail = xx_ref[:, tl:tl + CONV_HALO, :]

    @pl.when(li == n_l - 1)
    def _():
        cst_ref[...] = tail

    xx_ref[:, 0:CONV_HALO, :] = tail

    bd = bd_ref[...]
    q = qkv[:, 0:DN_WIDTH]
    k = qkv[:, DN_WIDTH:2 * DN_WIDTH]
    v = qkv[:, 2 * DN_WIDTH:3 * DN_WIDTH]
    qn = q * lax.rsqrt(_dot_sel(q * q, bd) + EPS) * (DN_DK ** -0.5)
    kn = k * lax.rsqrt(_dot_sel(k * k, bd) + EPS)
    ba = ba_ref[...]
    lane = lax.broadcasted_iota(jnp.int32, ba.shape, 1)
    beta = jnp.where(lane < DN_HEADS, jax.nn.sigmoid(ba), 0.0)
    xa = ba + dtb_ref[...]
    softplus = jnp.maximum(xa, 0.0) + jnp.log1p(jnp.exp(-jnp.abs(xa)))
    g = jnp.where((lane >= DN_HEADS) & (lane < 2 * DN_HEADS), -jnp.exp(alog_ref[...]) * softplus, 0.0)
    beta, g = per_seq(beta), per_seq(g)
    gc = _sel_dot(tri_ref[...], g)
    beta_l = _dot_sel(beta, eb_ref[...])
    gc_l = _dot_sel(gc, eg_ref[...])
    egc = jnp.exp(gc_l)
    kb = kn * beta_l
    q_s[...] = qn
    k_s[...] = kn
    kb_s[...] = kb
    qe_s[...] = qn * egc
    vb_s[...] = v * beta_l
    kbe_s[...] = kb * egc
    gcl_s[...] = gc_l
    gf_s[...] = _dot_sel(gc, ef_ref[...])

    ri = lax.broadcasted_iota(jnp.int32, (PAIR, PAIR), 0)
    ci = lax.broadcasted_iota(jnp.int32, (PAIR, PAIR), 1)
    same = (ri // DN_CHUNK) == (ci // DN_CHUNK)
    incl = same & (ri >= ci)
    strict = same & (ri > ci)
    eye = (ri == ci).astype(F32)
    first = lax.broadcasted_iota(jnp.int32, (DN_CHUNK, LANES), 1) < DN_DK

    def stack(a):
        return jnp.concatenate([jnp.where(first, a, 0.0), jnp.where(first, 0.0, a)], axis=0)

    def fold(a):
        return a[0:DN_CHUNK] + a[DN_CHUNK:PAIR]

    rows = lambda c: slice(DN_CHUNK * c, DN_CHUNK * (c + 1))
    lanes = lambda p: slice(LANES * p, LANES * (p + 1))

    def solve_slots(slots):
        chains = [(c, p) for c in slots for p in range(N_PAIRS)]
        kst = [stack(k_s[rows(c), lanes(p)]).astype(BF16) for c, p in chains]
        a_mat = [_dot_nt(stack(kb_s[rows(c), lanes(p)]).astype(BF16), ks) for (c, p), ks in zip(chains, kst)]
        qk = [_dot_nt(stack(q_s[rows(c), lanes(p)]).astype(BF16), ks) for (c, p), ks in zip(chains, kst)]
        dec = []
        for c, p in chains:
            g_col = jnp.concatenate([gf_s[rows(c), lanes(2 * p)], gf_s[rows(c), lanes(2 * p + 1)]], axis=0)
            diff = g_col - g_col.T
            dec.append(jnp.where(incl, jnp.exp(jnp.where(incl, diff, 0.0)), 0.0))
        pw = [jnp.where(strict, -(a * d), 0.0) for a, d in zip(a_mat, dec)]
        for (c, p), x, d in zip(chains, qk, dec):
            qkd_s[c * N_PAIRS + p] = (x * d).astype(BF16)
        t_inv = [eye + n for n in pw]
        for _ in range(NEUMANN_STEPS):
            pw16 = [x.astype(BF16) for x in pw]
            pw = [_dot(x, x) for x in pw16]
            t_inv = [t + _dot(t.astype(BF16), x.astype(BF16)) for t, x in zip(t_inv, pw)]
        for (c, p), t in zip(chains, t_inv):
            rhs = jnp.concatenate([stack(vb_s[rows(c), lanes(p)]), stack(kbe_s[rows(c), lanes(p)])], axis=1)
            uw = fold(_dot(t.astype(BF16), rhs.astype(BF16)))
            u_s[rows(c), lanes(p)] = uw[:, 0:LANES]
            w_s[rows(c), lanes(p)] = uw[:, LANES:2 * LANES]
            gcl = gcl_s[rows(c), lanes(p)]
            k_dec = k_s[rows(c), lanes(p)] * jnp.exp(gcl[DN_CHUNK - 1:DN_CHUNK, :] - gcl)
            kdt_s[c * N_PAIRS + p] = k_dec.T.astype(BF16)

    for c0 in range(0, n_slots, SLOT_GROUP):
        solve_slots(range(c0, min(c0 + SLOT_GROUP, n_slots)))

    pairs = range(N_PAIRS)
    for c in range(n_slots):
        sq = c // cps
        sps = [sp_ref[sq, p] for p in pairs]
        wq = [jnp.concatenate([w_s[rows(c), lanes(p)], qe_s[rows(c), lanes(p)]], axis=0).astype(BF16)
              for p in pairs]
        res = [_dot(x, sp.astype(BF16)) for x, sp in zip(wq, sps)]
        v_new = [u_s[rows(c), lanes(p)] - res[p][0:DN_CHUNK] for p in pairs]
        intra = [fold(_dot(qkd_s[c * N_PAIRS + p], stack(v_new[p]).astype(BF16))) for p in pairs]
        upd = [_dot(kdt_s[c * N_PAIRS + p], v_new[p].astype(BF16)) for p in pairs]
        for p in pairs:
            o_s[rows(c), lanes(p)] = res[p][DN_CHUNK:PAIR] + intra[p]
            g_last = gcl_s[DN_CHUNK * (c + 1) - 1:DN_CHUNK * (c + 1), lanes(p)]
            sp_ref[sq, p] = sps[p] * jnp.exp(g_last) + jnp.where(same, upd[p], 0.0)

    o = o_s[...]
    y = o * lax.rsqrt(_dot_sel(o * o, bd) * (1.0 / DN_DV) + EPS) * gdn_ref[...]
    for s in range(ns):
        zs = z_ref[s * tl:(s + 1) * tl, :]
        y_ref[s * tl:(s + 1) * tl, :] = (y[s * seq_rows:s * seq_rows + tl] * _silu(zs)).astype(y_ref.dtype)

    @pl.when(li == n_l - 1)
    def _():
        spo_ref[...] = sp_ref[...]


def _delta(h_qkv, z_dn, h_ba, cbuf8, cw8, alog_row, dtb_row, sp0, gdn_row, b, l, tl, ns, y_dtype):
    n_l = l // tl
    assert ns == 1 or n_l == 1
    n_slots = ns * (max(tl, DN_CHUNK) // DN_CHUNK)
    r = n_slots * DN_CHUNK
    consts = _delta_consts(r)
    row = lambda w: pl.BlockSpec((ns * tl, w), lambda bi, li: (bi * n_l + li, 0))
    cst = pl.BlockSpec((ns, CONV_HALO, CONV_CH), lambda bi, li: (bi, 0, 0))
    spb = pl.BlockSpec((ns, N_PAIRS, PAIR, PAIR), lambda bi, li: (bi, 0, 0, 0))
    wide = lambda n: pltpu.VMEM((r, n), F32)
    return pl.pallas_call(
        functools.partial(_delta_kernel, ns=ns, tl=tl, n_l=n_l),
        grid=(b // ns, n_l),
        in_specs=[row(CONV_CH), row(DN_WIDTH), row(LANES), cst, _const_spec(cw8.shape),
                  _const_spec(alog_row.shape), _const_spec(dtb_row.shape), spb, _const_spec(gdn_row.shape)]
                 + [_const_spec(c.shape) for c in consts],
        out_specs=[row(DN_WIDTH), cst, spb],
        out_shape=[jax.ShapeDtypeStruct((b * l, DN_WIDTH), y_dtype),
                   jax.ShapeDtypeStruct((b, CONV_HALO, CONV_CH), F32),
                   jax.ShapeDtypeStruct((b, N_PAIRS, PAIR, PAIR), F32)],
        scratch_shapes=[pltpu.VMEM((ns, CONV_HALO + tl, CONV_CH), F32), pltpu.VMEM((ns, N_PAIRS, PAIR, PAIR), F32)]
                       + [wide(DN_WIDTH)] * 7 + [wide(DN_HEADS * LANES)] + [wide(DN_WIDTH)] * 3
                       + [pltpu.VMEM((n_slots * N_PAIRS, PAIR, PAIR), BF16),
                          pltpu.VMEM((n_slots * N_PAIRS, PAIR, DN_CHUNK), BF16)],
        compiler_params=_cparams("parallel", "arbitrary"),
        name="gated_deltanet",
    )(h_qkv, z_dn, h_ba, cbuf8, cw8, alog_row, dtb_row, sp0, gdn_row, *consts)


def _merge_kernel(x_ref, ya_ref, yb_ref, yc_ref, g_ref, wa_ref, wb_ref, wc_ref, wo_ref, fg_ref, o_ref, *, final):
    merged = None
    for j, (y_ref, w_ref) in enumerate(((ya_ref, wa_ref), (yb_ref, wb_ref), (yc_ref, wc_ref))):
        br = _dot(y_ref[...].astype(BF16), w_ref[...])
        t = jax.nn.sigmoid(g_ref[:, D_MODEL * j:D_MODEL * (j + 1)]) * br
        merged = t if merged is None else merged + t
    x = x_ref[...] + _dot(merged.astype(BF16), wo_ref[...])
    o_ref[...] = _rms(x, fg_ref[...]) if final else x


def _merge(x2d, ya, yb, yc, h_gate, wa, wb, wc, wo, final_g, tm, final):
    t = x2d.shape[0]
    row = lambda w: pl.BlockSpec((tm, w), lambda i: (i, 0))
    return pl.pallas_call(
        functools.partial(_merge_kernel, final=final),
        grid=(t // tm,),
        in_specs=[row(D_MODEL), row(POOL_WIDTH), row(MLA_WIDTH), row(DN_WIDTH), row(N_BRANCH * D_MODEL)]
                 + [_const_spec(w.shape) for w in (wa, wb, wc, wo, final_g)],
        out_specs=row(D_MODEL),
        out_shape=jax.ShapeDtypeStruct((t, D_MODEL), F32),
        compiler_params=_cparams("parallel"),
        name="merge_out",
    )(x2d, ya, yb, yc, h_gate, wa, wb, wc, wo, final_g)


def _layer_weights(l, norm_g, w_in, pool_mix, pool_scale, q_norm_g, w_uq, kv_norm_g, w_uk, w_uv,
                   conv_w, a_log, dt_bias, dn_norm_g, w_br_pool, w_br_mla, w_br_dn, w_out):
    wi = w_in[l]
    d = wi.shape[0]
    o = np.cumsum((0, POOL_WIDTH, POOL_WIDTH, Q_LORA, KV_LORA, QK_ROPE, MLA_WIDTH, CONV_CH, DN_WIDTH,
                   DN_HEADS, DN_HEADS, N_BRANCH * D_MODEL))
    seg = lambda i: wi[:, o[i]:o[i + 1]]
    kr_blk = jnp.zeros((d, HEAD_BLOCK), F32).at[:, ROPE_LANE0:ROPE_LANE0 + QK_ROPE].set(seg(4))
    ba_blk = jnp.zeros((d, LANES), F32).at[:, 0:2 * DN_HEADS].set(wi[:, o[8]:o[10]])
    w_segs = [seg(0), seg(1), seg(2), seg(3), kr_blk, seg(5), seg(6), seg(7), ba_blk, seg(10)]
    w_segs = [w.astype(BF16) for w in w_segs]

    dq = QK_NOPE + QK_ROPE
    wuq = w_uq[l].reshape(Q_LORA, MLA_HEADS, dq)
    wuq_p = jnp.pad(wuq, ((0, 0), (0, 0), (0, HEAD_BLOCK - dq))).reshape(Q_LORA, -1).astype(BF16)
    wuk_p = jnp.pad(w_uk[l], ((0, 0), (0, 0), (0, HEAD_BLOCK - QK_NOPE))).reshape(KV_LORA, -1).astype(BF16)
    wuv_f = w_uv[l].reshape(KV_LORA, MLA_WIDTH).astype(BF16)
    wcat = jnp.zeros((MLA_HEADS, HEAD_BLOCK, KV_LORA + LANES), F32)
    wcat = wcat.at[:, 0:QK_NOPE, 0:KV_LORA].set(jnp.transpose(w_uk[l], (1, 2, 0)))
    sel = jnp.eye(QK_ROPE, dtype=F32)
    wcat = wcat.at[:, ROPE_LANE0:ROPE_LANE0 + QK_ROPE, KV_LORA:KV_LORA + QK_ROPE].set(sel)
    lanes16 = slice(DN_HEADS, 2 * DN_HEADS)
    return dict(
        norm_g=norm_g[l][None, :], w_segs=w_segs,
        pool_mix=pool_mix[l].astype(BF16), pool_scale=pool_scale[l][None, :],
        q_norm_g=q_norm_g[l][None, :], wuq_p=wuq_p, kv_norm_g=kv_norm_g[l][None, :],
        wuk_p=wuk_p, wuv_f=wuv_f, wcat=wcat.astype(BF16),
        cw8=jnp.pad(conv_w[l], ((0, SUBLANES - CONV_W), (0, 0))),
        alog_row=jnp.zeros((1, LANES), F32).at[0, lanes16].set(a_log[l]),
        dtb_row=jnp.zeros((1, LANES), F32).at[0, lanes16].set(dt_bias[l]),
        gdn_row=jnp.tile(dn_norm_g[l], DN_HEADS)[None, :],
        wa=w_br_pool[l].astype(BF16), wb=w_br_mla[l].astype(BF16), wc=w_br_dn[l].astype(BF16),
        wo=w_out[l].astype(BF16))


def _tile(n, target):
    t = min(n, target)
    while n % t:
        t -= SUBLANES
    return t


def _group_layer(x2d, b, l, start, w, tabs, final_g, final, pool_buf, conv_buf, sp0, paged):
    t = b * l
    small = l < 2 * SUBLANES
    y_dtype = F32 if small else BF16
    h_pool, z_pool, h_q, h_kv, h_kr, z_mla, h_qkv, z_dn, h_ba, h_gate = _inproj(
        x2d, w["norm_g"], w["w_segs"], _tile(t, 256))

    ya, pool_st = _pool(h_pool, z_pool, pool_buf, w["pool_mix"], w["pool_scale"], b, l, _tile(l, 512), start, y_dtype)

    tm = _tile(l, 512)
    if paged is None:
        ckv, kr_blk, q, k, v = _mla_prep(h_q, h_kv, h_kr, tabs, w["q_norm_g"], w["wuq_p"], w["kv_norm_g"],
                                         (w["wuk_p"], w["wuv_f"]), b, l, tm, absorbed=False)
        k_r = kr_blk[:, ROPE_LANE0:ROPE_LANE0 + QK_ROPE]
        yb = _attn_prompt(q, k, v, z_mla, b, l, _tile(l, 512))
    else:
        page_table, cache_kv, cache_krt, layer = paged
        ckv, kr_blk, qcat = _mla_prep(h_q, h_kv, h_kr, tabs, w["q_norm_g"], w["wuq_p"], w["kv_norm_g"],
                                      (w["wcat"],), b, l, tm, absorbed=True)
        k_r = kr_blk[:, ROPE_LANE0:ROPE_LANE0 + QK_ROPE]
        wq = qcat.shape[-1] // MLA_HEADS
        qcat = qcat.reshape(b, l, MLA_HEADS, wq).transpose(0, 2, 1, 3).reshape(b, MLA_HEADS * l, wq)
        n_pages = page_table.shape[1]
        n_grp = 16
        while n_pages % n_grp:
            n_grp //= 2
        yb = _attn_paged(page_table, qcat, cache_kv, cache_krt, layer, ckv.reshape(b, l, KV_LORA),
                         k_r.reshape(b, l, QK_ROPE), w["wuv_f"], z_mla.reshape(b, l, MLA_WIDTH), n_grp)
        yb = yb.reshape(t, MLA_WIDTH)

    if l >= DN_CHUNK:
        tl, ns = _tile(l, 256), 1
    else:
        tl, ns = l, SAMPLE_SEQS_PER_STEP
        while b % ns:
            ns //= 2
    yc, conv_st, sp = _delta(h_qkv, z_dn, h_ba, conv_buf, w["cw8"], w["alog_row"], w["dtb_row"], sp0,
                             w["gdn_row"], b, l, tl, ns, y_dtype)

    x_out = _merge(x2d, ya, yb, yc, h_gate, w["wa"], w["wb"], w["wc"], w["wo"], final_g, _tile(t, 256), final)
    return x_out, ckv, k_r, pool_st[:, 1:], conv_st[:, CONV_HALO - (CONV_W - 1):], sp


def _pairs_to_blockdiag(s):
    b = s.shape[0]
    s = s.reshape(b, N_PAIRS, 2, DN_DK, DN_DV)
    z = jnp.zeros_like(s[:, :, 0])
    top = jnp.concatenate([s[:, :, 0], z], axis=-1)
    bot = jnp.concatenate([z, s[:, :, 1]], axis=-1)
    return jnp.concatenate([top, bot], axis=-2)


def _blockdiag_to_pairs(sp):
    b = sp.shape[0]
    s = jnp.stack([sp[:, :, :DN_DK, :DN_DV], sp[:, :, DN_DK:, DN_DV:]], axis=2)
    return s.reshape(b, DN_HEADS, DN_DK, DN_DV)


def kernel(x_prompt, x_sample, cache_kv_latent, cache_k_rope, state_pool, state_conv, state_delta,
           page_table, norm_g, w_in, pool_mix, pool_scale, q_norm_g, w_uq, kv_norm_g, w_uk, w_uv,
           conv_w, a_log, dt_bias, dn_norm_g, w_br_pool, w_br_mla, w_br_dn, w_out, final_norm_g):
    bp, lp, d = x_prompt.shape
    db, ls, _ = x_sample.shape
    depth = w_in.shape[0]
    past_len = page_table.shape[1] * PAGE_SIZE
    final_g = final_norm_g[None, :]
    cache_krt = jnp.swapaxes(cache_k_rope, 2, 3)

    tabs_p = _rope_tables(lp, _tile(lp, 512), 0)
    tabs_s = _rope_tables(ls, ls, past_len)
    zero_pool = jnp.zeros((bp, POOL_HALO, POOL_WIDTH), F32)
    zero_conv = jnp.zeros((bp, CONV_HALO, CONV_CH), F32)
    zero_sp = jnp.zeros((bp, N_PAIRS, PAIR, PAIR), F32)

    xp = x_prompt.reshape(bp * lp, d)
    xs = x_sample.reshape(db * ls, d)
    outs_p, outs_s = [], []
    for l in range(depth):
        w = _layer_weights(l, norm_g, w_in, pool_mix, pool_scale, q_norm_g, w_uq, kv_norm_g, w_uk, w_uv,
                           conv_w, a_log, dt_bias, dn_norm_g, w_br_pool, w_br_mla, w_br_dn, w_out)
        final = l == depth - 1
        xp, ckv, kr, pst, cst, sp = _group_layer(xp, bp, lp, 0, w, tabs_p, final_g, final,
                                                 zero_pool, zero_conv, zero_sp, None)
        outs_p.append((ckv.reshape(bp, lp, KV_LORA), kr.reshape(bp, lp, QK_ROPE), pst, cst,
                       _blockdiag_to_pairs(sp)))
        pool_buf = jnp.pad(state_pool[l], ((0, 0), (POOL_HALO - POOL_BUF, 0), (0, 0)))
        conv_buf = jnp.pad(state_conv[l], ((0, 0), (CONV_HALO - (CONV_W - 1), 0), (0, 0)))
        xs, ckv, kr, pst, cst, sp = _group_layer(xs, db, ls, past_len, w, tabs_s, final_g, final,
                                                 pool_buf, conv_buf, _pairs_to_blockdiag(state_delta[l]),
                                                 (page_table, cache_kv_latent, cache_krt, l))
        outs_s.append((ckv.reshape(db, ls, KV_LORA), kr.reshape(db, ls, QK_ROPE), pst, cst,
                       _blockdiag_to_pairs(sp)))
    stack = lambda outs, i: jnp.stack([o[i] for o in outs])
    return (xp.reshape(bp, lp, d), xs.reshape(db, ls, d),
            *(stack(outs_p, i) for i in range(5)), *(stack(outs_s, i) for i in range(5)))
```

```python
import functools

import numpy as np
import jax
import jax.numpy as jnp
from jax import lax
from jax.experimental import pallas as pl
from jax.experimental.pallas import tpu as pltpu

F32 = jnp.float32
BF16 = jnp.bfloat16

D_MODEL = 1024
EPS = 1e-6
POOL_WINDOWS = (2, 4, 8, 16)
POOL_GROUP_DIM = 128
POOL_WIDTH = 512
POOL_BUF = 15
MLA_HEADS = 8
QK_NOPE = 64
QK_ROPE = 32
V_HEAD = 64
Q_LORA = 384
KV_LORA = 256
MLA_WIDTH = 512
MLA_SCALE = (QK_NOPE + QK_ROPE) ** -0.5
ROPE_THETA = 10000.0
DN_HEADS = 8
DN_DK = 64
DN_DV = 64
DN_WIDTH = 512
CONV_W = 4
CONV_CH = 1536
DN_CHUNK = 64
PAGE_SIZE = 128
N_BRANCH = 3

LANES = 128
SUBLANES = 8
HEAD_BLOCK = LANES
ROPE_LANE0 = QK_NOPE
ROPE_HALF = QK_ROPE // 2
POOL_HALO = 16
CONV_HALO = 8
VMEM_LIMIT = 48 * 1024 * 1024
NEG = -1e30
LOG2E = 1.4426950408889634
Q_SCALE = MLA_SCALE * LOG2E
PAGES_PER_GROUP = 32
PAGED_ROW_CHUNK = 16


def _cparams(*sem):
    return pltpu.CompilerParams(dimension_semantics=sem, vmem_limit_bytes=VMEM_LIMIT)


def _const_spec(shape):
    nd = len(shape)
    return pl.BlockSpec(shape, lambda *_: (0,) * nd, pipeline_mode=pl.Buffered(1))


def _dot(a, b):
    return jnp.dot(a, b, preferred_element_type=F32)


def _dot_nt(a, b):
    return lax.dot_general(a, b, (((1,), (1,)), ((), ())), preferred_element_type=F32)


def _silu(x):
    return x * jax.nn.sigmoid(x)


def _rms(x, g):
    return x * lax.rsqrt(jnp.mean(x * x, axis=-1, keepdims=True) + EPS) * g


IN_SEG_WIDTHS = (POOL_WIDTH, POOL_WIDTH, Q_LORA, KV_LORA, HEAD_BLOCK, MLA_WIDTH,
                 CONV_CH, DN_WIDTH, LANES, N_BRANCH * D_MODEL)
IN_DOT_CHUNK = 512


def _inproj_kernel(x_ref, g_ref, *refs):
    n = len(IN_SEG_WIDTHS)
    w_refs, o_refs = refs[:n], refs[n:]
    xn = _rms(x_ref[...], g_ref[...]).astype(BF16)
    for w_ref, o_ref, width in zip(w_refs, o_refs, IN_SEG_WIDTHS):
        for c0 in range(0, width, IN_DOT_CHUNK):
            c1 = min(c0 + IN_DOT_CHUNK, width)
            o_ref[:, c0:c1] = _dot_nt(xn, w_ref[c0:c1, :])


def _inproj(x2d, norm_g, w_segs, tm):
    t = x2d.shape[0]
    row = lambda w: pl.BlockSpec((tm, w), lambda i: (i, 0))
    return pl.pallas_call(
        _inproj_kernel,
        grid=(t // tm,),
        in_specs=[row(D_MODEL), _const_spec((1, D_MODEL))] + [_const_spec(w.shape) for w in w_segs],
        out_specs=[row(w) for w in IN_SEG_WIDTHS],
        out_shape=[jax.ShapeDtypeStruct((t, w), F32) for w in IN_SEG_WIDTHS],
        compiler_params=_cparams("parallel"),
        name="inproj",
    )(x2d, norm_g, *w_segs)


def _pool_kernel(u_ref, z_ref, buf_ref, mix_ref, scale_ref, y_ref, st_ref, xx_ref, *, tl, start, n_l):
    li = pl.program_id(1)

    @pl.when(li == 0)
    def _():
        xx_ref[0:POOL_HALO, :] = buf_ref[...]

    u = u_ref[...]
    xx_ref[POOL_HALO:POOL_HALO + tl, :] = u
    row = lax.broadcasted_iota(jnp.int32, (tl, 1), 0)
    pos1 = start + li * tl + row + 1
    for gi, w in enumerate(POOL_WINDOWS):
        lanes = slice(POOL_GROUP_DIM * gi, POOL_GROUP_DIM * (gi + 1))
        s = u[:, lanes]
        for o in range(1, w):
            s = s + xx_ref[POOL_HALO - o:POOL_HALO - o + tl, lanes]
        cnt = jnp.minimum(pos1, w).astype(F32)
        d = s / cnt - u[:, lanes]
        y = _dot(d.astype(BF16), mix_ref[gi]) * scale_ref[:, lanes]
        y_ref[:, lanes] = (y * _silu(z_ref[:, lanes])).astype(y_ref.dtype)

    tail = xx_ref[tl:tl + POOL_HALO, :]

    @pl.when(li == n_l - 1)
    def _():
        st_ref[...] = tail

    xx_ref[0:POOL_HALO, :] = tail


def _pool(h_pool, z_pool, buf16, mix, scale, b, l, tl, start, y_dtype):
    n_l = l // tl
    row = lambda w: pl.BlockSpec((tl, w), lambda bi, li: (bi * n_l + li, 0))
    st = pl.BlockSpec((None, POOL_HALO, POOL_WIDTH), lambda bi, li: (bi, 0, 0))
    return pl.pallas_call(
        functools.partial(_pool_kernel, tl=tl, start=start, n_l=n_l),
        grid=(b, n_l),
        in_specs=[row(POOL_WIDTH), row(POOL_WIDTH), st, _const_spec(mix.shape), _const_spec(scale.shape)],
        out_specs=[row(POOL_WIDTH), st],
        out_shape=[jax.ShapeDtypeStruct((b * l, POOL_WIDTH), y_dtype),
                   jax.ShapeDtypeStruct((b, POOL_HALO, POOL_WIDTH), F32)],
        scratch_shapes=[pltpu.VMEM((POOL_HALO + tl, POOL_WIDTH), F32)],
        compiler_params=_cparams("parallel", "arbitrary"),
        name="pool_mixer",
    )(h_pool, z_pool, buf16, mix, scale)


def _rope_table_kernel(inv_ref, c_ref, s1_ref, s2_ref, *, tl, start):
    i = pl.program_id(0)
    shape = (tl, LANES)
    pos = (start + i * tl + lax.broadcasted_iota(jnp.int32, shape, 0)).astype(F32)
    lane = lax.broadcasted_iota(jnp.int32, shape, 1)
    ang = pos * inv_ref[...]
    cos, sin = jnp.cos(ang), jnp.sin(ang)
    first = (lane >= ROPE_LANE0) & (lane < ROPE_LANE0 + ROPE_HALF)
    second = (lane >= ROPE_LANE0 + ROPE_HALF) & (lane < ROPE_LANE0 + QK_ROPE)
    c_ref[...] = jnp.where(first | second, cos, 1.0)
    s1_ref[...] = jnp.where(first, -sin, 0.0)
    s2_ref[...] = jnp.where(second, sin, 0.0)


def _rope_tables(l, tl, start):
    half = ROPE_HALF
    inv = jnp.power(ROPE_THETA, -jnp.arange(half, dtype=F32) / half)
    inv_lane = jnp.zeros((1, LANES), F32)
    inv_lane = inv_lane.at[0, ROPE_LANE0:ROPE_LANE0 + half].set(inv)
    inv_lane = inv_lane.at[0, ROPE_LANE0 + half:ROPE_LANE0 + QK_ROPE].set(inv)
    blk = pl.BlockSpec((tl, LANES), lambda i: (i, 0))
    return pl.pallas_call(
        functools.partial(_rope_table_kernel, tl=tl, start=start),
        grid=(l // tl,),
        in_specs=[_const_spec((1, LANES))],
        out_specs=[blk, blk, blk],
        out_shape=[jax.ShapeDtypeStruct((l, LANES), F32)] * 3,
        compiler_params=_cparams("parallel"),
        name="rope_tables",
    )(inv_lane)


def _rope(x, c, s1, s2):
    return x * c + pltpu.roll(x, LANES - ROPE_HALF, 1) * s1 + pltpu.roll(x, ROPE_HALF, 1) * s2


def _mla_prep_kernel(hq_ref, hkv_ref, hkr_ref, c_ref, s1_ref, s2_ref, qg_ref, wuq_ref, kvg_ref, *rest,
                     absorbed):
    c, s1, s2 = c_ref[...], s1_ref[...], s2_ref[...]
    cq = _rms(hq_ref[...], qg_ref[...]).astype(BF16)
    q = _dot(cq, wuq_ref[...])
    ckv = _rms(hkv_ref[...], kvg_ref[...])
    kr = _rope(hkr_ref[...], c, s1, s2)
    if absorbed:
        wcat_ref, ckv_ref, kr_ref, qcat_ref = rest
    else:
        wuk_ref, wuv_ref, ckv_ref, kr_ref, q_ref, k_ref, v_ref = rest
        ckv16 = ckv.astype(BF16)
        knope = _dot(ckv16, wuk_ref[...])
        v_ref[...] = _dot(ckv16, wuv_ref[...]).astype(BF16)
    ckv_ref[...] = ckv
    kr_ref[...] = kr
    for h in range(MLA_HEADS):
        blk = slice(HEAD_BLOCK * h, HEAD_BLOCK * (h + 1))
        qh = (_rope(q[:, blk], c, s1, s2) * Q_SCALE).astype(BF16)
        if absorbed:
            w = wcat_ref.shape[-1]
            qcat_ref[:, w * h:w * (h + 1)] = _dot(qh, wcat_ref[h]).astype(BF16)
        else:
            q_ref[:, blk] = qh
            k_ref[:, blk] = (knope[:, blk] + kr).astype(BF16)


def _mla_prep(h_q, h_kv, h_kr, tabs, q_norm_g, wuq_p, kv_norm_g, extra_w, b, l, tm, absorbed):
    n_l = l // tm
    t = b * l
    row = lambda w: pl.BlockSpec((tm, w), lambda bi, li: (bi * n_l + li, 0))
    tab = pl.BlockSpec((tm, LANES), lambda bi, li: (li, 0))
    hw = MLA_HEADS * HEAD_BLOCK
    in_specs = [row(Q_LORA), row(KV_LORA), row(HEAD_BLOCK), tab, tab, tab,
                _const_spec(q_norm_g.shape), _const_spec(wuq_p.shape), _const_spec(kv_norm_g.shape)]
    in_specs += [_const_spec(w.shape) for w in extra_w]
    out_specs = [row(KV_LORA), row(HEAD_BLOCK)]
    out_shape = [jax.ShapeDtypeStruct((t, KV_LORA), F32), jax.ShapeDtypeStruct((t, HEAD_BLOCK), F32)]
    if absorbed:
        wc = MLA_HEADS * extra_w[0].shape[-1]
        out_specs += [row(wc)]
        out_shape += [jax.ShapeDtypeStruct((t, wc), BF16)]
    else:
        out_specs += [row(hw), row(hw), row(MLA_WIDTH)]
        out_shape += [jax.ShapeDtypeStruct((t, hw), BF16), jax.ShapeDtypeStruct((t, hw), BF16),
                      jax.ShapeDtypeStruct((t, MLA_WIDTH), BF16)]
    return pl.pallas_call(
        functools.partial(_mla_prep_kernel, absorbed=absorbed),
        grid=(b, n_l),
        in_specs=in_specs, out_specs=out_specs, out_shape=out_shape,
        compiler_params=_cparams("parallel", "parallel"),
        name="mla_prep_absorbed" if absorbed else "mla_prep",
    )(h_q, h_kv, h_kr, *tabs, q_norm_g, wuq_p, kv_norm_g, *extra_w)


ATTN_ROW_CHUNK = 32


def _softmax_update(m_b, l_p, s, row_chunk):
    r, n = s.shape[0], s.shape[1] // LANES
    cols = [slice(LANES * c, LANES * (c + 1)) for c in range(n)]
    folded = s[:, cols[0]]
    for cb in cols[1:]:
        folded = jnp.maximum(folded, s[:, cb])
    mx = jnp.max(folded, axis=-1, keepdims=True)
    m_new = jnp.maximum(m_b, jnp.broadcast_to(mx, m_b.shape))
    a_b = jnp.exp2(m_b - m_new)
    l_rows, p_rows = [], []
    for c0 in range(0, r, row_chunk):
        rows = slice(c0, c0 + row_chunk)
        m_c = m_new[rows]
        pieces = [jnp.exp2(s[rows, cb] - m_c) for cb in cols]
        tot = pieces[0]
        for pc in pieces[1:]:
            tot = tot + pc
        l_rows.append(a_b[rows] * l_p[rows] + tot)
        p_rows.append(jnp.concatenate([pc.astype(BF16) for pc in pieces], axis=1))
    cat = lambda parts: parts[0] if len(parts) == 1 else jnp.concatenate(parts, axis=0)
    return m_new, cat(l_rows), a_b, cat(p_rows)


def _attn_kernel(q_ref, k_ref, v_ref, z_ref, y_ref, *, ta):
    qi = pl.program_id(1)
    rc = min(ATTN_ROW_CHUNK, ta)
    low_half = lax.broadcasted_iota(jnp.int32, (ta, LANES), 1) < V_HEAD
    causal = (lax.broadcasted_iota(jnp.int32, (ta, ta), 1)
              <= lax.broadcasted_iota(jnp.int32, (ta, ta), 0))
    for p in range(MLA_HEADS // 2):
        pair = slice(LANES * p, LANES * (p + 1))
        blks = [slice(HEAD_BLOCK * h, HEAD_BLOCK * (h + 1)) for h in (2 * p, 2 * p + 1)]
        qhs = [q_ref[:, blk] for blk in blks]

        def step(kb, carry, masked, pair=pair, blks=blks, qhs=qhs):
            r0 = pl.multiple_of(kb * ta, ta)
            vv = v_ref[pl.ds(r0, ta), pair]
            scores = [_dot_nt(qh, k_ref[pl.ds(r0, ta), blk]) for qh, blk in zip(qhs, blks)]
            out = []
            for (m_b, l_p, acc), s in zip(carry, scores):
                if masked:
                    s = jnp.where(causal, s, NEG)
                m_b, l_p, a_b, pm = _softmax_update(m_b, l_p, s, rc)
                out.append((m_b, l_p, a_b * acc + _dot(pm, vv)))
            return tuple(out)

        init = (jnp.full((ta, LANES), NEG, F32), jnp.zeros((ta, LANES), F32), jnp.zeros((ta, LANES), F32))
        carry = lax.fori_loop(0, qi, functools.partial(step, masked=False), (init, init))
        (_, l0, acc0), (_, l1, acc1) = step(qi, carry, True)
        den = lambda l_p: jnp.sum(l_p, axis=-1, keepdims=True)
        o = jnp.where(low_half, acc0 / den(l0), acc1 / den(l1))
        y_ref[:, pair] = (o * _silu(z_ref[:, pair])).astype(y_ref.dtype)


def _attn_prompt(q, k, v, z_mla, b, l, ta):
    n_q = l // ta
    hw = MLA_HEADS * HEAD_BLOCK
    row = lambda w: pl.BlockSpec((ta, w), lambda bi, qi: (bi * n_q + qi, 0))
    seq = lambda w: pl.BlockSpec((l, w), lambda bi, qi: (bi, 0))
    return pl.pallas_call(
        functools.partial(_attn_kernel, ta=ta),
        grid=(b, n_q),
        in_specs=[row(hw), seq(hw), seq(MLA_WIDTH), row(MLA_WIDTH)],
        out_specs=row(MLA_WIDTH),
        out_shape=jax.ShapeDtypeStruct((b * l, MLA_WIDTH), BF16),
        compiler_params=_cparams("parallel", "arbitrary"),
        name="mla_attention_prompt",
    )(q, k, v, z_mla)


def _attn_paged_kernel(pt_ref, qcat_ref, nkv_ref, nkr_ref, wuv_ref, z_ref, kv_hbm, krt_hbm, y_ref,
                       kvbuf, krbuf, sem, *, layer, n_pages, n_grp, l_new):
    b = pl.program_id(0)
    n_seq = pl.num_programs(0)
    n_groups = n_pages // n_grp
    rows = MLA_HEADS * l_new

    def group_copies(seq, j):
        slot = j % 2
        cps = []
        for g in range(n_grp):
            page = pt_ref[seq * n_pages + j * n_grp + g]
            keys = pl.ds(PAGE_SIZE * g, PAGE_SIZE)
            cps.append(pltpu.make_async_copy(kv_hbm.at[layer, page], kvbuf.at[slot, keys, :], sem.at[0, slot]))
            cps.append(pltpu.make_async_copy(krt_hbm.at[layer, page], krbuf.at[slot, :, keys], sem.at[1, slot]))
        return cps

    def start_group(seq, j):
        for cp in group_copies(seq, j):
            cp.start()

    def wait_group(seq, j):
        for cp in group_copies(seq, j):
            cp.wait()

    @pl.when(b == 0)
    def _():
        start_group(b, 0)

    qc = qcat_ref[...]
    ql, qr = qc[:, :KV_LORA], qc[:, KV_LORA:KV_LORA + QK_ROPE]

    def update(state, s, kv16):
        m_b, l_p, acc = state
        m_b, l_p, a_b, pm = _softmax_update(m_b, l_p, s, PAGED_ROW_CHUNK)
        a_wide = jnp.concatenate([a_b] * (KV_LORA // LANES), axis=1)
        return m_b, l_p, a_wide * acc + _dot(pm, kv16)

    state = (jnp.full((rows, LANES), NEG, F32), jnp.zeros((rows, LANES), F32), jnp.zeros((rows, KV_LORA), F32))
    half_keys = n_grp * PAGE_SIZE // 2
    for j in range(n_groups):
        if j + 1 < n_groups:
            start_group(b, j + 1)
        else:
            @pl.when(b + 1 < n_seq)
            def _():
                start_group(b + 1, 0)
        wait_group(b, j)
        slot = j % 2
        halves = []
        for hf in range(2):
            keys = slice(half_keys * hf, half_keys * (hf + 1))
            kv16 = kvbuf[slot, keys, :].astype(BF16)
            kr16 = krbuf[slot, :, keys].astype(BF16)
            halves.append((_dot_nt(ql, kv16) + _dot(qr, kr16), kv16))
        for s, kv16 in halves:
            state = update(state, s, kv16)

    pad = PAGE_SIZE - l_new
    nkv = jnp.concatenate([nkv_ref[...], jnp.zeros((pad, KV_LORA), F32)], axis=0).astype(BF16)
    nkr = jnp.concatenate([nkr_ref[...], jnp.zeros((pad, QK_ROPE), F32)], axis=0).astype(BF16)
    tok = lax.broadcasted_iota(jnp.int32, (rows, PAGE_SIZE), 0) % l_new
    key = lax.broadcasted_iota(jnp.int32, (rows, PAGE_SIZE), 1)
    _, l_p, acc = update(state, jnp.where(key <= tok, _dot_nt(ql, nkv) + _dot_nt(qr, nkr), NEG), nkv)
    o = (acc / jnp.sum(l_p, axis=-1, keepdims=True)).astype(BF16)
    full = _dot(o, wuv_ref[...])
    rh = lax.broadcasted_iota(jnp.int32, full.shape, 0) // l_new
    ch = lax.broadcasted_iota(jnp.int32, full.shape, 1) // V_HEAD
    full = jnp.where(rh == ch, full, 0.0)
    out = full[0:l_new]
    for h in range(1, MLA_HEADS):
        out = out + full[l_new * h:l_new * (h + 1)]
    y_ref[...] = (out * _silu(z_ref[...])).astype(y_ref.dtype)


def _attn_paged(page_table, qcat, cache_kv, cache_krt, layer, new_kv, new_kr, wuv, z_mla, n_grp):
    db, n_pages = page_table.shape
    l_new = new_kv.shape[1]
    rows = MLA_HEADS * l_new
    assert n_pages % n_grp == 0 and (n_pages // n_grp) % 2 == 0
    wq = qcat.shape[-1]
    per_b = lambda r, w: pl.BlockSpec((None, r, w), lambda b, pt: (b, 0, 0))
    hbm = pl.BlockSpec(memory_space=pl.ANY)
    grid_spec = pltpu.PrefetchScalarGridSpec(
        num_scalar_prefetch=1, grid=(db,),
        in_specs=[per_b(rows, wq), per_b(l_new, KV_LORA), per_b(l_new, QK_ROPE),
                  pl.BlockSpec(wuv.shape, lambda b, pt: (0, 0)), per_b(l_new, MLA_WIDTH), hbm, hbm],
        out_specs=per_b(l_new, MLA_WIDTH),
        scratch_shapes=[pltpu.VMEM((2, n_grp * PAGE_SIZE, KV_LORA), F32),
                        pltpu.VMEM((2, QK_ROPE, n_grp * PAGE_SIZE), F32),
                        pltpu.SemaphoreType.DMA((2, 2))])
    return pl.pallas_call(
        functools.partial(_attn_paged_kernel, layer=layer, n_pages=n_pages, n_grp=n_grp, l_new=l_new),
        grid_spec=grid_spec,
        out_shape=jax.ShapeDtypeStruct((db, l_new, MLA_WIDTH), F32),
        compiler_params=_cparams("arbitrary"),
        name="mla_attention_paged",
    )(page_table.reshape(-1), qcat, new_kv, new_kr, wuv, z_mla, cache_kv, cache_krt)


PAIR = 2 * DN_CHUNK
N_PAIRS = DN_HEADS // 2
NEUMANN_STEPS = 5
SLOT_GROUP = 2
SAMPLE_SEQS_PER_STEP = 4


def _delta_consts(r):
    w = DN_WIDTH
    hid = np.arange(w) // DN_DK
    bd = hid[:, None] == hid[None, :]
    rr = np.arange(r)
    tri = (rr[:, None] // DN_CHUNK == rr[None, :] // DN_CHUNK) & (rr[:, None] >= rr[None, :])
    src = np.arange(LANES)
    eb = src[:, None] == hid[None, :]
    eg = src[:, None] == hid[None, :] + DN_HEADS
    hf = np.arange(DN_HEADS * LANES) // LANES
    ef = src[:, None] == hf[None, :] + DN_HEADS
    return tuple(jnp.asarray(a.astype(np.float32), dtype=BF16) for a in (bd, tri, eb, eg, ef))


def _split3(x):
    hi = x.astype(BF16)
    r1 = x - hi.astype(F32)
    mid = r1.astype(BF16)
    lo = (r1 - mid.astype(F32)).astype(BF16)
    return hi, mid, lo


def _dot_sel(x, sel):
    hi, mid, lo = _split3(x)
    return _dot(hi, sel) + _dot(mid, sel) + _dot(lo, sel)


def _sel_dot(sel, x):
    hi, mid, lo = _split3(x)
    return _dot(sel, hi) + _dot(sel, mid) + _dot(sel, lo)


def _delta_kernel(u_ref, z_ref, ba_ref, cbuf_ref, cw_ref, alog_ref, dtb_ref, sp0_ref, gdn_ref,
                  bd_ref, tri_ref, eb_ref, eg_ref, ef_ref,
                  y_ref, cst_ref, spo_ref,
                  xx_ref, sp_ref, q_s, k_s, kb_s, qe_s, vb_s, kbe_s, gcl_s, gf_s, u_s, w_s, o_s, qkd_s, kdt_s,
                  *, ns, tl, n_l):
    li = pl.program_id(1)
    cps = max(tl, DN_CHUNK) // DN_CHUNK
    seq_rows = cps * DN_CHUNK
    n_slots = ns * cps

    @pl.when(li == 0)
    def _():
        xx_ref[:, 0:CONV_HALO, :] = cbuf_ref[...]
        sp_ref[...] = sp0_ref[...]

    def pad_rows(a):
        if tl == seq_rows:
            return a
        return jnp.concatenate([a, jnp.zeros((seq_rows - tl, a.shape[1]), a.dtype)], axis=0)

    def per_seq(a):
        if ns == 1:
            return pad_rows(a)
        return jnp.concatenate([pad_rows(a[s * tl:(s + 1) * tl]) for s in range(ns)], axis=0)

    base = CONV_HALO - (CONV_W - 1)
    pieces = []
    for s in range(ns):
        xx_ref[s, CONV_HALO:CONV_HALO + tl, :] = u_ref[s * tl:(s + 1) * tl, :]
        acc = xx_ref[s, base:base + tl, :] * cw_ref[0:1, :]
        for j in range(1, CONV_W):
            acc = acc + xx_ref[s, base + j:base + j + tl, :] * cw_ref[j:j + 1, :]
        pieces.append(pad_rows(_silu(acc)))
    qkv = pieces[0] if ns == 1 else jnp.concatenate(pieces, axis=0)
    tail = xx_ref[:, tl:tl + CONV_HALO, :]

    @pl.when(li == n_l - 1)
    def _():
        cst_ref[...] = tail

    xx_ref[:, 0:CONV_HALO, :] = tail

    bd = bd_ref[...]
    q = qkv[:, 0:DN_WIDTH]
    k = qkv[:, DN_WIDTH:2 * DN_WIDTH]
    v = qkv[:, 2 * DN_WIDTH:3 * DN_WIDTH]
    qn = q * lax.rsqrt(_dot_sel(q * q, bd) + EPS) * (DN_DK ** -0.5)
    kn = k * lax.rsqrt(_dot_sel(k * k, bd) + EPS)
    ba = ba_ref[...]
    lane = lax.broadcasted_iota(jnp.int32, ba.shape, 1)
    beta = jnp.where(lane < DN_HEADS, jax.nn.sigmoid(ba), 0.0)
    xa = ba + dtb_ref[...]
    softplus = jnp.maximum(xa, 0.0) + jnp.log1p(jnp.exp(-jnp.abs(xa)))
    g = jnp.where((lane >= DN_HEADS) & (lane < 2 * DN_HEADS), -jnp.exp(alog_ref[...]) * softplus, 0.0)
    beta, g = per_seq(beta), per_seq(g)
    gc = _sel_dot(tri_ref[...], g)
    beta_l = _dot_sel(beta, eb_ref[...])
    gc_l = _dot_sel(gc, eg_ref[...])
    egc = jnp.exp(gc_l)
    kb = kn * beta_l
    q_s[...] = qn
    k_s[...] = kn
    kb_s[...] = kb
    qe_s[...] = qn * egc
    vb_s[...] = v * beta_l
    kbe_s[...] = kb * egc
    gcl_s[...] = gc_l
    gf_s[...] = _dot_sel(gc, ef_ref[...])

    ri = lax.broadcasted_iota(jnp.int32, (PAIR, PAIR), 0)
    ci = lax.broadcasted_iota(jnp.int32, (PAIR, PAIR), 1)
    same = (ri // DN_CHUNK) == (ci // DN_CHUNK)
    incl = same & (ri >= ci)
    strict = same & (ri > ci)
    eye = (ri == ci).astype(F32)
    first = lax.broadcasted_iota(jnp.int32, (DN_CHUNK, LANES), 1) < DN_DK

    def stack(a):
        return jnp.concatenate([jnp.where(first, a, 0.0), jnp.where(first, 0.0, a)], axis=0)

    def fold(a):
        return a[0:DN_CHUNK] + a[DN_CHUNK:PAIR]

    rows = lambda c: slice(DN_CHUNK * c, DN_CHUNK * (c + 1))
    lanes = lambda p: slice(LANES * p, LANES * (p + 1))

    def solve_slots(slots):
        chains = [(c, p) for c in slots for p in range(N_PAIRS)]
        kst = [stack(k_s[rows(c), lanes(p)]).astype(BF16) for c, p in chains]
        a_mat = [_dot_nt(stack(kb_s[rows(c), lanes(p)]).astype(BF16), ks) for (c, p), ks in zip(chains, kst)]
        qk = [_dot_nt(stack(q_s[rows(c), lanes(p)]).astype(BF16), ks) for (c, p), ks in zip(chains, kst)]
        dec = []
        for c, p in chains:
            g_col = jnp.concatenate([gf_s[rows(c), lanes(2 * p)], gf_s[rows(c), lanes(2 * p + 1)]], axis=0)
            diff = g_col - g_col.T
            dec.append(jnp.where(incl, jnp.exp(jnp.where(incl, diff, 0.0)), 0.0))
        pw = [jnp.where(strict, -(a * d), 0.0) for a, d in zip(a_mat, dec)]
        for (c, p), x, d in zip(chains, qk, dec):
            qkd_s[c * N_PAIRS + p] = (x * d).astype(BF16)
        t_inv = [eye + n for n in pw]
        for _ in range(NEUMANN_STEPS):
            pw16 = [x.astype(BF16) for x in pw]
            pw = [_dot(x, x) for x in pw16]
            t_inv = [t + _dot(t.astype(BF16), x.astype(BF16)) for t, x in zip(t_inv, pw)]
        for (c, p), t in zip(chains, t_inv):
            rhs = jnp.concatenate([stack(vb_s[rows(c), lanes(p)]), stack(kbe_s[rows(c), lanes(p)])], axis=1)
            uw = fold(_dot(t.astype(BF16), rhs.astype(BF16)))
            u_s[rows(c), lanes(p)] = uw[:, 0:LANES]
            w_s[rows(c), lanes(p)] = uw[:, LANES:2 * LANES]
            gcl = gcl_s[rows(c), lanes(p)]
            k_dec = k_s[rows(c), lanes(p)] * jnp.exp(gcl[DN_CHUNK - 1:DN_CHUNK, :] - gcl)
            kdt_s[c * N_PAIRS + p] = k_dec.T.astype(BF16)

    for c0 in range(0, n_slots, SLOT_GROUP):
        solve_slots(range(c0, min(c0 + SLOT_GROUP, n_slots)))

    pairs = range(N_PAIRS)
    for c in range(n_slots):
        sq = c // cps
        sps = [sp_ref[sq, p] for p in pairs]
        wq = [jnp.concatenate([w_s[rows(c), lanes(p)], qe_s[rows(c), lanes(p)]], axis=0).astype(BF16)
              for p in pairs]
        res = [_dot(x, sp.astype(BF16)) for x, sp in zip(wq, sps)]
        v_new = [u_s[rows(c), lanes(p)] - res[p][0:DN_CHUNK] for p in pairs]
        intra = [fold(_dot(qkd_s[c * N_PAIRS + p], stack(v_new[p]).astype(BF16))) for p in pairs]
        upd = [_dot(kdt_s[c * N_PAIRS + p], v_new[p].astype(BF16)) for p in pairs]
        for p in pairs:
            o_s[rows(c), lanes(p)] = res[p][DN_CHUNK:PAIR] + intra[p]
            g_last = gcl_s[DN_CHUNK * (c + 1) - 1:DN_CHUNK * (c + 1), lanes(p)]
            sp_ref[sq, p] = sps[p] * jnp.exp(g_last) + jnp.where(same, upd[p], 0.0)

    o = o_s[...]
    y = o * lax.rsqrt(_dot_sel(o * o, bd) * (1.0 / DN_DV) + EPS) * gdn_ref[...]
    for s in range(ns):
        zs = z_ref[s * tl:(s + 1) * tl, :]
        y_ref[s * tl:(s + 1) * tl, :] = (y[s * seq_rows:s * seq_rows + tl] * _silu(zs)).astype(y_ref.dtype)

    @pl.when(li == n_l - 1)
    def _():
        spo_ref[...] = sp_ref[...]


def _delta(h_qkv, z_dn, h_ba, cbuf8, cw8, alog_row, dtb_row, sp0, gdn_row, b, l, tl, ns, y_dtype):
    n_l = l // tl
    assert ns == 1 or n_l == 1
    n_slots = ns * (max(tl, DN_CHUNK) // DN_CHUNK)
    r = n_slots * DN_CHUNK
    consts = _delta_consts(r)
    row = lambda w: pl.BlockSpec((ns * tl, w), lambda bi, li: (bi * n_l + li, 0))
    cst = pl.BlockSpec((ns, CONV_HALO, CONV_CH), lambda bi, li: (bi, 0, 0))
    spb = pl.BlockSpec((ns, N_PAIRS, PAIR, PAIR), lambda bi, li: (bi, 0, 0, 0))
    wide = lambda n: pltpu.VMEM((r, n), F32)
    return pl.pallas_call(
        functools.partial(_delta_kernel, ns=ns, tl=tl, n_l=n_l),
        grid=(b // ns, n_l),
        in_specs=[row(CONV_CH), row(DN_WIDTH), row(LANES), cst, _const_spec(cw8.shape),
                  _const_spec(alog_row.shape), _const_spec(dtb_row.shape), spb, _const_spec(gdn_row.shape)]
                 + [_const_spec(c.shape) for c in consts],
        out_specs=[row(DN_WIDTH), cst, spb],
        out_shape=[jax.ShapeDtypeStruct((b * l, DN_WIDTH), y_dtype),
                   jax.ShapeDtypeStruct((b, CONV_HALO, CONV_CH), F32),
                   jax.ShapeDtypeStruct((b, N_PAIRS, PAIR, PAIR), F32)],
        scratch_shapes=[pltpu.VMEM((ns, CONV_HALO + tl, CONV_CH), F32), pltpu.VMEM((ns, N_PAIRS, PAIR, PAIR), F32)]
                       + [wide(DN_WIDTH)] * 7 + [wide(DN_HEADS * LANES)] + [wide(DN_WIDTH)] * 3
                       + [pltpu.VMEM((n_slots * N_PAIRS, PAIR, PAIR), BF16),
                          pltpu.VMEM((n_slots * N_PAIRS, PAIR, DN_CHUNK), BF16)],
        compiler_params=_cparams("parallel", "arbitrary"),
        name="gated_deltanet",
    )(h_qkv, z_dn, h_ba, cbuf8, cw8, alog_row, dtb_row, sp0, gdn_row, *consts)


def _merge_kernel(x_ref, ya_ref, yb_ref, yc_ref, g_ref, wa_ref, wb_ref, wc_ref, wo_ref, fg_ref, o_ref, *, final):
    merged = None
    for j, (y_ref, w_ref) in enumerate(((ya_ref, wa_ref), (yb_ref, wb_ref), (yc_ref, wc_ref))):
        br = _dot(y_ref[...].astype(BF16), w_ref[...])
        t = jax.nn.sigmoid(g_ref[:, D_MODEL * j:D_MODEL * (j + 1)]) * br
        merged = t if merged is None else merged + t
    x = x_ref[...] + _dot(merged.astype(BF16), wo_ref[...])
    o_ref[...] = _rms(x, fg_ref[...]) if final else x


def _merge(x2d, ya, yb, yc, h_gate, wa, wb, wc, wo, final_g, tm, final):
    t = x2d.shape[0]
    row = lambda w: pl.BlockSpec((tm, w), lambda i: (i, 0))
    return pl.pallas_call(
        functools.partial(_merge_kernel, final=final),
        grid=(t // tm,),
        in_specs=[row(D_MODEL), row(POOL_WIDTH), row(MLA_WIDTH), row(DN_WIDTH), row(N_BRANCH * D_MODEL)]
                 + [_const_spec(w.shape) for w in (wa, wb, wc, wo, final_g)],
        out_specs=row(D_MODEL),
        out_shape=jax.ShapeDtypeStruct((t, D_MODEL), F32),
        compiler_params=_cparams("parallel"),
        name="merge_out",
    )(x2d, ya, yb, yc, h_gate, wa, wb, wc, wo, final_g)


def _layer_weights(l, norm_g, w_in, pool_mix, pool_scale, q_norm_g, w_uq, kv_norm_g, w_uk, w_uv,
                   conv_w, a_log, dt_bias, dn_norm_g, w_br_pool, w_br_mla, w_br_dn, w_out):
    wt = jnp.swapaxes(w_in, 1, 2)[l]
    o = np.cumsum((0, POOL_WIDTH, POOL_WIDTH, Q_LORA, KV_LORA, QK_ROPE, MLA_WIDTH, CONV_CH, DN_WIDTH,
                   DN_HEADS, DN_HEADS, N_BRANCH * D_MODEL))
    seg = lambda i: wt[o[i]:o[i + 1]]
    kr_blk = jnp.pad(seg(4), ((ROPE_LANE0, HEAD_BLOCK - ROPE_LANE0 - QK_ROPE), (0, 0)))
    ba_blk = jnp.pad(wt[o[8]:o[10]], ((0, LANES - 2 * DN_HEADS), (0, 0)))
    w_segs = [seg(0), seg(1), seg(2), seg(3), kr_blk, seg(5), seg(6), seg(7), ba_blk, seg(10)]
    w_segs = [w.astype(BF16) for w in w_segs]

    dq = QK_NOPE + QK_ROPE
    wuq = w_uq[l].reshape(Q_LORA, MLA_HEADS, dq)
    wuq_p = jnp.pad(wuq, ((0, 0), (0, 0), (0, HEAD_BLOCK - dq))).reshape(Q_LORA, -1).astype(BF16)
    wuk_p = jnp.pad(w_uk[l], ((0, 0), (0, 0), (0, HEAD_BLOCK - QK_NOPE))).reshape(KV_LORA, -1).astype(BF16)
    wuv_f = w_uv[l].reshape(KV_LORA, MLA_WIDTH).astype(BF16)
    wcat = jnp.zeros((MLA_HEADS, HEAD_BLOCK, KV_LORA + LANES), F32)
    wcat = wcat.at[:, 0:QK_NOPE, 0:KV_LORA].set(jnp.transpose(w_uk[l], (1, 2, 0)))
    sel = jnp.eye(QK_ROPE, dtype=F32)
    wcat = wcat.at[:, ROPE_LANE0:ROPE_LANE0 + QK_ROPE, KV_LORA:KV_LORA + QK_ROPE].set(sel)
    lanes16 = slice(DN_HEADS, 2 * DN_HEADS)
    return dict(
        norm_g=norm_g[l][None, :], w_segs=w_segs,
        pool_mix=pool_mix[l].astype(BF16), pool_scale=pool_scale[l][None, :],
        q_norm_g=q_norm_g[l][None, :], wuq_p=wuq_p, kv_norm_g=kv_norm_g[l][None, :],
        wuk_p=wuk_p, wuv_f=wuv_f, wcat=wcat.astype(BF16),
        cw8=jnp.pad(conv_w[l], ((0, SUBLANES - CONV_W), (0, 0))),
        alog_row=jnp.zeros((1, LANES), F32).at[0, lanes16].set(a_log[l]),
        dtb_row=jnp.zeros((1, LANES), F32).at[0, lanes16].set(dt_bias[l]),
        gdn_row=jnp.tile(dn_norm_g[l], DN_HEADS)[None, :],
        wa=w_br_pool[l].astype(BF16), wb=w_br_mla[l].astype(BF16), wc=w_br_dn[l].astype(BF16),
        wo=w_out[l].astype(BF16))


def _tile(n, target):
    t = min(n, target)
    while n % t:
        t -= SUBLANES
    return t


def _group_layer(x2d, b, l, start, w, tabs, final_g, final, pool_buf, conv_buf, sp0, paged):
    t = b * l
    small = l < 2 * SUBLANES
    y_dtype = F32 if small else BF16
    h_pool, z_pool, h_q, h_kv, h_kr, z_mla, h_qkv, z_dn, h_ba, h_gate = _inproj(
        x2d, w["norm_g"], w["w_segs"], _tile(t, 256))

    ya, pool_st = _pool(h_pool, z_pool, pool_buf, w["pool_mix"], w["pool_scale"], b, l, _tile(l, 512), start, y_dtype)

    tm = _tile(l, 512)
    if paged is None:
        ckv, kr_blk, q, k, v = _mla_prep(h_q, h_kv, h_kr, tabs, w["q_norm_g"], w["wuq_p"], w["kv_norm_g"],
                                         (w["wuk_p"], w["wuv_f"]), b, l, tm, absorbed=False)
        k_r = kr_blk[:, ROPE_LANE0:ROPE_LANE0 + QK_ROPE]
        yb = _attn_prompt(q, k, v, z_mla, b, l, _tile(l, 512))
    else:
        page_table, cache_kv, cache_krt, layer = paged
        ckv, kr_blk, qcat = _mla_prep(h_q, h_kv, h_kr, tabs, w["q_norm_g"], w["wuq_p"], w["kv_norm_g"],
                                      (w["wcat"],), b, l, tm, absorbed=True)
        k_r = kr_blk[:, ROPE_LANE0:ROPE_LANE0 + QK_ROPE]
        wq = qcat.shape[-1] // MLA_HEADS
        qcat = qcat.reshape(b, l, MLA_HEADS, wq).transpose(0, 2, 1, 3).reshape(b, MLA_HEADS * l, wq)
        n_pages = page_table.shape[1]
        n_grp = PAGES_PER_GROUP
        while n_pages % (2 * n_grp):
            n_grp //= 2
        yb = _attn_paged(page_table, qcat, cache_kv, cache_krt, layer, ckv.reshape(b, l, KV_LORA),
                         k_r.reshape(b, l, QK_ROPE), w["wuv_f"], z_mla.reshape(b, l, MLA_WIDTH), n_grp)
        yb = yb.reshape(t, MLA_WIDTH)

    if l >= DN_CHUNK:
        tl, ns = _tile(l, 256), 1
    else:
        tl, ns = l, SAMPLE_SEQS_PER_STEP
        while b % ns:
            ns //= 2
    yc, conv_st, sp = _delta(h_qkv, z_dn, h_ba, conv_buf, w["cw8"], w["alog_row"], w["dtb_row"], sp0,
                             w["gdn_row"], b, l, tl, ns, y_dtype)

    x_out = _merge(x2d, ya, yb, yc, h_gate, w["wa"], w["wb"], w["wc"], w["wo"], final_g, _tile(t, 256), final)
    return x_out, ckv, k_r, pool_st[:, 1:], conv_st[:, CONV_HALO - (CONV_W - 1):], sp


def _pairs_to_blockdiag(s):
    b = s.shape[0]
    s = s.reshape(b, N_PAIRS, 2, DN_DK, DN_DV)
    z = jnp.zeros_like(s[:, :, 0])
    top = jnp.concatenate([s[:, :, 0], z], axis=-1)
    bot = jnp.concatenate([z, s[:, :, 1]], axis=-1)
    return jnp.concatenate([top, bot], axis=-2)


def _blockdiag_to_pairs(sp):
    b = sp.shape[0]
    s = jnp.stack([sp[:, :, :DN_DK, :DN_DV], sp[:, :, DN_DK:, DN_DV:]], axis=2)
    return s.reshape(b, DN_HEADS, DN_DK, DN_DV)


def kernel(x_prompt, x_sample, cache_kv_latent, cache_k_rope, state_pool, state_conv, state_delta,
           page_table, norm_g, w_in, pool_mix, pool_scale, q_norm_g, w_uq, kv_norm_g, w_uk, w_uv,
           conv_w, a_log, dt_bias, dn_norm_g, w_br_pool, w_br_mla, w_br_dn, w_out, final_norm_g):
    bp, lp, d = x_prompt.shape
    db, ls, _ = x_sample.shape
    depth = w_in.shape[0]
    past_len = page_table.shape[1] * PAGE_SIZE
    final_g = final_norm_g[None, :]
    cache_krt = jnp.swapaxes(cache_k_rope, 2, 3)

    tabs_p = _rope_tables(lp, _tile(lp, 512), 0)
    tabs_s = _rope_tables(ls, ls, past_len)
    zero_pool = jnp.zeros((bp, POOL_HALO, POOL_WIDTH), F32)
    zero_conv = jnp.zeros((bp, CONV_HALO, CONV_CH), F32)
    zero_sp = jnp.zeros((bp, N_PAIRS, PAIR, PAIR), F32)

    xp = x_prompt.reshape(bp * lp, d)
    xs = x_sample.reshape(db * ls, d)
    outs_p, outs_s = [], []
    for l in range(depth):
        w = _layer_weights(l, norm_g, w_in, pool_mix, pool_scale, q_norm_g, w_uq, kv_norm_g, w_uk, w_uv,
                           conv_w, a_log, dt_bias, dn_norm_g, w_br_pool, w_br_mla, w_br_dn, w_out)
        final = l == depth - 1
        xp, ckv, kr, pst, cst, sp = _group_layer(xp, bp, lp, 0, w, tabs_p, final_g, final,
                                                 zero_pool, zero_conv, zero_sp, None)
        outs_p.append((ckv.reshape(bp, lp, KV_LORA), kr.reshape(bp, lp, QK_ROPE), pst, cst,
                       _blockdiag_to_pairs(sp)))
        pool_buf = jnp.pad(state_pool[l], ((0, 0), (POOL_HALO - POOL_BUF, 0), (0, 0)))
        conv_buf = jnp.pad(state_conv[l], ((0, 0), (CONV_HALO - (CONV_W - 1), 0), (0, 0)))
        xs, ckv, kr, pst, cst, sp = _group_layer(xs, db, ls, past_len, w, tabs_s, final_g, final,
                                                 pool_buf, conv_buf, _pairs_to_blockdiag(state_delta[l]),
                                                 (page_table, cache_kv_latent, cache_krt, l))
        outs_s.append((ckv.reshape(db, ls, KV_LORA), kr.reshape(db, ls, QK_ROPE), pst, cst,
                       _blockdiag_to_pairs(sp)))
    stack = lambda outs, i: jnp.stack([o[i] for o in outs])
    return (xp.reshape(bp, lp, d), xs.reshape(db, ls, d),
            *(stack(outs_p, i) for i in range(5)), *(stack(outs_s, i) for i in range(5)))
```

```python
import functools

import numpy as np
import jax
import jax.numpy as jnp
from jax import lax
from jax.experimental import pallas as pl
from jax.experimental.pallas import tpu as pltpu

F32 = jnp.float32
BF16 = jnp.bfloat16

D_MODEL = 1024
EPS = 1e-6
POOL_WINDOWS = (2, 4, 8, 16)
POOL_GROUP_DIM = 128
POOL_WIDTH = 512
POOL_BUF = 15
MLA_HEADS = 8
QK_NOPE = 64
QK_ROPE = 32
V_HEAD = 64
Q_LORA = 384
KV_LORA = 256
MLA_WIDTH = 512
MLA_SCALE = (QK_NOPE + QK_ROPE) ** -0.5
ROPE_THETA = 10000.0
DN_HEADS = 8
DN_DK = 64
DN_DV = 64
DN_WIDTH = 512
CONV_W = 4
CONV_CH = 1536
DN_CHUNK = 64
PAGE_SIZE = 128
N_BRANCH = 3

LANES = 128
SUBLANES = 8
HEAD_BLOCK = LANES
ROPE_LANE0 = QK_NOPE
ROPE_HALF = QK_ROPE // 2
POOL_HALO = 16
CONV_HALO = 8
VMEM_LIMIT = 48 * 1024 * 1024
NEG = -1e30
LOG2E = 1.4426950408889634
Q_SCALE = MLA_SCALE * LOG2E
PAGES_PER_GROUP = 32
PAGED_SLOTS = 3
PAGED_ROW_CHUNK = 16


def _cparams(*sem):
    return pltpu.CompilerParams(dimension_semantics=sem, vmem_limit_bytes=VMEM_LIMIT)


def _const_spec(shape):
    nd = len(shape)
    return pl.BlockSpec(shape, lambda *_: (0,) * nd, pipeline_mode=pl.Buffered(1))


def _dot(a, b):
    return jnp.dot(a, b, preferred_element_type=F32)


def _dot_nt(a, b):
    return lax.dot_general(a, b, (((1,), (1,)), ((), ())), preferred_element_type=F32)


def _silu(x):
    return x * jax.nn.sigmoid(x)


def _rms(x, g):
    return x * lax.rsqrt(jnp.mean(x * x, axis=-1, keepdims=True) + EPS) * g


IN_SPLITS = (POOL_WIDTH, POOL_WIDTH, Q_LORA, KV_LORA, QK_ROPE, MLA_WIDTH, CONV_CH, DN_WIDTH,
             DN_HEADS, DN_HEADS, N_BRANCH * D_MODEL)
IN_OFFSETS = tuple(int(v) for v in np.cumsum((0,) + IN_SPLITS))
IN_WIDTH = IN_OFFSETS[-1]
KR_SPLIT, BETA_SPLIT, ALPHA_SPLIT = 4, 8, 9
IN_OUTPUTS = ((IN_OFFSETS[0], POOL_WIDTH), (IN_OFFSETS[1], POOL_WIDTH), (IN_OFFSETS[2], Q_LORA),
              (IN_OFFSETS[3], KV_LORA), (None, HEAD_BLOCK), (IN_OFFSETS[5], MLA_WIDTH),
              (IN_OFFSETS[6], CONV_CH), (IN_OFFSETS[7], DN_WIDTH), (None, LANES),
              (IN_OFFSETS[10], N_BRANCH * D_MODEL))
IN_DOT_CHUNK = 512


def _inproj_kernel(x_ref, g_ref, w_ref, kr_ref, ba_ref, *o_refs):
    xn = _rms(x_ref[...], g_ref[...]).astype(BF16)
    small = iter((kr_ref, ba_ref))
    for o_ref, (row0, width) in zip(o_refs, IN_OUTPUTS):
        if row0 is None:
            o_ref[...] = _dot_nt(xn, next(small)[...])
            continue
        for c0 in range(0, width, IN_DOT_CHUNK):
            c1 = min(c0 + IN_DOT_CHUNK, width)
            o_ref[:, c0:c1] = _dot_nt(xn, w_ref[row0 + c0:row0 + c1, :])


def _inproj(x2d, norm_g, wt_all, layer, kr_blk, ba_blk, tm):
    t = x2d.shape[0]
    row = lambda w: pl.BlockSpec((tm, w), lambda i: (i, 0))
    w_spec = pl.BlockSpec((None,) + wt_all.shape[1:], lambda i: (layer, 0, 0), pipeline_mode=pl.Buffered(1))
    return pl.pallas_call(
        _inproj_kernel,
        grid=(t // tm,),
        in_specs=[row(D_MODEL), _const_spec((1, D_MODEL)), w_spec,
                  _const_spec(kr_blk.shape), _const_spec(ba_blk.shape)],
        out_specs=[row(w) for _, w in IN_OUTPUTS],
        out_shape=[jax.ShapeDtypeStruct((t, w), F32) for _, w in IN_OUTPUTS],
        compiler_params=_cparams("parallel"),
        name="inproj",
    )(x2d, norm_g, wt_all, kr_blk, ba_blk)


def _pool_kernel(u_ref, z_ref, buf_ref, mix_ref, scale_ref, y_ref, st_ref, xx_ref, *, tl, start, n_l):
    li = pl.program_id(1)

    @pl.when(li == 0)
    def _():
        xx_ref[0:POOL_HALO, :] = buf_ref[...]

    u = u_ref[...]
    xx_ref[POOL_HALO:POOL_HALO + tl, :] = u
    row = lax.broadcasted_iota(jnp.int32, (tl, 1), 0)
    pos1 = start + li * tl + row + 1
    for gi, w in enumerate(POOL_WINDOWS):
        lanes = slice(POOL_GROUP_DIM * gi, POOL_GROUP_DIM * (gi + 1))
        s = u[:, lanes]
        for o in range(1, w):
            s = s + xx_ref[POOL_HALO - o:POOL_HALO - o + tl, lanes]
        cnt = jnp.minimum(pos1, w).astype(F32)
        d = s / cnt - u[:, lanes]
        y = _dot(d.astype(BF16), mix_ref[gi]) * scale_ref[:, lanes]
        y_ref[:, lanes] = (y * _silu(z_ref[:, lanes])).astype(y_ref.dtype)

    tail = xx_ref[tl:tl + POOL_HALO, :]

    @pl.when(li == n_l - 1)
    def _():
        st_ref[...] = tail

    xx_ref[0:POOL_HALO, :] = tail


def _pool(h_pool, z_pool, buf16, mix, scale, b, l, tl, start, y_dtype):
    n_l = l // tl
    row = lambda w: pl.BlockSpec((tl, w), lambda bi, li: (bi * n_l + li, 0))
    st = pl.BlockSpec((None, POOL_HALO, POOL_WIDTH), lambda bi, li: (bi, 0, 0))
    return pl.pallas_call(
        functools.partial(_pool_kernel, tl=tl, start=start, n_l=n_l),
        grid=(b, n_l),
        in_specs=[row(POOL_WIDTH), row(POOL_WIDTH), st, _const_spec(mix.shape), _const_spec(scale.shape)],
        out_specs=[row(POOL_WIDTH), st],
        out_shape=[jax.ShapeDtypeStruct((b * l, POOL_WIDTH), y_dtype),
                   jax.ShapeDtypeStruct((b, POOL_HALO, POOL_WIDTH), F32)],
        scratch_shapes=[pltpu.VMEM((POOL_HALO + tl, POOL_WIDTH), F32)],
        compiler_params=_cparams("parallel", "arbitrary"),
        name="pool_mixer",
    )(h_pool, z_pool, buf16, mix, scale)


def _rope_table_kernel(inv_ref, c_ref, s1_ref, s2_ref, *, tl, start):
    i = pl.program_id(0)
    shape = (tl, LANES)
    pos = (start + i * tl + lax.broadcasted_iota(jnp.int32, shape, 0)).astype(F32)
    lane = lax.broadcasted_iota(jnp.int32, shape, 1)
    ang = pos * inv_ref[...]
    cos, sin = jnp.cos(ang), jnp.sin(ang)
    first = (lane >= ROPE_LANE0) & (lane < ROPE_LANE0 + ROPE_HALF)
    second = (lane >= ROPE_LANE0 + ROPE_HALF) & (lane < ROPE_LANE0 + QK_ROPE)
    c_ref[...] = jnp.where(first | second, cos, 1.0)
    s1_ref[...] = jnp.where(first, -sin, 0.0)
    s2_ref[...] = jnp.where(second, sin, 0.0)


def _rope_tables(l, tl, start):
    half = ROPE_HALF
    inv = jnp.power(ROPE_THETA, -jnp.arange(half, dtype=F32) / half)
    inv_lane = jnp.zeros((1, LANES), F32)
    inv_lane = inv_lane.at[0, ROPE_LANE0:ROPE_LANE0 + half].set(inv)
    inv_lane = inv_lane.at[0, ROPE_LANE0 + half:ROPE_LANE0 + QK_ROPE].set(inv)
    blk = pl.BlockSpec((tl, LANES), lambda i: (i, 0))
    return pl.pallas_call(
        functools.partial(_rope_table_kernel, tl=tl, start=start),
        grid=(l // tl,),
        in_specs=[_const_spec((1, LANES))],
        out_specs=[blk, blk, blk],
        out_shape=[jax.ShapeDtypeStruct((l, LANES), F32)] * 3,
        compiler_params=_cparams("parallel"),
        name="rope_tables",
    )(inv_lane)


def _rope(x, c, s1, s2):
    return x * c + pltpu.roll(x, LANES - ROPE_HALF, 1) * s1 + pltpu.roll(x, ROPE_HALF, 1) * s2


def _mla_prep_kernel(hq_ref, hkv_ref, hkr_ref, c_ref, s1_ref, s2_ref, qg_ref, wuq_ref, kvg_ref, *rest,
                     absorbed):
    c, s1, s2 = c_ref[...], s1_ref[...], s2_ref[...]
    cq = _rms(hq_ref[...], qg_ref[...]).astype(BF16)
    q = _dot(cq, wuq_ref[...])
    ckv = _rms(hkv_ref[...], kvg_ref[...])
    kr = _rope(hkr_ref[...], c, s1, s2)
    if absorbed:
        wcat_ref, ckv_ref, kr_ref, qcat_ref = rest
    else:
        wuk_ref, wuv_ref, ckv_ref, kr_ref, q_ref, k_ref, v_ref = rest
        ckv16 = ckv.astype(BF16)
        knope = _dot(ckv16, wuk_ref[...])
        v_ref[...] = _dot(ckv16, wuv_ref[...]).astype(BF16)
    ckv_ref[...] = ckv
    kr_ref[...] = kr
    for h in range(MLA_HEADS):
        blk = slice(HEAD_BLOCK * h, HEAD_BLOCK * (h + 1))
        qh = (_rope(q[:, blk], c, s1, s2) * Q_SCALE).astype(BF16)
        if absorbed:
            w = wcat_ref.shape[-1]
            qcat_ref[:, w * h:w * (h + 1)] = _dot(qh, wcat_ref[h]).astype(BF16)
        else:
            q_ref[:, blk] = qh
            k_ref[:, blk] = (knope[:, blk] + kr).astype(BF16)


def _mla_prep(h_q, h_kv, h_kr, tabs, q_norm_g, wuq_p, kv_norm_g, extra_w, b, l, tm, absorbed):
    n_l = l // tm
    t = b * l
    row = lambda w: pl.BlockSpec((tm, w), lambda bi, li: (bi * n_l + li, 0))
    tab = pl.BlockSpec((tm, LANES), lambda bi, li: (li, 0))
    hw = MLA_HEADS * HEAD_BLOCK
    in_specs = [row(Q_LORA), row(KV_LORA), row(HEAD_BLOCK), tab, tab, tab,
                _const_spec(q_norm_g.shape), _const_spec(wuq_p.shape), _const_spec(kv_norm_g.shape)]
    in_specs += [_const_spec(w.shape) for w in extra_w]
    out_specs = [row(KV_LORA), row(HEAD_BLOCK)]
    out_shape = [jax.ShapeDtypeStruct((t, KV_LORA), F32), jax.ShapeDtypeStruct((t, HEAD_BLOCK), F32)]
    if absorbed:
        wc = MLA_HEADS * extra_w[0].shape[-1]
        out_specs += [row(wc)]
        out_shape += [jax.ShapeDtypeStruct((t, wc), BF16)]
    else:
        out_specs += [row(hw), row(hw), row(MLA_WIDTH)]
        out_shape += [jax.ShapeDtypeStruct((t, hw), BF16), jax.ShapeDtypeStruct((t, hw), BF16),
                      jax.ShapeDtypeStruct((t, MLA_WIDTH), BF16)]
    return pl.pallas_call(
        functools.partial(_mla_prep_kernel, absorbed=absorbed),
        grid=(b, n_l),
        in_specs=in_specs, out_specs=out_specs, out_shape=out_shape,
        compiler_params=_cparams("parallel", "parallel"),
        name="mla_prep_absorbed" if absorbed else "mla_prep",
    )(h_q, h_kv, h_kr, *tabs, q_norm_g, wuq_p, kv_norm_g, *extra_w)


ATTN_ROW_CHUNK = 32


def _softmax_update(m_b, l_p, s, row_chunk):
    r, n = s.shape[0], s.shape[1] // LANES
    cols = [slice(LANES * c, LANES * (c + 1)) for c in range(n)]
    folded = s[:, cols[0]]
    for cb in cols[1:]:
        folded = jnp.maximum(folded, s[:, cb])
    mx = jnp.max(folded, axis=-1, keepdims=True)
    m_new = jnp.maximum(m_b, jnp.broadcast_to(mx, m_b.shape))
    a_b = jnp.exp2(m_b - m_new)
    l_rows, p_rows = [], []
    for c0 in range(0, r, row_chunk):
        rows = slice(c0, c0 + row_chunk)
        m_c = m_new[rows]
        pieces = [jnp.exp2(s[rows, cb] - m_c) for cb in cols]
        tot = pieces[0]
        for pc in pieces[1:]:
            tot = tot + pc
        l_rows.append(a_b[rows] * l_p[rows] + tot)
        p_rows.append(jnp.concatenate([pc.astype(BF16) for pc in pieces], axis=1))
    cat = lambda parts: parts[0] if len(parts) == 1 else jnp.concatenate(parts, axis=0)
    return m_new, cat(l_rows), a_b, cat(p_rows)


def _attn_kernel(q_ref, k_ref, v_ref, z_ref, y_ref, *, ta):
    qi = pl.program_id(1)
    rc = min(ATTN_ROW_CHUNK, ta)
    low_half = lax.broadcasted_iota(jnp.int32, (ta, LANES), 1) < V_HEAD
    causal = (lax.broadcasted_iota(jnp.int32, (ta, ta), 1)
              <= lax.broadcasted_iota(jnp.int32, (ta, ta), 0))
    for p in range(MLA_HEADS // 2):
        pair = slice(LANES * p, LANES * (p + 1))
        blks = [slice(HEAD_BLOCK * h, HEAD_BLOCK * (h + 1)) for h in (2 * p, 2 * p + 1)]
        qhs = [q_ref[:, blk] for blk in blks]

        def step(kb, carry, masked, pair=pair, blks=blks, qhs=qhs):
            r0 = pl.multiple_of(kb * ta, ta)
            vv = v_ref[pl.ds(r0, ta), pair]
            scores = [_dot_nt(qh, k_ref[pl.ds(r0, ta), blk]) for qh, blk in zip(qhs, blks)]
            out = []
            for (m_b, l_p, acc), s in zip(carry, scores):
                if masked:
                    s = jnp.where(causal, s, NEG)
                m_b, l_p, a_b, pm = _softmax_update(m_b, l_p, s, rc)
                out.append((m_b, l_p, a_b * acc + _dot(pm, vv)))
            return tuple(out)

        init = (jnp.full((ta, LANES), NEG, F32), jnp.zeros((ta, LANES), F32), jnp.zeros((ta, LANES), F32))
        carry = lax.fori_loop(0, qi, functools.partial(step, masked=False), (init, init))
        (_, l0, acc0), (_, l1, acc1) = step(qi, carry, True)
        den = lambda l_p: jnp.sum(l_p, axis=-1, keepdims=True)
        o = jnp.where(low_half, acc0 / den(l0), acc1 / den(l1))
        y_ref[:, pair] = (o * _silu(z_ref[:, pair])).astype(y_ref.dtype)


def _attn_prompt(q, k, v, z_mla, b, l, ta):
    n_q = l // ta
    hw = MLA_HEADS * HEAD_BLOCK
    row = lambda w: pl.BlockSpec((ta, w), lambda bi, qi: (bi * n_q + qi, 0))
    seq = lambda w: pl.BlockSpec((l, w), lambda bi, qi: (bi, 0))
    return pl.pallas_call(
        functools.partial(_attn_kernel, ta=ta),
        grid=(b, n_q),
        in_specs=[row(hw), seq(hw), seq(MLA_WIDTH), row(MLA_WIDTH)],
        out_specs=row(MLA_WIDTH),
        out_shape=jax.ShapeDtypeStruct((b * l, MLA_WIDTH), BF16),
        compiler_params=_cparams("parallel", "arbitrary"),
        name="mla_attention_prompt",
    )(q, k, v, z_mla)


def _attn_paged_kernel(pt_ref, qcat_ref, nkv_ref, nkr_ref, wuv_ref, z_ref, kv_hbm, krt_hbm, y_ref,
                       kvbuf, krbuf, sem, *, layer, n_pages, n_grp, l_new):
    b = pl.program_id(0)
    n_seq = pl.num_programs(0)
    n_groups = n_pages // n_grp
    rows = MLA_HEADS * l_new
    ahead = PAGED_SLOTS - 1

    def group_copies(seq, j):
        slot = (seq * n_groups + j) % PAGED_SLOTS
        cps = []
        for g in range(n_grp):
            page = pt_ref[seq * n_pages + j * n_grp + g]
            keys = pl.ds(PAGE_SIZE * g, PAGE_SIZE)
            cps.append(pltpu.make_async_copy(kv_hbm.at[layer, page], kvbuf.at[slot, keys, :], sem.at[0, slot]))
            cps.append(pltpu.make_async_copy(krt_hbm.at[layer, page], krbuf.at[slot, :, keys], sem.at[1, slot]))
        return cps

    def start_group(seq, j):
        for cp in group_copies(seq, j):
            cp.start()

    def wait_group(seq, j):
        for cp in group_copies(seq, j):
            cp.wait()

    def start_ahead(j):
        nxt = j + ahead
        if nxt < n_groups:
            start_group(b, nxt)
        else:
            @pl.when(b + 1 < n_seq)
            def _():
                start_group(b + 1, nxt - n_groups)

    @pl.when(b == 0)
    def _():
        for j0 in range(ahead):
            start_group(b, j0)

    qc = qcat_ref[...]
    ql, qr = qc[:, :KV_LORA], qc[:, KV_LORA:KV_LORA + QK_ROPE]

    def update(state, s, kv16):
        m_b, l_p, acc = state
        m_b, l_p, a_b, pm = _softmax_update(m_b, l_p, s, PAGED_ROW_CHUNK)
        a_wide = jnp.concatenate([a_b] * (KV_LORA // LANES), axis=1)
        return m_b, l_p, a_wide * acc + _dot(pm, kv16)

    state = (jnp.full((rows, LANES), NEG, F32), jnp.zeros((rows, LANES), F32), jnp.zeros((rows, KV_LORA), F32))
    half_keys = n_grp * PAGE_SIZE // 2
    for j in range(n_groups):
        start_ahead(j)
        wait_group(b, j)
        slot = (b * n_groups + j) % PAGED_SLOTS
        halves = []
        for hf in range(2):
            keys = slice(half_keys * hf, half_keys * (hf + 1))
            kv16 = kvbuf[slot, keys, :].astype(BF16)
            kr16 = krbuf[slot, :, keys].astype(BF16)
            halves.append((_dot_nt(ql, kv16) + _dot(qr, kr16), kv16))
        for s, kv16 in halves:
            state = update(state, s, kv16)

    pad = PAGE_SIZE - l_new
    nkv = jnp.concatenate([nkv_ref[...], jnp.zeros((pad, KV_LORA), F32)], axis=0).astype(BF16)
    nkr = jnp.concatenate([nkr_ref[...], jnp.zeros((pad, QK_ROPE), F32)], axis=0).astype(BF16)
    tok = lax.broadcasted_iota(jnp.int32, (rows, PAGE_SIZE), 0) % l_new
    key = lax.broadcasted_iota(jnp.int32, (rows, PAGE_SIZE), 1)
    _, l_p, acc = update(state, jnp.where(key <= tok, _dot_nt(ql, nkv) + _dot_nt(qr, nkr), NEG), nkv)
    o = (acc / jnp.sum(l_p, axis=-1, keepdims=True)).astype(BF16)
    full = _dot(o, wuv_ref[...])
    rh = lax.broadcasted_iota(jnp.int32, full.shape, 0) // l_new
    ch = lax.broadcasted_iota(jnp.int32, full.shape, 1) // V_HEAD
    full = jnp.where(rh == ch, full, 0.0)
    out = full[0:l_new]
    for h in range(1, MLA_HEADS):
        out = out + full[l_new * h:l_new * (h + 1)]
    y_ref[...] = (out * _silu(z_ref[...])).astype(y_ref.dtype)


def _attn_paged(page_table, qcat, cache_kv, cache_krt, layer, new_kv, new_kr, wuv, z_mla, n_grp):
    db, n_pages = page_table.shape
    l_new = new_kv.shape[1]
    rows = MLA_HEADS * l_new
    assert n_pages % n_grp == 0 and n_pages // n_grp >= PAGED_SLOTS - 1
    wq = qcat.shape[-1]
    per_b = lambda r, w: pl.BlockSpec((None, r, w), lambda b, pt: (b, 0, 0))
    hbm = pl.BlockSpec(memory_space=pl.ANY)
    grid_spec = pltpu.PrefetchScalarGridSpec(
        num_scalar_prefetch=1, grid=(db,),
        in_specs=[per_b(rows, wq), per_b(l_new, KV_LORA), per_b(l_new, QK_ROPE),
                  pl.BlockSpec(wuv.shape, lambda b, pt: (0, 0)), per_b(l_new, MLA_WIDTH), hbm, hbm],
        out_specs=per_b(l_new, MLA_WIDTH),
        scratch_shapes=[pltpu.VMEM((PAGED_SLOTS, n_grp * PAGE_SIZE, KV_LORA), F32),
                        pltpu.VMEM((PAGED_SLOTS, QK_ROPE, n_grp * PAGE_SIZE), F32),
                        pltpu.SemaphoreType.DMA((2, PAGED_SLOTS))])
    return pl.pallas_call(
        functools.partial(_attn_paged_kernel, layer=layer, n_pages=n_pages, n_grp=n_grp, l_new=l_new),
        grid_spec=grid_spec,
        out_shape=jax.ShapeDtypeStruct((db, l_new, MLA_WIDTH), F32),
        compiler_params=_cparams("arbitrary"),
        name="mla_attention_paged",
    )(page_table.reshape(-1), qcat, new_kv, new_kr, wuv, z_mla, cache_kv, cache_krt)


PAIR = 2 * DN_CHUNK
N_PAIRS = DN_HEADS // 2
NEUMANN_STEPS = 5
SLOT_GROUP = 4
SAMPLE_SEQS_PER_STEP = 4


def _delta_consts(r):
    w = DN_WIDTH
    hid = np.arange(w) // DN_DK
    bd = hid[:, None] == hid[None, :]
    rr = np.arange(r)
    tri = (rr[:, None] // DN_CHUNK == rr[None, :] // DN_CHUNK) & (rr[:, None] >= rr[None, :])
    return tuple(jnp.asarray(a.astype(np.float32), dtype=BF16) for a in (bd, tri))


def _split3(x):
    hi = x.astype(BF16)
    r1 = x - hi.astype(F32)
    mid = r1.astype(BF16)
    lo = (r1 - mid.astype(F32)).astype(BF16)
    return hi, mid, lo


def _sel_dot(sel, x):
    hi, mid, lo = _split3(x)
    return _dot(sel, hi) + _dot(sel, mid) + _dot(sel, lo)


def _head_sums(x2, bd):
    return _dot(x2.astype(BF16), bd)


def _delta_kernel(u_ref, z_ref, ba_ref, cbuf_ref, cw_ref, alog_ref, dtb_ref, sp0_ref, gdn_ref,
                  bd_ref, tri_ref,
                  y_ref, cst_ref, spo_ref,
                  xx_ref, sp_ref, q_s, k_s, kb_s, qe_s, vb_s, kbe_s, gcl_s, gf_s, u_s, w_s, o_s, qkd_s, kdt_s,
                  *, ns, tl, n_l):
    li = pl.program_id(1)
    cps = max(tl, DN_CHUNK) // DN_CHUNK
    seq_rows = cps * DN_CHUNK
    n_slots = ns * cps

    @pl.when(li == 0)
    def _():
        xx_ref[:, 0:CONV_HALO, :] = cbuf_ref[...]
        sp_ref[...] = sp0_ref[...]

    def pad_rows(a):
        if tl == seq_rows:
            return a
        return jnp.concatenate([a, jnp.zeros((seq_rows - tl, a.shape[1]), a.dtype)], axis=0)

    def per_seq(a):
        if ns == 1:
            return pad_rows(a)
        return jnp.concatenate([pad_rows(a[s * tl:(s + 1) * tl]) for s in range(ns)], axis=0)

    pieces = []
    for s in range(ns):
        xx_ref[s, CONV_HALO:CONV_HALO + tl, :] = u_ref[s * tl:(s + 1) * tl, :]
        ext = xx_ref[s]
        acc = None
        for j in range(CONV_W):
            back = CONV_W - 1 - j
            shifted = ext if back == 0 else pltpu.roll(ext, back, 0)
            term = shifted[CONV_HALO:CONV_HALO + tl] * cw_ref[j:j + 1, :]
            acc = term if acc is None else acc + term
        pieces.append(pad_rows(_silu(acc)))
    qkv = pieces[0] if ns == 1 else jnp.concatenate(pieces, axis=0)
    tail = xx_ref[:, tl:tl + CONV_HALO, :]

    @pl.when(li == n_l - 1)
    def _():
        cst_ref[...] = tail

    xx_ref[:, 0:CONV_HALO, :] = tail

    bd = bd_ref[...]
    q = qkv[:, 0:DN_WIDTH]
    k = qkv[:, DN_WIDTH:2 * DN_WIDTH]
    v = qkv[:, 2 * DN_WIDTH:3 * DN_WIDTH]
    qn = q * lax.rsqrt(_head_sums(q * q, bd) + EPS) * (DN_DK ** -0.5)
    kn = k * lax.rsqrt(_head_sums(k * k, bd) + EPS)
    ba = ba_ref[...]
    lane = lax.broadcasted_iota(jnp.int32, ba.shape, 1)
    beta = jnp.where(lane < DN_HEADS, jax.nn.sigmoid(ba), 0.0)
    xa = ba + dtb_ref[...]
    softplus = jnp.maximum(xa, 0.0) + jnp.log1p(jnp.exp(-jnp.abs(xa)))
    g = jnp.where((lane >= DN_HEADS) & (lane < 2 * DN_HEADS), -jnp.exp(alog_ref[...]) * softplus, 0.0)
    beta, g = per_seq(beta), per_seq(g)
    gc = _sel_dot(tri_ref[...], g)
    n_rows = gc.shape[0]
    first_r = lax.broadcasted_iota(jnp.int32, (n_rows, LANES), 1) < DN_DK

    def head_lanes(a, lane0):
        full = [jnp.broadcast_to(a[:, lane0 + h:lane0 + h + 1], (n_rows, LANES)) for h in range(DN_HEADS)]
        pairs = [jnp.where(first_r, full[2 * p], full[2 * p + 1]) for p in range(N_PAIRS)]
        return jnp.concatenate(pairs, axis=1), full

    beta_l, _ = head_lanes(beta, 0)
    gc_l, gc_full = head_lanes(gc, DN_HEADS)
    egc = jnp.exp(gc_l)
    kb = kn * beta_l
    q_s[...] = qn
    k_s[...] = kn
    kb_s[...] = kb
    qe_s[...] = qn * egc
    vb_s[...] = v * beta_l
    kbe_s[...] = kb * egc
    gcl_s[...] = gc_l
    for h in range(DN_HEADS):
        gf_s[:, LANES * h:LANES * (h + 1)] = gc_full[h]

    ri = lax.broadcasted_iota(jnp.int32, (PAIR, PAIR), 0)
    ci = lax.broadcasted_iota(jnp.int32, (PAIR, PAIR), 1)
    same = (ri // DN_CHUNK) == (ci // DN_CHUNK)
    incl = same & (ri >= ci)
    strict = same & (ri > ci)
    eye = (ri == ci).astype(F32)
    first = lax.broadcasted_iota(jnp.int32, (DN_CHUNK, LANES), 1) < DN_DK

    def stack(a):
        return jnp.concatenate([jnp.where(first, a, 0.0), jnp.where(first, 0.0, a)], axis=0)

    def fold(a):
        return a[0:DN_CHUNK] + a[DN_CHUNK:PAIR]

    rows = lambda c: slice(DN_CHUNK * c, DN_CHUNK * (c + 1))
    lanes = lambda p: slice(LANES * p, LANES * (p + 1))

    def solve_slots(slots):
        chains = [(c, p) for c in slots for p in range(N_PAIRS)]
        kst = [stack(k_s[rows(c), lanes(p)]).astype(BF16) for c, p in chains]
        a_mat = [_dot_nt(stack(kb_s[rows(c), lanes(p)]).astype(BF16), ks) for (c, p), ks in zip(chains, kst)]
        qk = [_dot_nt(stack(q_s[rows(c), lanes(p)]).astype(BF16), ks) for (c, p), ks in zip(chains, kst)]
        dec = []
        for c, p in chains:
            g_col = jnp.concatenate([gf_s[rows(c), lanes(2 * p)], gf_s[rows(c), lanes(2 * p + 1)]], axis=0)
            diff = g_col - g_col.T
            dec.append(jnp.where(incl, jnp.exp(jnp.where(incl, diff, 0.0)), 0.0))
        pw = [jnp.where(strict, -(a * d), 0.0) for a, d in zip(a_mat, dec)]
        for (c, p), x, d in zip(chains, qk, dec):
            qkd_s[c * N_PAIRS + p] = (x * d).astype(BF16)
        t_inv = [eye + n for n in pw]
        for _ in range(NEUMANN_STEPS):
            pw16 = [x.astype(BF16) for x in pw]
            pw = [_dot(x, x) for x in pw16]
            t_inv = [t + _dot(t.astype(BF16), x.astype(BF16)) for t, x in zip(t_inv, pw)]
        for (c, p), t in zip(chains, t_inv):
            rhs = jnp.concatenate([stack(vb_s[rows(c), lanes(p)]), stack(kbe_s[rows(c), lanes(p)])], axis=1)
            uw = fold(_dot(t.astype(BF16), rhs.astype(BF16)))
            u_s[rows(c), lanes(p)] = uw[:, 0:LANES]
            w_s[rows(c), lanes(p)] = uw[:, LANES:2 * LANES]
            gcl = gcl_s[rows(c), lanes(p)]
            k_dec = k_s[rows(c), lanes(p)] * jnp.exp(gcl[DN_CHUNK - 1:DN_CHUNK, :] - gcl)
            kdt_s[c * N_PAIRS + p] = k_dec.T.astype(BF16)

    for c0 in range(0, n_slots, SLOT_GROUP):
        solve_slots(range(c0, min(c0 + SLOT_GROUP, n_slots)))

    pairs = range(N_PAIRS)
    for c in range(n_slots):
        sq = c // cps
        sps = [sp_ref[sq, p] for p in pairs]
        wq = [jnp.concatenate([w_s[rows(c), lanes(p)], qe_s[rows(c), lanes(p)]], axis=0).astype(BF16)
              for p in pairs]
        res = [_dot(x, sp.astype(BF16)) for x, sp in zip(wq, sps)]
        v_new = [u_s[rows(c), lanes(p)] - res[p][0:DN_CHUNK] for p in pairs]
        intra = [fold(_dot(qkd_s[c * N_PAIRS + p], stack(v_new[p]).astype(BF16))) for p in pairs]
        upd = [_dot(kdt_s[c * N_PAIRS + p], v_new[p].astype(BF16)) for p in pairs]
        for p in pairs:
            o_s[rows(c), lanes(p)] = res[p][DN_CHUNK:PAIR] + intra[p]
            g_last = gcl_s[DN_CHUNK * (c + 1) - 1:DN_CHUNK * (c + 1), lanes(p)]
            sp_ref[sq, p] = sps[p] * jnp.exp(g_last) + jnp.where(same, upd[p], 0.0)

    o = o_s[...]
    y = o * lax.rsqrt(_head_sums(o * o, bd) * (1.0 / DN_DV) + EPS) * gdn_ref[...]
    for s in range(ns):
        zs = z_ref[s * tl:(s + 1) * tl, :]
        y_ref[s * tl:(s + 1) * tl, :] = (y[s * seq_rows:s * seq_rows + tl] * _silu(zs)).astype(y_ref.dtype)

    @pl.when(li == n_l - 1)
    def _():
        spo_ref[...] = sp_ref[...]


def _delta(h_qkv, z_dn, h_ba, cbuf8, cw8, alog_row, dtb_row, sp0, gdn_row, b, l, tl, ns, y_dtype):
    n_l = l // tl
    assert ns == 1 or n_l == 1
    n_slots = ns * (max(tl, DN_CHUNK) // DN_CHUNK)
    r = n_slots * DN_CHUNK
    consts = _delta_consts(r)
    row = lambda w: pl.BlockSpec((ns * tl, w), lambda bi, li: (bi * n_l + li, 0))
    cst = pl.BlockSpec((ns, CONV_HALO, CONV_CH), lambda bi, li: (bi, 0, 0))
    spb = pl.BlockSpec((ns, N_PAIRS, PAIR, PAIR), lambda bi, li: (bi, 0, 0, 0))
    wide = lambda n: pltpu.VMEM((r, n), F32)
    return pl.pallas_call(
        functools.partial(_delta_kernel, ns=ns, tl=tl, n_l=n_l),
        grid=(b // ns, n_l),
        in_specs=[row(CONV_CH), row(DN_WIDTH), row(LANES), cst, _const_spec(cw8.shape),
                  _const_spec(alog_row.shape), _const_spec(dtb_row.shape), spb, _const_spec(gdn_row.shape)]
                 + [_const_spec(c.shape) for c in consts],
        out_specs=[row(DN_WIDTH), cst, spb],
        out_shape=[jax.ShapeDtypeStruct((b * l, DN_WIDTH), y_dtype),
                   jax.ShapeDtypeStruct((b, CONV_HALO, CONV_CH), F32),
                   jax.ShapeDtypeStruct((b, N_PAIRS, PAIR, PAIR), F32)],
        scratch_shapes=[pltpu.VMEM((ns, CONV_HALO + tl, CONV_CH), F32), pltpu.VMEM((ns, N_PAIRS, PAIR, PAIR), F32)]
                       + [wide(DN_WIDTH)] * 7 + [wide(DN_HEADS * LANES)] + [wide(DN_WIDTH)] * 3
                       + [pltpu.VMEM((n_slots * N_PAIRS, PAIR, PAIR), BF16),
                          pltpu.VMEM((n_slots * N_PAIRS, PAIR, DN_CHUNK), BF16)],
        compiler_params=_cparams("parallel", "arbitrary"),
        name="gated_deltanet",
    )(h_qkv, z_dn, h_ba, cbuf8, cw8, alog_row, dtb_row, sp0, gdn_row, *consts)


def _merge_kernel(x_ref, ya_ref, yb_ref, yc_ref, g_ref, wa_ref, wb_ref, wc_ref, wo_ref, fg_ref, o_ref, *, final):
    merged = None
    for j, (y_ref, w_ref) in enumerate(((ya_ref, wa_ref), (yb_ref, wb_ref), (yc_ref, wc_ref))):
        br = _dot(y_ref[...].astype(BF16), w_ref[...])
        t = jax.nn.sigmoid(g_ref[:, D_MODEL * j:D_MODEL * (j + 1)]) * br
        merged = t if merged is None else merged + t
    x = x_ref[...] + _dot(merged.astype(BF16), wo_ref[...])
    o_ref[...] = _rms(x, fg_ref[...]) if final else x


def _merge(x2d, ya, yb, yc, h_gate, wa, wb, wc, wo, final_g, tm, final):
    t = x2d.shape[0]
    row = lambda w: pl.BlockSpec((tm, w), lambda i: (i, 0))
    return pl.pallas_call(
        functools.partial(_merge_kernel, final=final),
        grid=(t // tm,),
        in_specs=[row(D_MODEL), row(POOL_WIDTH), row(MLA_WIDTH), row(DN_WIDTH), row(N_BRANCH * D_MODEL)]
                 + [_const_spec(w.shape) for w in (wa, wb, wc, wo, final_g)],
        out_specs=row(D_MODEL),
        out_shape=jax.ShapeDtypeStruct((t, D_MODEL), F32),
        compiler_params=_cparams("parallel"),
        name="merge_out",
    )(x2d, ya, yb, yc, h_gate, wa, wb, wc, wo, final_g)


def _layer_weights(l, norm_g, wt_all, pool_mix, pool_scale, q_norm_g, w_uq, kv_norm_g, w_uk, w_uv,
                   conv_w, a_log, dt_bias, dn_norm_g, w_br_pool, w_br_mla, w_br_dn, w_out):
    o = IN_OFFSETS
    kr_rows = wt_all[l, o[KR_SPLIT]:o[KR_SPLIT + 1]]
    kr_blk = jnp.pad(kr_rows, ((ROPE_LANE0, HEAD_BLOCK - ROPE_LANE0 - QK_ROPE), (0, 0)))
    ba_blk = jnp.pad(wt_all[l, o[BETA_SPLIT]:o[ALPHA_SPLIT + 1]], ((0, LANES - 2 * DN_HEADS), (0, 0)))

    dq = QK_NOPE + QK_ROPE
    wuq = w_uq[l].reshape(Q_LORA, MLA_HEADS, dq)
    wuq_p = jnp.pad(wuq, ((0, 0), (0, 0), (0, HEAD_BLOCK - dq))).reshape(Q_LORA, -1).astype(BF16)
    wuk_p = jnp.pad(w_uk[l], ((0, 0), (0, 0), (0, HEAD_BLOCK - QK_NOPE))).reshape(KV_LORA, -1).astype(BF16)
    wuv_f = w_uv[l].reshape(KV_LORA, MLA_WIDTH).astype(BF16)
    wcat = jnp.zeros((MLA_HEADS, HEAD_BLOCK, KV_LORA + LANES), F32)
    wcat = wcat.at[:, 0:QK_NOPE, 0:KV_LORA].set(jnp.transpose(w_uk[l], (1, 2, 0)))
    sel = jnp.eye(QK_ROPE, dtype=F32)
    wcat = wcat.at[:, ROPE_LANE0:ROPE_LANE0 + QK_ROPE, KV_LORA:KV_LORA + QK_ROPE].set(sel)
    lanes16 = slice(DN_HEADS, 2 * DN_HEADS)
    return dict(
        norm_g=norm_g[l][None, :], kr_blk=kr_blk, ba_blk=ba_blk,
        pool_mix=pool_mix[l].astype(BF16), pool_scale=pool_scale[l][None, :],
        q_norm_g=q_norm_g[l][None, :], wuq_p=wuq_p, kv_norm_g=kv_norm_g[l][None, :],
        wuk_p=wuk_p, wuv_f=wuv_f, wcat=wcat.astype(BF16),
        cw8=jnp.pad(conv_w[l], ((0, SUBLANES - CONV_W), (0, 0))),
        alog_row=jnp.zeros((1, LANES), F32).at[0, lanes16].set(a_log[l]),
        dtb_row=jnp.zeros((1, LANES), F32).at[0, lanes16].set(dt_bias[l]),
        gdn_row=jnp.tile(dn_norm_g[l], DN_HEADS)[None, :],
        wa=w_br_pool[l].astype(BF16), wb=w_br_mla[l].astype(BF16), wc=w_br_dn[l].astype(BF16),
        wo=w_out[l].astype(BF16))


def _tile(n, target):
    t = min(n, target)
    while n % t:
        t -= SUBLANES
    return t


def _group_layer(x2d, b, l, start, w, wt_all, layer, tabs, final_g, final, pool_buf, conv_buf, sp0, paged):
    t = b * l
    small = l < 2 * SUBLANES
    y_dtype = F32 if small else BF16
    h_pool, z_pool, h_q, h_kv, h_kr, z_mla, h_qkv, z_dn, h_ba, h_gate = _inproj(
        x2d, w["norm_g"], wt_all, layer, w["kr_blk"], w["ba_blk"], _tile(t, 256))

    ya, pool_st = _pool(h_pool, z_pool, pool_buf, w["pool_mix"], w["pool_scale"], b, l, _tile(l, 512), start, y_dtype)

    tm = _tile(l, 512)
    if paged is None:
        ckv, kr_blk, q, k, v = _mla_prep(h_q, h_kv, h_kr, tabs, w["q_norm_g"], w["wuq_p"], w["kv_norm_g"],
                                         (w["wuk_p"], w["wuv_f"]), b, l, tm, absorbed=False)
        k_r = kr_blk[:, ROPE_LANE0:ROPE_LANE0 + QK_ROPE]
        yb = _attn_prompt(q, k, v, z_mla, b, l, _tile(l, 512))
    else:
        page_table, cache_kv, cache_krt = paged
        ckv, kr_blk, qcat = _mla_prep(h_q, h_kv, h_kr, tabs, w["q_norm_g"], w["wuq_p"], w["kv_norm_g"],
                                      (w["wcat"],), b, l, tm, absorbed=True)
        k_r = kr_blk[:, ROPE_LANE0:ROPE_LANE0 + QK_ROPE]
        wq = qcat.shape[-1] // MLA_HEADS
        qcat = qcat.reshape(b, l, MLA_HEADS, wq).transpose(0, 2, 1, 3).reshape(b, MLA_HEADS * l, wq)
        n_pages = page_table.shape[1]
        n_grp = PAGES_PER_GROUP
        while n_pages % (2 * n_grp):
            n_grp //= 2
        yb = _attn_paged(page_table, qcat, cache_kv, cache_krt, layer, ckv.reshape(b, l, KV_LORA),
                         k_r.reshape(b, l, QK_ROPE), w["wuv_f"], z_mla.reshape(b, l, MLA_WIDTH), n_grp)
        yb = yb.reshape(t, MLA_WIDTH)

    if l >= DN_CHUNK:
        tl, ns = _tile(l, 256), 1
    else:
        tl, ns = l, SAMPLE_SEQS_PER_STEP
        while b % ns:
            ns //= 2
    yc, conv_st, sp = _delta(h_qkv, z_dn, h_ba, conv_buf, w["cw8"], w["alog_row"], w["dtb_row"], sp0,
                             w["gdn_row"], b, l, tl, ns, y_dtype)

    x_out = _merge(x2d, ya, yb, yc, h_gate, w["wa"], w["wb"], w["wc"], w["wo"], final_g, _tile(t, 512), final)
    return x_out, ckv, k_r, pool_st[:, 1:], conv_st[:, CONV_HALO - (CONV_W - 1):], sp


def _pairs_to_blockdiag(s):
    b = s.shape[0]
    s = s.reshape(b, N_PAIRS, 2, DN_DK, DN_DV)
    z = jnp.zeros_like(s[:, :, 0])
    top = jnp.concatenate([s[:, :, 0], z], axis=-1)
    bot = jnp.concatenate([z, s[:, :, 1]], axis=-1)
    return jnp.concatenate([top, bot], axis=-2)


def _blockdiag_to_pairs(sp):
    b = sp.shape[0]
    s = jnp.stack([sp[:, :, :DN_DK, :DN_DV], sp[:, :, DN_DK:, DN_DV:]], axis=2)
    return s.reshape(b, DN_HEADS, DN_DK, DN_DV)


def kernel(x_prompt, x_sample, cache_kv_latent, cache_k_rope, state_pool, state_conv, state_delta,
           page_table, norm_g, w_in, pool_mix, pool_scale, q_norm_g, w_uq, kv_norm_g, w_uk, w_uv,
           conv_w, a_log, dt_bias, dn_norm_g, w_br_pool, w_br_mla, w_br_dn, w_out, final_norm_g):
    bp, lp, d = x_prompt.shape
    db, ls, _ = x_sample.shape
    depth = w_in.shape[0]
    past_len = page_table.shape[1] * PAGE_SIZE
    final_g = final_norm_g[None, :]
    cache_krt = jnp.swapaxes(cache_k_rope, 2, 3)
    wt_all = jnp.swapaxes(w_in, 1, 2).astype(BF16)

    tabs_p = _rope_tables(lp, _tile(lp, 512), 0)
    tabs_s = _rope_tables(ls, ls, past_len)
    zero_pool = jnp.zeros((bp, POOL_HALO, POOL_WIDTH), F32)
    zero_conv = jnp.zeros((bp, CONV_HALO, CONV_CH), F32)
    zero_sp = jnp.zeros((bp, N_PAIRS, PAIR, PAIR), F32)

    xp = x_prompt.reshape(bp * lp, d)
    xs = x_sample.reshape(db * ls, d)
    outs_p, outs_s = [], []
    for l in range(depth):
        w = _layer_weights(l, norm_g, wt_all, pool_mix, pool_scale, q_norm_g, w_uq, kv_norm_g, w_uk, w_uv,
                           conv_w, a_log, dt_bias, dn_norm_g, w_br_pool, w_br_mla, w_br_dn, w_out)
        final = l == depth - 1
        xp, ckv, kr, pst, cst, sp = _group_layer(xp, bp, lp, 0, w, wt_all, l, tabs_p, final_g, final,
                                                 zero_pool, zero_conv, zero_sp, None)
        outs_p.append((ckv.reshape(bp, lp, KV_LORA), kr.reshape(bp, lp, QK_ROPE), pst, cst,
                       _blockdiag_to_pairs(sp)))
        pool_buf = jnp.pad(state_pool[l], ((0, 0), (POOL_HALO - POOL_BUF, 0), (0, 0)))
        conv_buf = jnp.pad(state_conv[l], ((0, 0), (CONV_HALO - (CONV_W - 1), 0), (0, 0)))
        xs, ckv, kr, pst, cst, sp = _group_layer(xs, db, ls, past_len, w, wt_all, l, tabs_s, final_g, final,
                                                 pool_buf, conv_buf, _pairs_to_blockdiag(state_delta[l]),
                                                 (page_table, cache_kv_latent, cache_krt))
        outs_s.append((ckv.reshape(db, ls, KV_LORA), kr.reshape(db, ls, QK_ROPE), pst, cst,
                       _blockdiag_to_pairs(sp)))
    stack = lambda outs, i: jnp.stack([o[i] for o in outs])
    return (xp.reshape(bp, lp, d), xs.reshape(db, ls, d),
            *(stack(outs_p, i) for i in range(5)), *(stack(outs_s, i) for i in range(5)))
```

```python
import functools

import numpy as np
import jax
import jax.numpy as jnp
from jax import lax
from jax.experimental import pallas as pl
from jax.experimental.pallas import tpu as pltpu

F32 = jnp.float32
BF16 = jnp.bfloat16

D_MODEL = 1024
EPS = 1e-6
POOL_WINDOWS = (2, 4, 8, 16)
POOL_GROUP_DIM = 128
POOL_WIDTH = 512
POOL_BUF = 15
MLA_HEADS = 8
QK_NOPE = 64
QK_ROPE = 32
V_HEAD = 64
Q_LORA = 384
KV_LORA = 256
MLA_WIDTH = 512
MLA_SCALE = (QK_NOPE + QK_ROPE) ** -0.5
ROPE_THETA = 10000.0
DN_HEADS = 8
DN_DK = 64
DN_DV = 64
DN_WIDTH = 512
CONV_W = 4
CONV_CH = 1536
DN_CHUNK = 64
PAGE_SIZE = 128
N_BRANCH = 3

LANES = 128
SUBLANES = 8
HEAD_BLOCK = LANES
ROPE_LANE0 = QK_NOPE
ROPE_HALF = QK_ROPE // 2
POOL_HALO = 16
CONV_HALO = 8
VMEM_LIMIT = 48 * 1024 * 1024
NEG = -1e30
LOG2E = 1.4426950408889634
Q_SCALE = MLA_SCALE * LOG2E
PAGES_PER_GROUP = 32
PAGED_SLOTS = 3
PAGED_SUBBLOCKS = 8
PAGED_ROW_CHUNK = 16


def _cparams(*sem):
    return pltpu.CompilerParams(dimension_semantics=sem, vmem_limit_bytes=VMEM_LIMIT)


def _const_spec(shape):
    nd = len(shape)
    return pl.BlockSpec(shape, lambda *_: (0,) * nd, pipeline_mode=pl.Buffered(1))


def _dot(a, b):
    return jnp.dot(a, b, preferred_element_type=F32)


def _dot_nt(a, b):
    return lax.dot_general(a, b, (((1,), (1,)), ((), ())), preferred_element_type=F32)


def _silu(x):
    return x * jax.nn.sigmoid(x)


def _rms(x, g):
    return x * lax.rsqrt(jnp.mean(x * x, axis=-1, keepdims=True) + EPS) * g


IN_SPLITS = (POOL_WIDTH, POOL_WIDTH, Q_LORA, KV_LORA, QK_ROPE, MLA_WIDTH, CONV_CH, DN_WIDTH,
             DN_HEADS, DN_HEADS, N_BRANCH * D_MODEL)
IN_OFFSETS = tuple(int(v) for v in np.cumsum((0,) + IN_SPLITS))
IN_WIDTH = IN_OFFSETS[-1]
KR_SPLIT, BETA_SPLIT, ALPHA_SPLIT = 4, 8, 9
IN_OUTPUTS = ((IN_OFFSETS[0], POOL_WIDTH), (IN_OFFSETS[1], POOL_WIDTH), (IN_OFFSETS[2], Q_LORA),
              (IN_OFFSETS[3], KV_LORA), (None, HEAD_BLOCK), (IN_OFFSETS[5], MLA_WIDTH),
              (IN_OFFSETS[6], CONV_CH), (IN_OFFSETS[7], DN_WIDTH), (None, LANES),
              (IN_OFFSETS[10], N_BRANCH * D_MODEL))
IN_DOT_CHUNK = 512


def _inproj_kernel(x_ref, g_ref, w_ref, kr_ref, ba_ref, *o_refs):
    xn = _rms(x_ref[...], g_ref[...]).astype(BF16)
    small = iter((kr_ref, ba_ref))
    for o_ref, (row0, width) in zip(o_refs, IN_OUTPUTS):
        if row0 is None:
            o_ref[...] = _dot_nt(xn, next(small)[...])
            continue
        for c0 in range(0, width, IN_DOT_CHUNK):
            c1 = min(c0 + IN_DOT_CHUNK, width)
            o_ref[:, c0:c1] = _dot_nt(xn, w_ref[row0 + c0:row0 + c1, :])


def _inproj(x2d, norm_g, wt_all, layer, kr_blk, ba_blk, tm):
    t = x2d.shape[0]
    row = lambda w: pl.BlockSpec((tm, w), lambda i: (i, 0))
    w_spec = pl.BlockSpec((None,) + wt_all.shape[1:], lambda i: (layer, 0, 0), pipeline_mode=pl.Buffered(1))
    return pl.pallas_call(
        _inproj_kernel,
        grid=(t // tm,),
        in_specs=[row(D_MODEL), _const_spec((1, D_MODEL)), w_spec,
                  _const_spec(kr_blk.shape), _const_spec(ba_blk.shape)],
        out_specs=[row(w) for _, w in IN_OUTPUTS],
        out_shape=[jax.ShapeDtypeStruct((t, w), F32) for _, w in IN_OUTPUTS],
        compiler_params=_cparams("parallel"),
        name="inproj",
    )(x2d, norm_g, wt_all, kr_blk, ba_blk)


def _pool_kernel(u_ref, z_ref, buf_ref, mix_ref, scale_ref, y_ref, st_ref, xx_ref, *, tl, start, n_l):
    li = pl.program_id(1)

    @pl.when(li == 0)
    def _():
        xx_ref[0:POOL_HALO, :] = buf_ref[...]

    u = u_ref[...]
    xx_ref[POOL_HALO:POOL_HALO + tl, :] = u
    row = lax.broadcasted_iota(jnp.int32, (tl, 1), 0)
    pos1 = start + li * tl + row + 1
    for gi, w in enumerate(POOL_WINDOWS):
        lanes = slice(POOL_GROUP_DIM * gi, POOL_GROUP_DIM * (gi + 1))
        s = u[:, lanes]
        for o in range(1, w):
            s = s + xx_ref[POOL_HALO - o:POOL_HALO - o + tl, lanes]
        cnt = jnp.minimum(pos1, w).astype(F32)
        d = s / cnt - u[:, lanes]
        y = _dot(d.astype(BF16), mix_ref[gi]) * scale_ref[:, lanes]
        y_ref[:, lanes] = (y * _silu(z_ref[:, lanes])).astype(y_ref.dtype)

    tail = xx_ref[tl:tl + POOL_HALO, :]

    @pl.when(li == n_l - 1)
    def _():
        st_ref[...] = tail

    xx_ref[0:POOL_HALO, :] = tail


def _pool(h_pool, z_pool, buf16, mix, scale, b, l, tl, start, y_dtype):
    n_l = l // tl
    row = lambda w: pl.BlockSpec((tl, w), lambda bi, li: (bi * n_l + li, 0))
    st = pl.BlockSpec((None, POOL_HALO, POOL_WIDTH), lambda bi, li: (bi, 0, 0))
    return pl.pallas_call(
        functools.partial(_pool_kernel, tl=tl, start=start, n_l=n_l),
        grid=(b, n_l),
        in_specs=[row(POOL_WIDTH), row(POOL_WIDTH), st, _const_spec(mix.shape), _const_spec(scale.shape)],
        out_specs=[row(POOL_WIDTH), st],
        out_shape=[jax.ShapeDtypeStruct((b * l, POOL_WIDTH), y_dtype),
                   jax.ShapeDtypeStruct((b, POOL_HALO, POOL_WIDTH), F32)],
        scratch_shapes=[pltpu.VMEM((POOL_HALO + tl, POOL_WIDTH), F32)],
        compiler_params=_cparams("parallel", "arbitrary"),
        name="pool_mixer",
    )(h_pool, z_pool, buf16, mix, scale)


def _rope_table_kernel(inv_ref, c_ref, s1_ref, s2_ref, *, tl, start):
    i = pl.program_id(0)
    shape = (tl, LANES)
    pos = (start + i * tl + lax.broadcasted_iota(jnp.int32, shape, 0)).astype(F32)
    lane = lax.broadcasted_iota(jnp.int32, shape, 1)
    ang = pos * inv_ref[...]
    cos, sin = jnp.cos(ang), jnp.sin(ang)
    first = (lane >= ROPE_LANE0) & (lane < ROPE_LANE0 + ROPE_HALF)
    second = (lane >= ROPE_LANE0 + ROPE_HALF) & (lane < ROPE_LANE0 + QK_ROPE)
    c_ref[...] = jnp.where(first | second, cos, 1.0)
    s1_ref[...] = jnp.where(first, -sin, 0.0)
    s2_ref[...] = jnp.where(second, sin, 0.0)


def _rope_tables(l, tl, start):
    half = ROPE_HALF
    inv = jnp.power(ROPE_THETA, -jnp.arange(half, dtype=F32) / half)
    inv_lane = jnp.zeros((1, LANES), F32)
    inv_lane = inv_lane.at[0, ROPE_LANE0:ROPE_LANE0 + half].set(inv)
    inv_lane = inv_lane.at[0, ROPE_LANE0 + half:ROPE_LANE0 + QK_ROPE].set(inv)
    blk = pl.BlockSpec((tl, LANES), lambda i: (i, 0))
    return pl.pallas_call(
        functools.partial(_rope_table_kernel, tl=tl, start=start),
        grid=(l // tl,),
        in_specs=[_const_spec((1, LANES))],
        out_specs=[blk, blk, blk],
        out_shape=[jax.ShapeDtypeStruct((l, LANES), F32)] * 3,
        compiler_params=_cparams("parallel"),
        name="rope_tables",
    )(inv_lane)


def _rope(x, c, s1, s2):
    return x * c + pltpu.roll(x, LANES - ROPE_HALF, 1) * s1 + pltpu.roll(x, ROPE_HALF, 1) * s2


def _mla_prep_kernel(hq_ref, hkv_ref, hkr_ref, c_ref, s1_ref, s2_ref, qg_ref, wuq_ref, kvg_ref, *rest,
                     absorbed):
    c, s1, s2 = c_ref[...], s1_ref[...], s2_ref[...]
    cq = _rms(hq_ref[...], qg_ref[...]).astype(BF16)
    q = _dot(cq, wuq_ref[...])
    ckv = _rms(hkv_ref[...], kvg_ref[...])
    kr = _rope(hkr_ref[...], c, s1, s2)
    if absorbed:
        wcat_ref, ckv_ref, kr_ref, qcat_ref = rest
    else:
        wuk_ref, wuv_ref, ckv_ref, kr_ref, q_ref, k_ref, v_ref = rest
        ckv16 = ckv.astype(BF16)
        knope = _dot(ckv16, wuk_ref[...])
        v_ref[...] = _dot(ckv16, wuv_ref[...]).astype(BF16)
    ckv_ref[...] = ckv
    kr_ref[...] = pltpu.roll(kr, LANES - ROPE_LANE0, 1)[:, 0:QK_ROPE]
    for h in range(MLA_HEADS):
        blk = slice(HEAD_BLOCK * h, HEAD_BLOCK * (h + 1))
        qh = (_rope(q[:, blk], c, s1, s2) * Q_SCALE).astype(BF16)
        if absorbed:
            w = wcat_ref.shape[-1]
            qcat_ref[:, w * h:w * (h + 1)] = _dot(qh, wcat_ref[h]).astype(BF16)
        else:
            q_ref[:, blk] = qh
            k_ref[:, blk] = (knope[:, blk] + kr).astype(BF16)


def _mla_prep(h_q, h_kv, h_kr, tabs, q_norm_g, wuq_p, kv_norm_g, extra_w, b, l, tm, absorbed):
    n_l = l // tm
    t = b * l
    row = lambda w: pl.BlockSpec((tm, w), lambda bi, li: (bi * n_l + li, 0))
    tab = pl.BlockSpec((tm, LANES), lambda bi, li: (li, 0))
    hw = MLA_HEADS * HEAD_BLOCK
    in_specs = [row(Q_LORA), row(KV_LORA), row(HEAD_BLOCK), tab, tab, tab,
                _const_spec(q_norm_g.shape), _const_spec(wuq_p.shape), _const_spec(kv_norm_g.shape)]
    in_specs += [_const_spec(w.shape) for w in extra_w]
    out_specs = [row(KV_LORA), row(QK_ROPE)]
    out_shape = [jax.ShapeDtypeStruct((t, KV_LORA), F32), jax.ShapeDtypeStruct((t, QK_ROPE), F32)]
    if absorbed:
        wc = MLA_HEADS * extra_w[0].shape[-1]
        out_specs += [row(wc)]
        out_shape += [jax.ShapeDtypeStruct((t, wc), BF16)]
    else:
        out_specs += [row(hw), row(hw), row(MLA_WIDTH)]
        out_shape += [jax.ShapeDtypeStruct((t, hw), BF16), jax.ShapeDtypeStruct((t, hw), BF16),
                      jax.ShapeDtypeStruct((t, MLA_WIDTH), BF16)]
    return pl.pallas_call(
        functools.partial(_mla_prep_kernel, absorbed=absorbed),
        grid=(b, n_l),
        in_specs=in_specs, out_specs=out_specs, out_shape=out_shape,
        compiler_params=_cparams("parallel", "parallel"),
        name="mla_prep_absorbed" if absorbed else "mla_prep",
    )(h_q, h_kv, h_kr, *tabs, q_norm_g, wuq_p, kv_norm_g, *extra_w)


ATTN_ROW_CHUNK = 32


def _softmax_update(m_b, l_p, s, row_chunk):
    r, n = s.shape[0], s.shape[1] // LANES
    cols = [slice(LANES * c, LANES * (c + 1)) for c in range(n)]
    folded = s[:, cols[0]]
    for cb in cols[1:]:
        folded = jnp.maximum(folded, s[:, cb])
    mx = jnp.max(folded, axis=-1, keepdims=True)
    m_new = jnp.maximum(m_b, jnp.broadcast_to(mx, m_b.shape))
    a_b = jnp.exp2(m_b - m_new)
    l_rows, p_rows = [], []
    for c0 in range(0, r, row_chunk):
        rows = slice(c0, c0 + row_chunk)
        m_c = m_new[rows]
        pieces = [jnp.exp2(s[rows, cb] - m_c) for cb in cols]
        tot = pieces[0]
        for pc in pieces[1:]:
            tot = tot + pc
        l_rows.append(a_b[rows] * l_p[rows] + tot)
        p_rows.append(jnp.concatenate([pc.astype(BF16) for pc in pieces], axis=1))
    cat = lambda parts: parts[0] if len(parts) == 1 else jnp.concatenate(parts, axis=0)
    return m_new, cat(l_rows), a_b, cat(p_rows)


def _attn_kernel(q_ref, k_ref, v_ref, z_ref, y_ref, *, ta):
    qi = pl.program_id(1)
    rc = min(ATTN_ROW_CHUNK, ta)
    low_half = lax.broadcasted_iota(jnp.int32, (ta, LANES), 1) < V_HEAD
    causal = (lax.broadcasted_iota(jnp.int32, (ta, ta), 1)
              <= lax.broadcasted_iota(jnp.int32, (ta, ta), 0))
    for p in range(MLA_HEADS // 2):
        pair = slice(LANES * p, LANES * (p + 1))
        blks = [slice(HEAD_BLOCK * h, HEAD_BLOCK * (h + 1)) for h in (2 * p, 2 * p + 1)]
        qhs = [q_ref[:, blk] for blk in blks]

        def step(kb, carry, masked, pair=pair, blks=blks, qhs=qhs):
            r0 = pl.multiple_of(kb * ta, ta)
            vv = v_ref[pl.ds(r0, ta), pair]
            scores = [_dot_nt(qh, k_ref[pl.ds(r0, ta), blk]) for qh, blk in zip(qhs, blks)]
            out = []
            for (m_b, l_p, acc), s in zip(carry, scores):
                if masked:
                    s = jnp.where(causal, s, NEG)
                m_b, l_p, a_b, pm = _softmax_update(m_b, l_p, s, rc)
                out.append((m_b, l_p, a_b * acc + _dot(pm, vv)))
            return tuple(out)

        init = (jnp.full((ta, LANES), NEG, F32), jnp.zeros((ta, LANES), F32), jnp.zeros((ta, LANES), F32))
        carry = lax.fori_loop(0, qi, functools.partial(step, masked=False), (init, init))
        (_, l0, acc0), (_, l1, acc1) = step(qi, carry, True)
        den = lambda l_p: jnp.sum(l_p, axis=-1, keepdims=True)
        o = jnp.where(low_half, acc0 / den(l0), acc1 / den(l1))
        y_ref[:, pair] = (o * _silu(z_ref[:, pair])).astype(y_ref.dtype)


def _attn_prompt(q, k, v, z_mla, b, l, ta):
    n_q = l // ta
    hw = MLA_HEADS * HEAD_BLOCK
    row = lambda w: pl.BlockSpec((ta, w), lambda bi, qi: (bi * n_q + qi, 0))
    seq = lambda w: pl.BlockSpec((l, w), lambda bi, qi: (bi, 0))
    return pl.pallas_call(
        functools.partial(_attn_kernel, ta=ta),
        grid=(b, n_q),
        in_specs=[row(hw), seq(hw), seq(MLA_WIDTH), row(MLA_WIDTH)],
        out_specs=row(MLA_WIDTH),
        out_shape=jax.ShapeDtypeStruct((b * l, MLA_WIDTH), BF16),
        compiler_params=_cparams("parallel", "arbitrary"),
        name="mla_attention_prompt",
    )(q, k, v, z_mla)


def _attn_paged_kernel(pt_ref, qcat_ref, nkv_ref, nkr_ref, wuv_ref, z_ref, kv_hbm, krt_hbm, y_ref,
                       kvbuf, krbuf, sem, *, layer, n_pages, n_grp, l_new):
    b = pl.program_id(0)
    n_seq = pl.num_programs(0)
    n_groups = n_pages // n_grp
    rows = MLA_HEADS * l_new
    ahead = PAGED_SLOTS - 1

    def group_copies(seq, j):
        slot = (seq * n_groups + j) % PAGED_SLOTS
        cps = []
        for g in range(n_grp):
            page = pt_ref[seq * n_pages + j * n_grp + g]
            keys = pl.ds(PAGE_SIZE * g, PAGE_SIZE)
            cps.append(pltpu.make_async_copy(kv_hbm.at[layer, page], kvbuf.at[slot, keys, :], sem.at[0, slot]))
            cps.append(pltpu.make_async_copy(krt_hbm.at[layer, page], krbuf.at[slot, :, keys], sem.at[1, slot]))
        return cps

    def start_group(seq, j):
        for cp in group_copies(seq, j):
            cp.start()

    def wait_group(seq, j):
        for cp in group_copies(seq, j):
            cp.wait()

    def start_ahead(j):
        nxt = j + ahead
        if nxt < n_groups:
            start_group(b, nxt)
        else:
            @pl.when(b + 1 < n_seq)
            def _():
                start_group(b + 1, nxt - n_groups)

    @pl.when(b == 0)
    def _():
        for j0 in range(ahead):
            start_group(b, j0)

    qc = qcat_ref[...]
    ql, qr = qc[:, :KV_LORA], qc[:, KV_LORA:KV_LORA + QK_ROPE]

    def update(state, s, kv16):
        m_b, l_p, acc = state
        m_b, l_p, a_b, pm = _softmax_update(m_b, l_p, s, PAGED_ROW_CHUNK)
        a_wide = jnp.concatenate([a_b] * (KV_LORA // LANES), axis=1)
        return m_b, l_p, a_wide * acc + _dot(pm, kv16)

    state = (jnp.full((rows, LANES), NEG, F32), jnp.zeros((rows, LANES), F32), jnp.zeros((rows, KV_LORA), F32))
    sub_keys = n_grp * PAGE_SIZE // PAGED_SUBBLOCKS
    for j in range(n_groups):
        start_ahead(j)
        wait_group(b, j)
        slot = (b * n_groups + j) % PAGED_SLOTS
        subs = []
        for sb in range(PAGED_SUBBLOCKS):
            keys = slice(sub_keys * sb, sub_keys * (sb + 1))
            kv16 = kvbuf[slot, keys, :].astype(BF16)
            kr16 = krbuf[slot, :, keys].astype(BF16)
            subs.append((_dot_nt(ql, kv16) + _dot(qr, kr16), kv16))
        for s, kv16 in subs:
            state = update(state, s, kv16)

    pad = PAGE_SIZE - l_new
    nkv = jnp.concatenate([nkv_ref[...], jnp.zeros((pad, KV_LORA), F32)], axis=0).astype(BF16)
    nkr = jnp.concatenate([nkr_ref[...], jnp.zeros((pad, QK_ROPE), F32)], axis=0).astype(BF16)
    tok = lax.broadcasted_iota(jnp.int32, (rows, PAGE_SIZE), 0) % l_new
    key = lax.broadcasted_iota(jnp.int32, (rows, PAGE_SIZE), 1)
    _, l_p, acc = update(state, jnp.where(key <= tok, _dot_nt(ql, nkv) + _dot_nt(qr, nkr), NEG), nkv)
    o = (acc / jnp.sum(l_p, axis=-1, keepdims=True)).astype(BF16)
    full = _dot(o, wuv_ref[...])
    rh = lax.broadcasted_iota(jnp.int32, full.shape, 0) // l_new
    ch = lax.broadcasted_iota(jnp.int32, full.shape, 1) // V_HEAD
    full = jnp.where(rh == ch, full, 0.0)
    out = full[0:l_new]
    for h in range(1, MLA_HEADS):
        out = out + full[l_new * h:l_new * (h + 1)]
    y_ref[...] = (out * _silu(z_ref[...])).astype(y_ref.dtype)


def _attn_paged(page_table, qcat, cache_kv, cache_krt, layer, new_kv, new_kr, wuv, z_mla, n_grp):
    db, n_pages = page_table.shape
    l_new = new_kv.shape[1]
    rows = MLA_HEADS * l_new
    assert n_pages % n_grp == 0 and n_pages // n_grp >= PAGED_SLOTS - 1 and n_grp % PAGED_SUBBLOCKS == 0
    wq = qcat.shape[-1]
    per_b = lambda r, w: pl.BlockSpec((None, r, w), lambda b, pt: (b, 0, 0))
    hbm = pl.BlockSpec(memory_space=pl.ANY)
    grid_spec = pltpu.PrefetchScalarGridSpec(
        num_scalar_prefetch=1, grid=(db,),
        in_specs=[per_b(rows, wq), per_b(l_new, KV_LORA), per_b(l_new, QK_ROPE),
                  pl.BlockSpec(wuv.shape, lambda b, pt: (0, 0)), per_b(l_new, MLA_WIDTH), hbm, hbm],
        out_specs=per_b(l_new, MLA_WIDTH),
        scratch_shapes=[pltpu.VMEM((PAGED_SLOTS, n_grp * PAGE_SIZE, KV_LORA), F32),
                        pltpu.VMEM((PAGED_SLOTS, QK_ROPE, n_grp * PAGE_SIZE), F32),
                        pltpu.SemaphoreType.DMA((2, PAGED_SLOTS))])
    return pl.pallas_call(
        functools.partial(_attn_paged_kernel, layer=layer, n_pages=n_pages, n_grp=n_grp, l_new=l_new),
        grid_spec=grid_spec,
        out_shape=jax.ShapeDtypeStruct((db, l_new, MLA_WIDTH), F32),
        compiler_params=_cparams("arbitrary"),
        name="mla_attention_paged",
    )(page_table.reshape(-1), qcat, new_kv, new_kr, wuv, z_mla, cache_kv, cache_krt)


PAIR = 2 * DN_CHUNK
N_PAIRS = DN_HEADS // 2
NEUMANN_STEPS = 5
SLOT_GROUP = 4
SAMPLE_SEQS_PER_STEP = 4


def _delta_consts(r):
    hid = np.arange(PAIR) // DN_DK
    bd = hid[:, None] == hid[None, :]
    rr = np.arange(r)
    tri = (rr[:, None] // DN_CHUNK == rr[None, :] // DN_CHUNK) & (rr[:, None] >= rr[None, :])
    return tuple(jnp.asarray(a.astype(np.float32), dtype=BF16) for a in (bd, tri))


def _split3(x):
    hi = x.astype(BF16)
    r1 = x - hi.astype(F32)
    mid = r1.astype(BF16)
    lo = (r1 - mid.astype(F32)).astype(BF16)
    return hi, mid, lo


def _sel_dot(sel, x):
    hi, mid, lo = _split3(x)
    return _dot(sel, hi) + _dot(sel, mid) + _dot(sel, lo)


def _head_sums(x2, bd):
    x16 = x2.astype(BF16)
    return jnp.concatenate([_dot(x16[:, LANES * p:LANES * (p + 1)], bd) for p in range(N_PAIRS)], axis=1)


def _delta_kernel(u_ref, z_ref, ba_ref, cbuf_ref, cw_ref, alog_ref, dtb_ref, sp0_ref, gdn_ref,
                  bd_ref, tri_ref,
                  y_ref, cst_ref, spo_ref,
                  xx_ref, sp_ref, q_s, k_s, kb_s, qe_s, vb_s, kbe_s, gcl_s, gf_s, u_s, w_s, o_s, qkd_s, kdt_s,
                  *, ns, tl, n_l):
    li = pl.program_id(1)
    cps = max(tl, DN_CHUNK) // DN_CHUNK
    seq_rows = cps * DN_CHUNK
    n_slots = ns * cps

    @pl.when(li == 0)
    def _():
        xx_ref[:, 0:CONV_HALO, :] = cbuf_ref[...]
        zero = jnp.zeros((DN_DK, DN_DV), F32)
        for s in range(ns):
            for p in range(N_PAIRS):
                top = jnp.concatenate([sp0_ref[s, 2 * p], zero], axis=1)
                bot = jnp.concatenate([zero, sp0_ref[s, 2 * p + 1]], axis=1)
                sp_ref[s, p] = jnp.concatenate([top, bot], axis=0)

    def pad_rows(a):
        if tl == seq_rows:
            return a
        return jnp.concatenate([a, jnp.zeros((seq_rows - tl, a.shape[1]), a.dtype)], axis=0)

    def per_seq(a):
        if ns == 1:
            return pad_rows(a)
        return jnp.concatenate([pad_rows(a[s * tl:(s + 1) * tl]) for s in range(ns)], axis=0)

    pieces = []
    for s in range(ns):
        xx_ref[s, CONV_HALO:CONV_HALO + tl, :] = u_ref[s * tl:(s + 1) * tl, :]
        ext = xx_ref[s]
        acc = None
        for j in range(CONV_W):
            back = CONV_W - 1 - j
            shifted = ext if back == 0 else pltpu.roll(ext, back, 0)
            term = shifted[CONV_HALO:CONV_HALO + tl] * cw_ref[j:j + 1, :]
            acc = term if acc is None else acc + term
        pieces.append(pad_rows(_silu(acc)))
    qkv = pieces[0] if ns == 1 else jnp.concatenate(pieces, axis=0)
    tail = xx_ref[:, tl:tl + CONV_HALO, :]

    @pl.when(li == n_l - 1)
    def _():
        cst_ref[...] = tail

    xx_ref[:, 0:CONV_HALO, :] = tail

    bd = bd_ref[...]
    q = qkv[:, 0:DN_WIDTH]
    k = qkv[:, DN_WIDTH:2 * DN_WIDTH]
    v = qkv[:, 2 * DN_WIDTH:3 * DN_WIDTH]
    qn = q * lax.rsqrt(_head_sums(q * q, bd) + EPS) * (DN_DK ** -0.5)
    kn = k * lax.rsqrt(_head_sums(k * k, bd) + EPS)
    ba = ba_ref[...]
    lane = lax.broadcasted_iota(jnp.int32, ba.shape, 1)
    beta = jnp.where(lane < DN_HEADS, jax.nn.sigmoid(ba), 0.0)
    xa = ba + dtb_ref[...]
    softplus = jnp.maximum(xa, 0.0) + jnp.log1p(jnp.exp(-jnp.abs(xa)))
    g = jnp.where((lane >= DN_HEADS) & (lane < 2 * DN_HEADS), -jnp.exp(alog_ref[...]) * softplus, 0.0)
    beta, g = per_seq(beta), per_seq(g)
    gc = _sel_dot(tri_ref[...], g)
    n_rows = gc.shape[0]
    first_r = lax.broadcasted_iota(jnp.int32, (n_rows, LANES), 1) < DN_DK

    def head_lanes(a, lane0):
        full = [jnp.broadcast_to(a[:, lane0 + h:lane0 + h + 1], (n_rows, LANES)) for h in range(DN_HEADS)]
        pairs = [jnp.where(first_r, full[2 * p], full[2 * p + 1]) for p in range(N_PAIRS)]
        return jnp.concatenate(pairs, axis=1), full

    beta_l, _ = head_lanes(beta, 0)
    gc_l, gc_full = head_lanes(gc, DN_HEADS)
    egc = jnp.exp(gc_l)
    kb = kn * beta_l
    q_s[...] = qn
    k_s[...] = kn
    kb_s[...] = kb
    qe_s[...] = qn * egc
    vb_s[...] = v * beta_l
    kbe_s[...] = kb * egc
    gcl_s[...] = gc_l
    for h in range(DN_HEADS):
        gf_s[:, LANES * h:LANES * (h + 1)] = gc_full[h]

    ri = lax.broadcasted_iota(jnp.int32, (PAIR, PAIR), 0)
    ci = lax.broadcasted_iota(jnp.int32, (PAIR, PAIR), 1)
    same = (ri // DN_CHUNK) == (ci // DN_CHUNK)
    incl = same & (ri >= ci)
    strict = same & (ri > ci)
    eye = (ri == ci).astype(F32)
    first = lax.broadcasted_iota(jnp.int32, (DN_CHUNK, LANES), 1) < DN_DK

    def stack(a):
        return jnp.concatenate([jnp.where(first, a, 0.0), jnp.where(first, 0.0, a)], axis=0)

    def fold(a):
        return a[0:DN_CHUNK] + a[DN_CHUNK:PAIR]

    rows = lambda c: slice(DN_CHUNK * c, DN_CHUNK * (c + 1))
    lanes = lambda p: slice(LANES * p, LANES * (p + 1))

    def solve_slots(slots):
        chains = [(c, p) for c in slots for p in range(N_PAIRS)]
        kst = [stack(k_s[rows(c), lanes(p)]).astype(BF16) for c, p in chains]
        a_mat = [_dot_nt(stack(kb_s[rows(c), lanes(p)]).astype(BF16), ks) for (c, p), ks in zip(chains, kst)]
        qk = [_dot_nt(stack(q_s[rows(c), lanes(p)]).astype(BF16), ks) for (c, p), ks in zip(chains, kst)]
        dec = []
        for c, p in chains:
            g_col = jnp.concatenate([gf_s[rows(c), lanes(2 * p)], gf_s[rows(c), lanes(2 * p + 1)]], axis=0)
            diff = g_col - g_col.T
            dec.append(jnp.where(incl, jnp.exp(jnp.where(incl, diff, 0.0)), 0.0))
        pw = [jnp.where(strict, -(a * d), 0.0) for a, d in zip(a_mat, dec)]
        for (c, p), x, d in zip(chains, qk, dec):
            qkd_s[c * N_PAIRS + p] = (x * d).astype(BF16)
        t_inv = [eye + n for n in pw]
        for _ in range(NEUMANN_STEPS):
            pw16 = [x.astype(BF16) for x in pw]
            pw = [_dot(x, x) for x in pw16]
            t_inv = [t + _dot(t.astype(BF16), x.astype(BF16)) for t, x in zip(t_inv, pw)]
        for (c, p), t in zip(chains, t_inv):
            rhs = jnp.concatenate([stack(vb_s[rows(c), lanes(p)]), stack(kbe_s[rows(c), lanes(p)])], axis=1)
            uw = fold(_dot(t.astype(BF16), rhs.astype(BF16)))
            u_s[rows(c), lanes(p)] = uw[:, 0:LANES]
            w_s[rows(c), lanes(p)] = uw[:, LANES:2 * LANES]
            gcl = gcl_s[rows(c), lanes(p)]
            k_dec = k_s[rows(c), lanes(p)] * jnp.exp(gcl[DN_CHUNK - 1:DN_CHUNK, :] - gcl)
            kdt_s[c * N_PAIRS + p] = k_dec.T.astype(BF16)

    for c0 in range(0, n_slots, SLOT_GROUP):
        solve_slots(range(c0, min(c0 + SLOT_GROUP, n_slots)))

    pairs = range(N_PAIRS)
    for c in range(n_slots):
        sq = c // cps
        sps = [sp_ref[sq, p] for p in pairs]
        wq = [jnp.concatenate([w_s[rows(c), lanes(p)], qe_s[rows(c), lanes(p)]], axis=0).astype(BF16)
              for p in pairs]
        res = [_dot(x, sp.astype(BF16)) for x, sp in zip(wq, sps)]
        v_new = [u_s[rows(c), lanes(p)] - res[p][0:DN_CHUNK] for p in pairs]
        intra = [fold(_dot(qkd_s[c * N_PAIRS + p], stack(v_new[p]).astype(BF16))) for p in pairs]
        upd = [_dot(kdt_s[c * N_PAIRS + p], v_new[p].astype(BF16)) for p in pairs]
        for p in pairs:
            o_s[rows(c), lanes(p)] = res[p][DN_CHUNK:PAIR] + intra[p]
            g_last = gcl_s[DN_CHUNK * (c + 1) - 1:DN_CHUNK * (c + 1), lanes(p)]
            sp_ref[sq, p] = sps[p] * jnp.exp(g_last) + jnp.where(same, upd[p], 0.0)

    o = o_s[...]
    y = o * lax.rsqrt(_head_sums(o * o, bd) * (1.0 / DN_DV) + EPS) * gdn_ref[...]
    for s in range(ns):
        zs = z_ref[s * tl:(s + 1) * tl, :]
        y_ref[s * tl:(s + 1) * tl, :] = (y[s * seq_rows:s * seq_rows + tl] * _silu(zs)).astype(y_ref.dtype)

    @pl.when(li == n_l - 1)
    def _():
        for s in range(ns):
            for p in range(N_PAIRS):
                sp = sp_ref[s, p]
                spo_ref[s, 2 * p] = sp[0:DN_DK, 0:DN_DV]
                spo_ref[s, 2 * p + 1] = pltpu.roll(sp[DN_DK:PAIR], DN_DV, 1)[:, 0:DN_DV]


def _delta(h_qkv, z_dn, h_ba, cbuf8, cw8, alog_row, dtb_row, sp0, gdn_row, b, l, tl, ns, y_dtype):
    n_l = l // tl
    assert ns == 1 or n_l == 1
    n_slots = ns * (max(tl, DN_CHUNK) // DN_CHUNK)
    r = n_slots * DN_CHUNK
    consts = _delta_consts(r)
    row = lambda w: pl.BlockSpec((ns * tl, w), lambda bi, li: (bi * n_l + li, 0))
    cst = pl.BlockSpec((ns, CONV_HALO, CONV_CH), lambda bi, li: (bi, 0, 0))
    spb = pl.BlockSpec((ns, DN_HEADS, DN_DK, DN_DV), lambda bi, li: (bi, 0, 0, 0))
    wide = lambda n: pltpu.VMEM((r, n), F32)
    return pl.pallas_call(
        functools.partial(_delta_kernel, ns=ns, tl=tl, n_l=n_l),
        grid=(b // ns, n_l),
        in_specs=[row(CONV_CH), row(DN_WIDTH), row(LANES), cst, _const_spec(cw8.shape),
                  _const_spec(alog_row.shape), _const_spec(dtb_row.shape), spb, _const_spec(gdn_row.shape)]
                 + [_const_spec(c.shape) for c in consts],
        out_specs=[row(DN_WIDTH), cst, spb],
        out_shape=[jax.ShapeDtypeStruct((b * l, DN_WIDTH), y_dtype),
                   jax.ShapeDtypeStruct((b, CONV_HALO, CONV_CH), F32),
                   jax.ShapeDtypeStruct((b, DN_HEADS, DN_DK, DN_DV), F32)],
        scratch_shapes=[pltpu.VMEM((ns, CONV_HALO + tl, CONV_CH), F32), pltpu.VMEM((ns, N_PAIRS, PAIR, PAIR), F32)]
                       + [wide(DN_WIDTH)] * 7 + [wide(DN_HEADS * LANES)] + [wide(DN_WIDTH)] * 3
                       + [pltpu.VMEM((n_slots * N_PAIRS, PAIR, PAIR), BF16),
                          pltpu.VMEM((n_slots * N_PAIRS, PAIR, DN_CHUNK), BF16)],
        compiler_params=_cparams("parallel", "arbitrary"),
        name="gated_deltanet",
    )(h_qkv, z_dn, h_ba, cbuf8, cw8, alog_row, dtb_row, sp0, gdn_row, *consts)


def _merge_kernel(x_ref, ya_ref, yb_ref, yc_ref, g_ref, wa_ref, wb_ref, wc_ref, wo_ref, fg_ref, o_ref, *, final):
    merged = None
    for j, (y_ref, w_ref) in enumerate(((ya_ref, wa_ref), (yb_ref, wb_ref), (yc_ref, wc_ref))):
        br = _dot(y_ref[...].astype(BF16), w_ref[...])
        t = jax.nn.sigmoid(g_ref[:, D_MODEL * j:D_MODEL * (j + 1)]) * br
        merged = t if merged is None else merged + t
    x = x_ref[...] + _dot(merged.astype(BF16), wo_ref[...])
    o_ref[...] = _rms(x, fg_ref[...]) if final else x


def _merge(x2d, ya, yb, yc, h_gate, wa, wb, wc, wo, final_g, tm, final):
    t = x2d.shape[0]
    row = lambda w: pl.BlockSpec((tm, w), lambda i: (i, 0))
    return pl.pallas_call(
        functools.partial(_merge_kernel, final=final),
        grid=(t // tm,),
        in_specs=[row(D_MODEL), row(POOL_WIDTH), row(MLA_WIDTH), row(DN_WIDTH), row(N_BRANCH * D_MODEL)]
                 + [_const_spec(w.shape) for w in (wa, wb, wc, wo, final_g)],
        out_specs=row(D_MODEL),
        out_shape=jax.ShapeDtypeStruct((t, D_MODEL), F32),
        compiler_params=_cparams("parallel"),
        name="merge_out",
    )(x2d, ya, yb, yc, h_gate, wa, wb, wc, wo, final_g)


def _layer_weights(l, norm_g, wt_all, pool_mix, pool_scale, q_norm_g, w_uq, kv_norm_g, w_uk, w_uv,
                   conv_w, a_log, dt_bias, dn_norm_g, w_br_pool, w_br_mla, w_br_dn, w_out):
    o = IN_OFFSETS
    kr_rows = wt_all[l, o[KR_SPLIT]:o[KR_SPLIT + 1]]
    kr_blk = jnp.pad(kr_rows, ((ROPE_LANE0, HEAD_BLOCK - ROPE_LANE0 - QK_ROPE), (0, 0)))
    ba_blk = jnp.pad(wt_all[l, o[BETA_SPLIT]:o[ALPHA_SPLIT + 1]], ((0, LANES - 2 * DN_HEADS), (0, 0)))

    dq = QK_NOPE + QK_ROPE
    wuq = w_uq[l].reshape(Q_LORA, MLA_HEADS, dq)
    wuq_p = jnp.pad(wuq, ((0, 0), (0, 0), (0, HEAD_BLOCK - dq))).reshape(Q_LORA, -1).astype(BF16)
    wuk_p = jnp.pad(w_uk[l], ((0, 0), (0, 0), (0, HEAD_BLOCK - QK_NOPE))).reshape(KV_LORA, -1).astype(BF16)
    wuv_f = w_uv[l].reshape(KV_LORA, MLA_WIDTH).astype(BF16)
    wcat = jnp.zeros((MLA_HEADS, HEAD_BLOCK, KV_LORA + LANES), F32)
    wcat = wcat.at[:, 0:QK_NOPE, 0:KV_LORA].set(jnp.transpose(w_uk[l], (1, 2, 0)))
    sel = jnp.eye(QK_ROPE, dtype=F32)
    wcat = wcat.at[:, ROPE_LANE0:ROPE_LANE0 + QK_ROPE, KV_LORA:KV_LORA + QK_ROPE].set(sel)
    lanes16 = slice(DN_HEADS, 2 * DN_HEADS)
    return dict(
        norm_g=norm_g[l][None, :], kr_blk=kr_blk, ba_blk=ba_blk,
        pool_mix=pool_mix[l].astype(BF16), pool_scale=pool_scale[l][None, :],
        q_norm_g=q_norm_g[l][None, :], wuq_p=wuq_p, kv_norm_g=kv_norm_g[l][None, :],
        wuk_p=wuk_p, wuv_f=wuv_f, wcat=wcat.astype(BF16),
        cw8=jnp.pad(conv_w[l], ((0, SUBLANES - CONV_W), (0, 0))),
        alog_row=jnp.zeros((1, LANES), F32).at[0, lanes16].set(a_log[l]),
        dtb_row=jnp.zeros((1, LANES), F32).at[0, lanes16].set(dt_bias[l]),
        gdn_row=jnp.tile(dn_norm_g[l], DN_HEADS)[None, :],
        wa=w_br_pool[l].astype(BF16), wb=w_br_mla[l].astype(BF16), wc=w_br_dn[l].astype(BF16),
        wo=w_out[l].astype(BF16))


def _tile(n, target):
    t = min(n, target)
    while n % t:
        t -= SUBLANES
    return t


def _group_layer(x2d, b, l, start, w, wt_all, layer, tabs, final_g, final, pool_buf, conv_buf, sp0, paged):
    t = b * l
    small = l < 2 * SUBLANES
    y_dtype = F32 if small else BF16
    h_pool, z_pool, h_q, h_kv, h_kr, z_mla, h_qkv, z_dn, h_ba, h_gate = _inproj(
        x2d, w["norm_g"], wt_all, layer, w["kr_blk"], w["ba_blk"], _tile(t, 256))

    ya, pool_st = _pool(h_pool, z_pool, pool_buf, w["pool_mix"], w["pool_scale"], b, l, _tile(l, 512), start, y_dtype)

    tm = _tile(l, 512)
    if paged is None:
        ckv, k_r, q, k, v = _mla_prep(h_q, h_kv, h_kr, tabs, w["q_norm_g"], w["wuq_p"], w["kv_norm_g"],
                                         (w["wuk_p"], w["wuv_f"]), b, l, tm, absorbed=False)
        yb = _attn_prompt(q, k, v, z_mla, b, l, _tile(l, 512))
    else:
        page_table, cache_kv, cache_krt = paged
        ckv, k_r, qcat = _mla_prep(h_q, h_kv, h_kr, tabs, w["q_norm_g"], w["wuq_p"], w["kv_norm_g"],
                                      (w["wcat"],), b, l, tm, absorbed=True)
        wq = qcat.shape[-1] // MLA_HEADS
        qcat = qcat.reshape(b, l, MLA_HEADS, wq).transpose(0, 2, 1, 3).reshape(b, MLA_HEADS * l, wq)
        n_pages = page_table.shape[1]
        n_grp = PAGES_PER_GROUP
        while n_pages % (2 * n_grp):
            n_grp //= 2
        yb = _attn_paged(page_table, qcat, cache_kv, cache_krt, layer, ckv.reshape(b, l, KV_LORA),
                         k_r.reshape(b, l, QK_ROPE), w["wuv_f"], z_mla.reshape(b, l, MLA_WIDTH), n_grp)
        yb = yb.reshape(t, MLA_WIDTH)

    if l >= DN_CHUNK:
        tl, ns = _tile(l, 256), 1
    else:
        tl, ns = l, SAMPLE_SEQS_PER_STEP
        while b % ns:
            ns //= 2
    yc, conv_st, sp = _delta(h_qkv, z_dn, h_ba, conv_buf, w["cw8"], w["alog_row"], w["dtb_row"], sp0,
                             w["gdn_row"], b, l, tl, ns, y_dtype)

    x_out = _merge(x2d, ya, yb, yc, h_gate, w["wa"], w["wb"], w["wc"], w["wo"], final_g, _tile(t, 512), final)
    return x_out, ckv, k_r, pool_st[:, 1:], conv_st[:, CONV_HALO - (CONV_W - 1):], sp


def kernel(x_prompt, x_sample, cache_kv_latent, cache_k_rope, state_pool, state_conv, state_delta,
           page_table, norm_g, w_in, pool_mix, pool_scale, q_norm_g, w_uq, kv_norm_g, w_uk, w_uv,
           conv_w, a_log, dt_bias, dn_norm_g, w_br_pool, w_br_mla, w_br_dn, w_out, final_norm_g):
    bp, lp, d = x_prompt.shape
    db, ls, _ = x_sample.shape
    depth = w_in.shape[0]
    past_len = page_table.shape[1] * PAGE_SIZE
    final_g = final_norm_g[None, :]
    cache_krt = jnp.swapaxes(cache_k_rope, 2, 3)
    wt_all = jnp.swapaxes(w_in, 1, 2).astype(BF16)

    tabs_p = _rope_tables(lp, _tile(lp, 512), 0)
    tabs_s = _rope_tables(ls, ls, past_len)
    zero_pool = jnp.zeros((bp, POOL_HALO, POOL_WIDTH), F32)
    zero_conv = jnp.zeros((bp, CONV_HALO, CONV_CH), F32)
    zero_sp = jnp.zeros((bp, DN_HEADS, DN_DK, DN_DV), F32)

    xp = x_prompt.reshape(bp * lp, d)
    xs = x_sample.reshape(db * ls, d)
    outs_p, outs_s = [], []
    for l in range(depth):
        w = _layer_weights(l, norm_g, wt_all, pool_mix, pool_scale, q_norm_g, w_uq, kv_norm_g, w_uk, w_uv,
                           conv_w, a_log, dt_bias, dn_norm_g, w_br_pool, w_br_mla, w_br_dn, w_out)
        final = l == depth - 1
        xp, ckv, kr, pst, cst, sp = _group_layer(xp, bp, lp, 0, w, wt_all, l, tabs_p, final_g, final,
                                                 zero_pool, zero_conv, zero_sp, None)
        outs_p.append((ckv.reshape(bp, lp, KV_LORA), kr.reshape(bp, lp, QK_ROPE), pst, cst, sp))
        pool_buf = jnp.pad(state_pool[l], ((0, 0), (POOL_HALO - POOL_BUF, 0), (0, 0)))
        conv_buf = jnp.pad(state_conv[l], ((0, 0), (CONV_HALO - (CONV_W - 1), 0), (0, 0)))
        xs, ckv, kr, pst, cst, sp = _group_layer(xs, db, ls, past_len, w, wt_all, l, tabs_s, final_g, final,
                                                 pool_buf, conv_buf, state_delta[l],
                                                 (page_table, cache_kv_latent, cache_krt))
        outs_s.append((ckv.reshape(db, ls, KV_LORA), kr.reshape(db, ls, QK_ROPE), pst, cst, sp))
    stack = lambda outs, i: jnp.stack([o[i] for o in outs])
    return (xp.reshape(bp, lp, d), xs.reshape(db, ls, d),
            *(stack(outs_p, i) for i in range(5)), *(stack(outs_s, i) for i in range(5)))
```

```python
import functools

import numpy as np
import jax
import jax.numpy as jnp
from jax import lax
from jax.experimental import pallas as pl
from jax.experimental.pallas import tpu as pltpu

F32 = jnp.float32
BF16 = jnp.bfloat16

D_MODEL = 1024
EPS = 1e-6
POOL_WINDOWS = (2, 4, 8, 16)
POOL_GROUP_DIM = 128
POOL_WIDTH = 512
POOL_BUF = 15
MLA_HEADS = 8
QK_NOPE = 64
QK_ROPE = 32
V_HEAD = 64
Q_LORA = 384
KV_LORA = 256
MLA_WIDTH = 512
MLA_SCALE = (QK_NOPE + QK_ROPE) ** -0.5
ROPE_THETA = 10000.0
DN_HEADS = 8
DN_DK = 64
DN_DV = 64
DN_WIDTH = 512
CONV_W = 4
CONV_CH = 1536
DN_CHUNK = 64
PAGE_SIZE = 128
N_BRANCH = 3

LANES = 128
SUBLANES = 8
HEAD_BLOCK = LANES
ROPE_LANE0 = QK_NOPE
ROPE_HALF = QK_ROPE // 2
POOL_HALO = 16
CONV_HALO = 8
VMEM_LIMIT = 48 * 1024 * 1024
NEG = -1e30
LOG2E = 1.4426950408889634
Q_SCALE = MLA_SCALE * LOG2E
PAGES_PER_GROUP = 32
PAGED_SLOTS = 3
PAGED_SUBBLOCKS = 8
PAGED_ROW_CHUNK = 16


def _cparams(*sem):
    return pltpu.CompilerParams(dimension_semantics=sem, vmem_limit_bytes=VMEM_LIMIT)


def _const_spec(shape):
    nd = len(shape)
    return pl.BlockSpec(shape, lambda *_: (0,) * nd, pipeline_mode=pl.Buffered(1))


def _dot(a, b):
    return jnp.dot(a, b, preferred_element_type=F32)


def _dot_nt(a, b):
    return lax.dot_general(a, b, (((1,), (1,)), ((), ())), preferred_element_type=F32)


def _silu(x):
    return x * jax.nn.sigmoid(x)


def _rms(x, g):
    return x * lax.rsqrt(jnp.mean(x * x, axis=-1, keepdims=True) + EPS) * g


IN_SPLITS = (POOL_WIDTH, POOL_WIDTH, Q_LORA, KV_LORA, QK_ROPE, MLA_WIDTH, CONV_CH, DN_WIDTH,
             DN_HEADS, DN_HEADS, N_BRANCH * D_MODEL)
IN_OFFSETS = tuple(int(v) for v in np.cumsum((0,) + IN_SPLITS))
IN_WIDTH = IN_OFFSETS[-1]
KR_SPLIT, BETA_SPLIT, ALPHA_SPLIT = 4, 8, 9
IN_OUTPUTS = ((IN_OFFSETS[0], POOL_WIDTH), (IN_OFFSETS[1], POOL_WIDTH), (IN_OFFSETS[2], Q_LORA),
              (IN_OFFSETS[3], KV_LORA), (None, HEAD_BLOCK), (IN_OFFSETS[5], MLA_WIDTH),
              (IN_OFFSETS[6], CONV_CH), (IN_OFFSETS[7], DN_WIDTH), (None, LANES),
              (IN_OFFSETS[10], N_BRANCH * D_MODEL))
IN_DOT_CHUNK = 512


def _inproj_kernel(x_ref, g_ref, w_ref, kr_ref, ba_ref, *o_refs):
    xn = _rms(x_ref[...], g_ref[...]).astype(BF16)
    small = iter((kr_ref, ba_ref))
    for o_ref, (row0, width) in zip(o_refs, IN_OUTPUTS):
        if row0 is None:
            o_ref[...] = _dot_nt(xn, next(small)[...])
            continue
        for c0 in range(0, width, IN_DOT_CHUNK):
            c1 = min(c0 + IN_DOT_CHUNK, width)
            o_ref[:, c0:c1] = _dot_nt(xn, w_ref[row0 + c0:row0 + c1, :])


def _inproj(x2d, norm_g, wt_all, layer, kr_blk, ba_blk, tm):
    t = x2d.shape[0]
    row = lambda w: pl.BlockSpec((tm, w), lambda i: (i, 0))
    w_spec = pl.BlockSpec((None,) + wt_all.shape[1:], lambda i: (layer, 0, 0), pipeline_mode=pl.Buffered(1))
    return pl.pallas_call(
        _inproj_kernel,
        grid=(t // tm,),
        in_specs=[row(D_MODEL), _const_spec((1, D_MODEL)), w_spec,
                  _const_spec(kr_blk.shape), _const_spec(ba_blk.shape)],
        out_specs=[row(w) for _, w in IN_OUTPUTS],
        out_shape=[jax.ShapeDtypeStruct((t, w), F32) for _, w in IN_OUTPUTS],
        compiler_params=_cparams("parallel"),
        name="inproj",
    )(x2d, norm_g, wt_all, kr_blk, ba_blk)


def _pool_kernel(u_ref, z_ref, buf_ref, mix_ref, scale_ref, y_ref, st_ref, xx_ref, *, tl, start, n_l):
    li = pl.program_id(1)

    @pl.when(li == 0)
    def _():
        xx_ref[0:POOL_HALO, :] = buf_ref[...]

    u = u_ref[...]
    xx_ref[POOL_HALO:POOL_HALO + tl, :] = u
    row = lax.broadcasted_iota(jnp.int32, (tl, 1), 0)
    pos1 = start + li * tl + row + 1
    for gi, w in enumerate(POOL_WINDOWS):
        lanes = slice(POOL_GROUP_DIM * gi, POOL_GROUP_DIM * (gi + 1))
        s = u[:, lanes]
        for o in range(1, w):
            s = s + xx_ref[POOL_HALO - o:POOL_HALO - o + tl, lanes]
        cnt = jnp.minimum(pos1, w).astype(F32)
        d = s / cnt - u[:, lanes]
        y = _dot(d.astype(BF16), mix_ref[gi]) * scale_ref[:, lanes]
        y_ref[:, lanes] = (y * _silu(z_ref[:, lanes])).astype(y_ref.dtype)

    tail = xx_ref[tl:tl + POOL_HALO, :]

    @pl.when(li == n_l - 1)
    def _():
        st_ref[...] = tail

    xx_ref[0:POOL_HALO, :] = tail


def _pool(h_pool, z_pool, buf16, mix, scale, b, l, tl, start, y_dtype):
    n_l = l // tl
    row = lambda w: pl.BlockSpec((tl, w), lambda bi, li: (bi * n_l + li, 0))
    st = pl.BlockSpec((None, POOL_HALO, POOL_WIDTH), lambda bi, li: (bi, 0, 0))
    return pl.pallas_call(
        functools.partial(_pool_kernel, tl=tl, start=start, n_l=n_l),
        grid=(b, n_l),
        in_specs=[row(POOL_WIDTH), row(POOL_WIDTH), st, _const_spec(mix.shape), _const_spec(scale.shape)],
        out_specs=[row(POOL_WIDTH), st],
        out_shape=[jax.ShapeDtypeStruct((b * l, POOL_WIDTH), y_dtype),
                   jax.ShapeDtypeStruct((b, POOL_HALO, POOL_WIDTH), F32)],
        scratch_shapes=[pltpu.VMEM((POOL_HALO + tl, POOL_WIDTH), F32)],
        compiler_params=_cparams("parallel", "arbitrary"),
        name="pool_mixer",
    )(h_pool, z_pool, buf16, mix, scale)


def _rope_table_kernel(inv_ref, c_ref, s1_ref, s2_ref, *, tl, start):
    i = pl.program_id(0)
    shape = (tl, LANES)
    pos = (start + i * tl + lax.broadcasted_iota(jnp.int32, shape, 0)).astype(F32)
    lane = lax.broadcasted_iota(jnp.int32, shape, 1)
    ang = pos * inv_ref[...]
    cos, sin = jnp.cos(ang), jnp.sin(ang)
    first = (lane >= ROPE_LANE0) & (lane < ROPE_LANE0 + ROPE_HALF)
    second = (lane >= ROPE_LANE0 + ROPE_HALF) & (lane < ROPE_LANE0 + QK_ROPE)
    c_ref[...] = jnp.where(first | second, cos, 1.0)
    s1_ref[...] = jnp.where(first, -sin, 0.0)
    s2_ref[...] = jnp.where(second, sin, 0.0)


def _rope_tables(l, tl, start):
    half = ROPE_HALF
    inv = jnp.power(ROPE_THETA, -jnp.arange(half, dtype=F32) / half)
    inv_lane = jnp.zeros((1, LANES), F32)
    inv_lane = inv_lane.at[0, ROPE_LANE0:ROPE_LANE0 + half].set(inv)
    inv_lane = inv_lane.at[0, ROPE_LANE0 + half:ROPE_LANE0 + QK_ROPE].set(inv)
    blk = pl.BlockSpec((tl, LANES), lambda i: (i, 0))
    return pl.pallas_call(
        functools.partial(_rope_table_kernel, tl=tl, start=start),
        grid=(l // tl,),
        in_specs=[_const_spec((1, LANES))],
        out_specs=[blk, blk, blk],
        out_shape=[jax.ShapeDtypeStruct((l, LANES), F32)] * 3,
        compiler_params=_cparams("parallel"),
        name="rope_tables",
    )(inv_lane)


def _rope(x, c, s1, s2):
    return x * c + pltpu.roll(x, LANES - ROPE_HALF, 1) * s1 + pltpu.roll(x, ROPE_HALF, 1) * s2


def _mla_prep_kernel(hq_ref, hkv_ref, hkr_ref, c_ref, s1_ref, s2_ref, qg_ref, wuq_ref, kvg_ref, *rest,
                     absorbed):
    c, s1, s2 = c_ref[...], s1_ref[...], s2_ref[...]
    cq = _rms(hq_ref[...], qg_ref[...]).astype(BF16)
    q = _dot(cq, wuq_ref[...])
    ckv = _rms(hkv_ref[...], kvg_ref[...])
    kr = _rope(hkr_ref[...], c, s1, s2)
    if absorbed:
        wcat_ref, ckv_ref, kr_ref, qcat_ref = rest
    else:
        wuk_ref, wuv_ref, ckv_ref, kr_ref, q_ref, k_ref, v_ref = rest
        ckv16 = ckv.astype(BF16)
        knope = _dot(ckv16, wuk_ref[...])
        lane = lax.broadcasted_iota(jnp.int32, (1, MLA_HEADS * HEAD_BLOCK), 1)
        odd = (lane // HEAD_BLOCK) % 2
        ones_col = (lane % HEAD_BLOCK == V_HEAD * (1 - odd)).astype(F32)
        v_ref[...] = (_dot(ckv16, wuv_ref[...]) + ones_col).astype(BF16)
    ckv_ref[...] = ckv
    kr_ref[...] = pltpu.roll(kr, LANES - ROPE_LANE0, 1)[:, 0:QK_ROPE]
    for h in range(MLA_HEADS):
        blk = slice(HEAD_BLOCK * h, HEAD_BLOCK * (h + 1))
        qh = (_rope(q[:, blk], c, s1, s2) * Q_SCALE).astype(BF16)
        if absorbed:
            w = wcat_ref.shape[-1]
            qcat_ref[:, w * h:w * (h + 1)] = _dot(qh, wcat_ref[h]).astype(BF16)
        else:
            q_ref[:, blk] = qh
            k_ref[:, blk] = (knope[:, blk] + kr).astype(BF16)


def _mla_prep(h_q, h_kv, h_kr, tabs, q_norm_g, wuq_p, kv_norm_g, extra_w, b, l, tm, absorbed):
    n_l = l // tm
    t = b * l
    row = lambda w: pl.BlockSpec((tm, w), lambda bi, li: (bi * n_l + li, 0))
    tab = pl.BlockSpec((tm, LANES), lambda bi, li: (li, 0))
    hw = MLA_HEADS * HEAD_BLOCK
    in_specs = [row(Q_LORA), row(KV_LORA), row(HEAD_BLOCK), tab, tab, tab,
                _const_spec(q_norm_g.shape), _const_spec(wuq_p.shape), _const_spec(kv_norm_g.shape)]
    in_specs += [_const_spec(w.shape) for w in extra_w]
    out_specs = [row(KV_LORA), row(QK_ROPE)]
    out_shape = [jax.ShapeDtypeStruct((t, KV_LORA), F32), jax.ShapeDtypeStruct((t, QK_ROPE), F32)]
    if absorbed:
        wc = MLA_HEADS * extra_w[0].shape[-1]
        out_specs += [row(wc)]
        out_shape += [jax.ShapeDtypeStruct((t, wc), BF16)]
    else:
        out_specs += [row(hw), row(hw), row(hw)]
        out_shape += [jax.ShapeDtypeStruct((t, hw), BF16)] * 3
    return pl.pallas_call(
        functools.partial(_mla_prep_kernel, absorbed=absorbed),
        grid=(b, n_l),
        in_specs=in_specs, out_specs=out_specs, out_shape=out_shape,
        compiler_params=_cparams("parallel", "parallel"),
        name="mla_prep_absorbed" if absorbed else "mla_prep",
    )(h_q, h_kv, h_kr, *tabs, q_norm_g, wuq_p, kv_norm_g, *extra_w)


ATTN_ROW_CHUNK = 32


def _softmax_update(m_b, l_p, s, row_chunk):
    r, n = s.shape[0], s.shape[1] // LANES
    cols = [slice(LANES * c, LANES * (c + 1)) for c in range(n)]
    folded = s[:, cols[0]]
    for cb in cols[1:]:
        folded = jnp.maximum(folded, s[:, cb])
    mx = jnp.max(folded, axis=-1, keepdims=True)
    m_new = jnp.maximum(m_b, jnp.broadcast_to(mx, m_b.shape))
    a_b = jnp.exp2(m_b - m_new)
    l_rows, p_rows = [], []
    for c0 in range(0, r, row_chunk):
        rows = slice(c0, c0 + row_chunk)
        m_c = m_new[rows]
        pieces = [jnp.exp2(s[rows, cb] - m_c) for cb in cols]
        if l_p is not None:
            tot = pieces[0]
            for pc in pieces[1:]:
                tot = tot + pc
            l_rows.append(a_b[rows] * l_p[rows] + tot)
        p_rows.append(jnp.concatenate([pc.astype(BF16) for pc in pieces], axis=1))
    cat = lambda parts: parts[0] if len(parts) == 1 else jnp.concatenate(parts, axis=0)
    return m_new, (None if l_p is None else cat(l_rows)), a_b, cat(p_rows)


def _attn_kernel(q_ref, k_ref, v_ref, z_ref, y_ref, *, ta):
    qi = pl.program_id(1)
    rc = min(ATTN_ROW_CHUNK, ta)
    low_half = lax.broadcasted_iota(jnp.int32, (ta, LANES), 1) < V_HEAD
    lane_id = lax.broadcasted_iota(jnp.int32, (ta, LANES), 1)
    causal = (lax.broadcasted_iota(jnp.int32, (ta, ta), 1)
              <= lax.broadcasted_iota(jnp.int32, (ta, ta), 0))
    for p in range(MLA_HEADS // 2):
        pair = slice(LANES * p, LANES * (p + 1))
        blks = [slice(HEAD_BLOCK * h, HEAD_BLOCK * (h + 1)) for h in (2 * p, 2 * p + 1)]
        qhs = [q_ref[:, blk] for blk in blks]

        def step(kb, carry, masked, pair=pair, blks=blks, qhs=qhs):
            r0 = pl.multiple_of(kb * ta, ta)
            scores = [_dot_nt(qh, k_ref[pl.ds(r0, ta), blk]) for qh, blk in zip(qhs, blks)]
            out = []
            for (m_b, acc), s, blk in zip(carry, scores, blks):
                if masked:
                    s = jnp.where(causal, s, NEG)
                m_b, _, a_b, pm = _softmax_update(m_b, None, s, rc)
                out.append((m_b, a_b * acc + _dot(pm, v_ref[pl.ds(r0, ta), blk])))
            return tuple(out)

        init = (jnp.full((ta, LANES), NEG, F32), jnp.zeros((ta, LANES), F32))
        carry = lax.fori_loop(0, qi, functools.partial(step, masked=False), (init, init))
        (_, acc0), (_, acc1) = step(qi, carry, True)
        den = lambda acc, at: jnp.sum(jnp.where(lane_id == at, acc, 0.0), axis=-1, keepdims=True)
        o = jnp.where(low_half, acc0 / den(acc0, V_HEAD), acc1 / den(acc1, 0))
        y_ref[:, pair] = (o * _silu(z_ref[:, pair])).astype(y_ref.dtype)


def _attn_prompt(q, k, v, z_mla, b, l, ta):
    n_q = l // ta
    hw = MLA_HEADS * HEAD_BLOCK
    row = lambda w: pl.BlockSpec((ta, w), lambda bi, qi: (bi * n_q + qi, 0))
    seq = lambda w: pl.BlockSpec((l, w), lambda bi, qi: (bi, 0))
    return pl.pallas_call(
        functools.partial(_attn_kernel, ta=ta),
        grid=(b, n_q),
        in_specs=[row(hw), seq(hw), seq(hw), row(MLA_WIDTH)],
        out_specs=row(MLA_WIDTH),
        out_shape=jax.ShapeDtypeStruct((b * l, MLA_WIDTH), BF16),
        compiler_params=_cparams("parallel", "arbitrary"),
        name="mla_attention_prompt",
    )(q, k, v, z_mla)


def _attn_paged_kernel(pt_ref, qcat_ref, nkv_ref, nkr_ref, wuv_ref, z_ref, kv_hbm, krt_hbm, y_ref,
                       kvbuf, krbuf, sem, *, layer, n_pages, n_grp, l_new):
    b = pl.program_id(0)
    n_seq = pl.num_programs(0)
    n_groups = n_pages // n_grp
    rows = MLA_HEADS * l_new
    ahead = PAGED_SLOTS - 1

    sub_pages = n_grp // PAGED_SUBBLOCKS

    def group_copies(seq, j, pages):
        slot = (seq * n_groups + j) % PAGED_SLOTS
        cps = []
        for g in pages:
            page = pt_ref[seq * n_pages + j * n_grp + g]
            keys = pl.ds(PAGE_SIZE * g, PAGE_SIZE)
            cps.append(pltpu.make_async_copy(kv_hbm.at[layer, page], kvbuf.at[slot, keys, :], sem.at[0, slot]))
            cps.append(pltpu.make_async_copy(krt_hbm.at[layer, page], krbuf.at[slot, g], sem.at[1, slot]))
        return cps

    def start_pages(seq, j, pages):
        for cp in group_copies(seq, j, pages):
            cp.start()

    def wait_group(seq, j):
        for cp in group_copies(seq, j, range(n_grp)):
            cp.wait()

    def start_ahead(j, sb):
        nxt = j + ahead
        pages = range(sub_pages * sb, sub_pages * (sb + 1))
        if nxt < n_groups:
            start_pages(b, nxt, pages)
        else:
            @pl.when(b + 1 < n_seq)
            def _():
                start_pages(b + 1, nxt - n_groups, pages)

    @pl.when(b == 0)
    def _():
        for j0 in range(ahead):
            start_pages(b, j0, range(n_grp))

    qc = qcat_ref[...]
    ql, qr = qc[:, :KV_LORA], qc[:, KV_LORA:KV_LORA + QK_ROPE]

    def update(state, s, kv16):
        m_b, l_p, acc = state
        m_b, l_p, a_b, pm = _softmax_update(m_b, l_p, s, PAGED_ROW_CHUNK)
        a_wide = jnp.concatenate([a_b] * (KV_LORA // LANES), axis=1)
        return m_b, l_p, a_wide * acc + _dot(pm, kv16)

    state = (jnp.full((rows, LANES), NEG, F32), jnp.zeros((rows, LANES), F32), jnp.zeros((rows, KV_LORA), F32))
    sub_keys = n_grp * PAGE_SIZE // PAGED_SUBBLOCKS
    for j in range(n_groups):
        wait_group(b, j)
        slot = (b * n_groups + j) % PAGED_SLOTS
        subs = []
        for sb in range(PAGED_SUBBLOCKS):
            keys = slice(sub_keys * sb, sub_keys * (sb + 1))
            kv16 = kvbuf[slot, keys, :].astype(BF16)
            kr16 = jnp.concatenate([krbuf[slot, g] for g in range(sub_pages * sb, sub_pages * (sb + 1))],
                                   axis=1).astype(BF16)
            subs.append((_dot_nt(ql, kv16) + _dot(qr, kr16), kv16))
            start_ahead(j, sb)
        for s, kv16 in subs:
            state = update(state, s, kv16)

    pad = PAGE_SIZE - l_new
    nkv = jnp.concatenate([nkv_ref[...], jnp.zeros((pad, KV_LORA), F32)], axis=0).astype(BF16)
    nkr = jnp.concatenate([nkr_ref[...], jnp.zeros((pad, QK_ROPE), F32)], axis=0).astype(BF16)
    tok = lax.broadcasted_iota(jnp.int32, (rows, PAGE_SIZE), 0) % l_new
    key = lax.broadcasted_iota(jnp.int32, (rows, PAGE_SIZE), 1)
    _, l_p, acc = update(state, jnp.where(key <= tok, _dot_nt(ql, nkv) + _dot_nt(qr, nkr), NEG), nkv)
    o = (acc / jnp.sum(l_p, axis=-1, keepdims=True)).astype(BF16)
    full = _dot(o, wuv_ref[...])
    rh = lax.broadcasted_iota(jnp.int32, full.shape, 0) // l_new
    ch = lax.broadcasted_iota(jnp.int32, full.shape, 1) // V_HEAD
    full = jnp.where(rh == ch, full, 0.0)
    out = full[0:l_new]
    for h in range(1, MLA_HEADS):
        out = out + full[l_new * h:l_new * (h + 1)]
    y_ref[...] = (out * _silu(z_ref[...])).astype(y_ref.dtype)


def _attn_paged(page_table, qcat, cache_kv, cache_krt, layer, new_kv, new_kr, wuv, z_mla, n_grp):
    db, n_pages = page_table.shape
    l_new = new_kv.shape[1]
    rows = MLA_HEADS * l_new
    assert n_pages % n_grp == 0 and n_pages // n_grp >= PAGED_SLOTS - 1 and n_grp % PAGED_SUBBLOCKS == 0
    wq = qcat.shape[-1]
    per_b = lambda r, w: pl.BlockSpec((None, r, w), lambda b, pt: (b, 0, 0))
    hbm = pl.BlockSpec(memory_space=pl.ANY)
    grid_spec = pltpu.PrefetchScalarGridSpec(
        num_scalar_prefetch=1, grid=(db,),
        in_specs=[per_b(rows, wq), per_b(l_new, KV_LORA), per_b(l_new, QK_ROPE),
                  pl.BlockSpec(wuv.shape, lambda b, pt: (0, 0)), per_b(l_new, MLA_WIDTH), hbm, hbm],
        out_specs=per_b(l_new, MLA_WIDTH),
        scratch_shapes=[pltpu.VMEM((PAGED_SLOTS, n_grp * PAGE_SIZE, KV_LORA), F32),
                        pltpu.VMEM((PAGED_SLOTS, n_grp, QK_ROPE, PAGE_SIZE), F32),
                        pltpu.SemaphoreType.DMA((2, PAGED_SLOTS))])
    return pl.pallas_call(
        functools.partial(_attn_paged_kernel, layer=layer, n_pages=n_pages, n_grp=n_grp, l_new=l_new),
        grid_spec=grid_spec,
        out_shape=jax.ShapeDtypeStruct((db, l_new, MLA_WIDTH), F32),
        compiler_params=_cparams("arbitrary"),
        name="mla_attention_paged",
    )(page_table.reshape(-1), qcat, new_kv, new_kr, wuv, z_mla, cache_kv, cache_krt)


PAIR = 2 * DN_CHUNK
N_PAIRS = DN_HEADS // 2
SLOT_GROUP = 4
SAMPLE_SEQS_PER_STEP = 4


def _delta_consts(r):
    hid = np.arange(PAIR) // DN_DK
    bd = hid[:, None] == hid[None, :]
    rr = np.arange(r)
    tri = (rr[:, None] // DN_CHUNK == rr[None, :] // DN_CHUNK) & (rr[:, None] >= rr[None, :])
    return tuple(jnp.asarray(a.astype(np.float32), dtype=BF16) for a in (bd, tri))


def _split3(x):
    hi = x.astype(BF16)
    r1 = x - hi.astype(F32)
    mid = r1.astype(BF16)
    lo = (r1 - mid.astype(F32)).astype(BF16)
    return hi, mid, lo


def _sel_dot(sel, x):
    hi, mid, lo = _split3(x)
    return _dot(sel, hi) + _dot(sel, mid) + _dot(sel, lo)


def _head_sums(x2, bd):
    x16 = x2.astype(BF16)
    return jnp.concatenate([_dot(x16[:, LANES * p:LANES * (p + 1)], bd) for p in range(N_PAIRS)], axis=1)


def _delta_kernel(u_ref, z_ref, ba_ref, cbuf_ref, cw_ref, alog_ref, dtb_ref, sp0_ref, gdn_ref,
                  bd_ref, tri_ref,
                  y_ref, cst_ref, spo_ref,
                  xx_ref, sp_ref, q_s, k_s, kb_s, qe_s, vb_s, kbe_s, gcl_s, gf_s, u_s, w_s, o_s, qkd_s, kdt_s,
                  *, ns, tl, n_l):
    li = pl.program_id(1)
    cps = max(tl, DN_CHUNK) // DN_CHUNK
    neumann_steps = max(int(np.ceil(np.log2(min(tl, DN_CHUNK)))) - 1, 0)
    seq_rows = cps * DN_CHUNK
    n_slots = ns * cps

    @pl.when(li == 0)
    def _():
        xx_ref[:, 0:CONV_HALO, :] = cbuf_ref[...]
        zero = jnp.zeros((DN_DK, DN_DV), F32)
        for s in range(ns):
            for p in range(N_PAIRS):
                top = jnp.concatenate([sp0_ref[s, 2 * p], zero], axis=1)
                bot = jnp.concatenate([zero, sp0_ref[s, 2 * p + 1]], axis=1)
                sp_ref[s, p] = jnp.concatenate([top, bot], axis=0)

    def pad_rows(a):
        if tl == seq_rows:
            return a
        return jnp.concatenate([a, jnp.zeros((seq_rows - tl, a.shape[1]), a.dtype)], axis=0)

    def per_seq(a):
        if ns == 1:
            return pad_rows(a)
        return jnp.concatenate([pad_rows(a[s * tl:(s + 1) * tl]) for s in range(ns)], axis=0)

    pieces = []
    for s in range(ns):
        xx_ref[s, CONV_HALO:CONV_HALO + tl, :] = u_ref[s * tl:(s + 1) * tl, :]
        ext = xx_ref[s]
        acc = None
        for j in range(CONV_W):
            back = CONV_W - 1 - j
            shifted = ext if back == 0 else pltpu.roll(ext, back, 0)
            term = shifted[CONV_HALO:CONV_HALO + tl] * cw_ref[j:j + 1, :]
            acc = term if acc is None else acc + term
        pieces.append(pad_rows(_silu(acc)))
    qkv = pieces[0] if ns == 1 else jnp.concatenate(pieces, axis=0)
    tail = xx_ref[:, tl:tl + CONV_HALO, :]

    @pl.when(li == n_l - 1)
    def _():
        cst_ref[...] = tail

    xx_ref[:, 0:CONV_HALO, :] = tail

    bd = bd_ref[...]
    q = qkv[:, 0:DN_WIDTH]
    k = qkv[:, DN_WIDTH:2 * DN_WIDTH]
    v = qkv[:, 2 * DN_WIDTH:3 * DN_WIDTH]
    qn = q * lax.rsqrt(_head_sums(q * q, bd) + EPS) * (DN_DK ** -0.5)
    kn = k * lax.rsqrt(_head_sums(k * k, bd) + EPS)
    ba = ba_ref[...]
    lane = lax.broadcasted_iota(jnp.int32, ba.shape, 1)
    beta = jnp.where(lane < DN_HEADS, jax.nn.sigmoid(ba), 0.0)
    xa = ba + dtb_ref[...]
    softplus = jnp.maximum(xa, 0.0) + jnp.log1p(jnp.exp(-jnp.abs(xa)))
    g = jnp.where((lane >= DN_HEADS) & (lane < 2 * DN_HEADS), -jnp.exp(alog_ref[...]) * softplus, 0.0)
    beta, g = per_seq(beta), per_seq(g)
    gc = _sel_dot(tri_ref[...], g)
    n_rows = gc.shape[0]
    first_r = lax.broadcasted_iota(jnp.int32, (n_rows, LANES), 1) < DN_DK

    def head_lanes(a, lane0):
        full = [jnp.broadcast_to(a[:, lane0 + h:lane0 + h + 1], (n_rows, LANES)) for h in range(DN_HEADS)]
        pairs = [jnp.where(first_r, full[2 * p], full[2 * p + 1]) for p in range(N_PAIRS)]
        return jnp.concatenate(pairs, axis=1), full

    beta_l, _ = head_lanes(beta, 0)
    gc_l, gc_full = head_lanes(gc, DN_HEADS)
    egc = jnp.exp(gc_l)
    kb = kn * beta_l
    q_s[...] = qn
    k_s[...] = kn
    kb_s[...] = kb
    qe_s[...] = qn * egc
    vb_s[...] = v * beta_l
    kbe_s[...] = kb * egc
    gcl_s[...] = gc_l
    for h in range(DN_HEADS):
        gf_s[:, LANES * h:LANES * (h + 1)] = gc_full[h]

    ri = lax.broadcasted_iota(jnp.int32, (PAIR, PAIR), 0)
    ci = lax.broadcasted_iota(jnp.int32, (PAIR, PAIR), 1)
    same = (ri // DN_CHUNK) == (ci // DN_CHUNK)
    incl = same & (ri >= ci)
    strict = same & (ri > ci)
    eye = (ri == ci).astype(F32)
    first = lax.broadcasted_iota(jnp.int32, (DN_CHUNK, LANES), 1) < DN_DK

    def stack(a):
        return jnp.concatenate([jnp.where(first, a, 0.0), jnp.where(first, 0.0, a)], axis=0)

    def fold(a):
        return a[0:DN_CHUNK] + a[DN_CHUNK:PAIR]

    rows = lambda c: slice(DN_CHUNK * c, DN_CHUNK * (c + 1))
    lanes = lambda p: slice(LANES * p, LANES * (p + 1))

    def solve_slots(slots):
        chains = [(c, p) for c in slots for p in range(N_PAIRS)]
        kst = [stack(k_s[rows(c), lanes(p)]).astype(BF16) for c, p in chains]
        a_mat = [_dot_nt(stack(kb_s[rows(c), lanes(p)]).astype(BF16), ks) for (c, p), ks in zip(chains, kst)]
        qk = [_dot_nt(stack(q_s[rows(c), lanes(p)]).astype(BF16), ks) for (c, p), ks in zip(chains, kst)]
        dec = []
        for c, p in chains:
            g_col = jnp.concatenate([gf_s[rows(c), lanes(2 * p)], gf_s[rows(c), lanes(2 * p + 1)]], axis=0)
            diff = g_col - g_col.T
            dec.append(jnp.where(incl, jnp.exp(jnp.where(incl, diff, 0.0)), 0.0))
        pw = [jnp.where(strict, -(a * d), 0.0) for a, d in zip(a_mat, dec)]
        for (c, p), x, d in zip(chains, qk, dec):
            qkd_s[c * N_PAIRS + p] = (x * d).astype(BF16)
        t_inv = [eye + n for n in pw]
        for _ in range(neumann_steps):
            pw16 = [x.astype(BF16) for x in pw]
            pw = [_dot(x, x) for x in pw16]
            t_inv = [t + _dot(t.astype(BF16), x.astype(BF16)) for t, x in zip(t_inv, pw)]
        for (c, p), t in zip(chains, t_inv):
            rhs = jnp.concatenate([stack(vb_s[rows(c), lanes(p)]), stack(kbe_s[rows(c), lanes(p)])], axis=1)
            uw = fold(_dot(t.astype(BF16), rhs.astype(BF16)))
            u_s[rows(c), lanes(p)] = uw[:, 0:LANES]
            w_s[rows(c), lanes(p)] = uw[:, LANES:2 * LANES]
            gcl = gcl_s[rows(c), lanes(p)]
            k_dec = k_s[rows(c), lanes(p)] * jnp.exp(gcl[DN_CHUNK - 1:DN_CHUNK, :] - gcl)
            kdt_s[c * N_PAIRS + p] = k_dec.T.astype(BF16)

    for c0 in range(0, n_slots, SLOT_GROUP):
        solve_slots(range(c0, min(c0 + SLOT_GROUP, n_slots)))

    pairs = range(N_PAIRS)
    for c in range(n_slots):
        sq = c // cps
        sps = [sp_ref[sq, p] for p in pairs]
        wq = [jnp.concatenate([w_s[rows(c), lanes(p)], qe_s[rows(c), lanes(p)]], axis=0).astype(BF16)
              for p in pairs]
        res = [_dot(x, sp.astype(BF16)) for x, sp in zip(wq, sps)]
        v_new = [u_s[rows(c), lanes(p)] - res[p][0:DN_CHUNK] for p in pairs]
        intra = [fold(_dot(qkd_s[c * N_PAIRS + p], stack(v_new[p]).astype(BF16))) for p in pairs]
        upd = [_dot(kdt_s[c * N_PAIRS + p], v_new[p].astype(BF16)) for p in pairs]
        for p in pairs:
            o_s[rows(c), lanes(p)] = res[p][DN_CHUNK:PAIR] + intra[p]
            g_last = gcl_s[DN_CHUNK * (c + 1) - 1:DN_CHUNK * (c + 1), lanes(p)]
            sp_ref[sq, p] = sps[p] * jnp.exp(g_last) + jnp.where(same, upd[p], 0.0)

    o = o_s[...]
    y = o * lax.rsqrt(_head_sums(o * o, bd) * (1.0 / DN_DV) + EPS) * gdn_ref[...]
    for s in range(ns):
        zs = z_ref[s * tl:(s + 1) * tl, :]
        y_ref[s * tl:(s + 1) * tl, :] = (y[s * seq_rows:s * seq_rows + tl] * _silu(zs)).astype(y_ref.dtype)

    @pl.when(li == n_l - 1)
    def _():
        for s in range(ns):
            for p in range(N_PAIRS):
                sp = sp_ref[s, p]
                spo_ref[s, 2 * p] = sp[0:DN_DK, 0:DN_DV]
                spo_ref[s, 2 * p + 1] = pltpu.roll(sp[DN_DK:PAIR], DN_DV, 1)[:, 0:DN_DV]


def _delta(h_qkv, z_dn, h_ba, cbuf8, cw8, alog_row, dtb_row, sp0, gdn_row, b, l, tl, ns, y_dtype):
    n_l = l // tl
    assert ns == 1 or n_l == 1
    n_slots = ns * (max(tl, DN_CHUNK) // DN_CHUNK)
    r = n_slots * DN_CHUNK
    consts = _delta_consts(r)
    row = lambda w: pl.BlockSpec((ns * tl, w), lambda bi, li: (bi * n_l + li, 0))
    cst = pl.BlockSpec((ns, CONV_HALO, CONV_CH), lambda bi, li: (bi, 0, 0))
    spb = pl.BlockSpec((ns, DN_HEADS, DN_DK, DN_DV), lambda bi, li: (bi, 0, 0, 0))
    wide = lambda n: pltpu.VMEM((r, n), F32)
    return pl.pallas_call(
        functools.partial(_delta_kernel, ns=ns, tl=tl, n_l=n_l),
        grid=(b // ns, n_l),
        in_specs=[row(CONV_CH), row(DN_WIDTH), row(LANES), cst, _const_spec(cw8.shape),
                  _const_spec(alog_row.shape), _const_spec(dtb_row.shape), spb, _const_spec(gdn_row.shape)]
                 + [_const_spec(c.shape) for c in consts],
        out_specs=[row(DN_WIDTH), cst, spb],
        out_shape=[jax.ShapeDtypeStruct((b * l, DN_WIDTH), y_dtype),
                   jax.ShapeDtypeStruct((b, CONV_HALO, CONV_CH), F32),
                   jax.ShapeDtypeStruct((b, DN_HEADS, DN_DK, DN_DV), F32)],
        scratch_shapes=[pltpu.VMEM((ns, CONV_HALO + tl, CONV_CH), F32), pltpu.VMEM((ns, N_PAIRS, PAIR, PAIR), F32)]
                       + [wide(DN_WIDTH)] * 7 + [wide(DN_HEADS * LANES)] + [wide(DN_WIDTH)] * 3
                       + [pltpu.VMEM((n_slots * N_PAIRS, PAIR, PAIR), BF16),
                          pltpu.VMEM((n_slots * N_PAIRS, PAIR, DN_CHUNK), BF16)],
        compiler_params=_cparams("parallel", "arbitrary"),
        name="gated_deltanet",
    )(h_qkv, z_dn, h_ba, cbuf8, cw8, alog_row, dtb_row, sp0, gdn_row, *consts)


def _merge_kernel(x_ref, ya_ref, yb_ref, yc_ref, g_ref, wa_ref, wb_ref, wc_ref, wo_ref, fg_ref, o_ref, *, final):
    merged = None
    for j, (y_ref, w_ref) in enumerate(((ya_ref, wa_ref), (yb_ref, wb_ref), (yc_ref, wc_ref))):
        br = _dot(y_ref[...].astype(BF16), w_ref[...])
        t = jax.nn.sigmoid(g_ref[:, D_MODEL * j:D_MODEL * (j + 1)]) * br
        merged = t if merged is None else merged + t
    x = x_ref[...] + _dot(merged.astype(BF16), wo_ref[...])
    o_ref[...] = _rms(x, fg_ref[...]) if final else x


def _merge(x2d, ya, yb, yc, h_gate, wa, wb, wc, wo, final_g, tm, final):
    t = x2d.shape[0]
    row = lambda w: pl.BlockSpec((tm, w), lambda i: (i, 0))
    return pl.pallas_call(
        functools.partial(_merge_kernel, final=final),
        grid=(t // tm,),
        in_specs=[row(D_MODEL), row(POOL_WIDTH), row(MLA_WIDTH), row(DN_WIDTH), row(N_BRANCH * D_MODEL)]
                 + [_const_spec(w.shape) for w in (wa, wb, wc, wo, final_g)],
        out_specs=row(D_MODEL),
        out_shape=jax.ShapeDtypeStruct((t, D_MODEL), F32),
        compiler_params=_cparams("parallel"),
        name="merge_out",
    )(x2d, ya, yb, yc, h_gate, wa, wb, wc, wo, final_g)


def _layer_weights(l, norm_g, wt_all, pool_mix, pool_scale, q_norm_g, w_uq, kv_norm_g, w_uk, w_uv,
                   conv_w, a_log, dt_bias, dn_norm_g, w_br_pool, w_br_mla, w_br_dn, w_out):
    o = IN_OFFSETS
    kr_rows = wt_all[l, o[KR_SPLIT]:o[KR_SPLIT + 1]]
    kr_blk = jnp.pad(kr_rows, ((ROPE_LANE0, HEAD_BLOCK - ROPE_LANE0 - QK_ROPE), (0, 0)))
    ba_blk = jnp.pad(wt_all[l, o[BETA_SPLIT]:o[ALPHA_SPLIT + 1]], ((0, LANES - 2 * DN_HEADS), (0, 0)))

    dq = QK_NOPE + QK_ROPE
    wuq = w_uq[l].reshape(Q_LORA, MLA_HEADS, dq)
    wuq_p = jnp.pad(wuq, ((0, 0), (0, 0), (0, HEAD_BLOCK - dq))).reshape(Q_LORA, -1).astype(BF16)
    wuk_p = jnp.pad(w_uk[l], ((0, 0), (0, 0), (0, HEAD_BLOCK - QK_NOPE))).reshape(KV_LORA, -1).astype(BF16)
    wuv_f = w_uv[l].reshape(KV_LORA, MLA_WIDTH).astype(BF16)
    wuv_e = jnp.pad(w_uv[l][:, 0::2], ((0, 0), (0, 0), (0, HEAD_BLOCK - V_HEAD)))
    wuv_o = jnp.pad(w_uv[l][:, 1::2], ((0, 0), (0, 0), (HEAD_BLOCK - V_HEAD, 0)))
    wuv_p = jnp.stack([wuv_e, wuv_o], axis=2).reshape(KV_LORA, -1).astype(BF16)
    wcat = jnp.zeros((MLA_HEADS, HEAD_BLOCK, KV_LORA + LANES), F32)
    wcat = wcat.at[:, 0:QK_NOPE, 0:KV_LORA].set(jnp.transpose(w_uk[l], (1, 2, 0)))
    sel = jnp.eye(QK_ROPE, dtype=F32)
    wcat = wcat.at[:, ROPE_LANE0:ROPE_LANE0 + QK_ROPE, KV_LORA:KV_LORA + QK_ROPE].set(sel)
    lanes16 = slice(DN_HEADS, 2 * DN_HEADS)
    return dict(
        norm_g=norm_g[l][None, :], kr_blk=kr_blk, ba_blk=ba_blk,
        pool_mix=pool_mix[l].astype(BF16), pool_scale=pool_scale[l][None, :],
        q_norm_g=q_norm_g[l][None, :], wuq_p=wuq_p, kv_norm_g=kv_norm_g[l][None, :],
        wuk_p=wuk_p, wuv_f=wuv_f, wuv_p=wuv_p, wcat=wcat.astype(BF16),
        cw8=jnp.pad(conv_w[l], ((0, SUBLANES - CONV_W), (0, 0))),
        alog_row=jnp.zeros((1, LANES), F32).at[0, lanes16].set(a_log[l]),
        dtb_row=jnp.zeros((1, LANES), F32).at[0, lanes16].set(dt_bias[l]),
        gdn_row=jnp.tile(dn_norm_g[l], DN_HEADS)[None, :],
        wa=w_br_pool[l].astype(BF16), wb=w_br_mla[l].astype(BF16), wc=w_br_dn[l].astype(BF16),
        wo=w_out[l].astype(BF16))


def _tile(n, target):
    t = min(n, target)
    while n % t:
        t -= SUBLANES
    return t


def _group_layer(x2d, b, l, start, w, wt_all, layer, tabs, final_g, final, pool_buf, conv_buf, sp0, paged):
    t = b * l
    small = l < 2 * SUBLANES
    y_dtype = F32 if small else BF16
    h_pool, z_pool, h_q, h_kv, h_kr, z_mla, h_qkv, z_dn, h_ba, h_gate = _inproj(
        x2d, w["norm_g"], wt_all, layer, w["kr_blk"], w["ba_blk"], _tile(t, 256))

    ya, pool_st = _pool(h_pool, z_pool, pool_buf, w["pool_mix"], w["pool_scale"], b, l, _tile(l, 512), start, y_dtype)

    tm = _tile(l, 512)
    if paged is None:
        ckv, k_r, q, k, v = _mla_prep(h_q, h_kv, h_kr, tabs, w["q_norm_g"], w["wuq_p"], w["kv_norm_g"],
                                         (w["wuk_p"], w["wuv_p"]), b, l, tm, absorbed=False)
        yb = _attn_prompt(q, k, v, z_mla, b, l, _tile(l, 512))
    else:
        page_table, cache_kv, cache_krt = paged
        tabs_all = [jnp.tile(tab, (b, 1)) for tab in tabs]
        ckv, k_r, qcat = _mla_prep(h_q, h_kv, h_kr, tabs_all, w["q_norm_g"], w["wuq_p"], w["kv_norm_g"],
                                      (w["wcat"],), 1, t, t, absorbed=True)
        wq = qcat.shape[-1] // MLA_HEADS
        qcat = qcat.reshape(b, l, MLA_HEADS, wq).transpose(0, 2, 1, 3).reshape(b, MLA_HEADS * l, wq)
        n_pages = page_table.shape[1]
        n_grp = PAGES_PER_GROUP
        while n_pages % (2 * n_grp):
            n_grp //= 2
        yb = _attn_paged(page_table, qcat, cache_kv, cache_krt, layer, ckv.reshape(b, l, KV_LORA),
                         k_r.reshape(b, l, QK_ROPE), w["wuv_f"], z_mla.reshape(b, l, MLA_WIDTH), n_grp)
        yb = yb.reshape(t, MLA_WIDTH)

    if l >= DN_CHUNK:
        tl, ns = _tile(l, 256), 1
    else:
        tl, ns = l, SAMPLE_SEQS_PER_STEP
        while b % ns:
            ns //= 2
    yc, conv_st, sp = _delta(h_qkv, z_dn, h_ba, conv_buf, w["cw8"], w["alog_row"], w["dtb_row"], sp0,
                             w["gdn_row"], b, l, tl, ns, y_dtype)

    x_out = _merge(x2d, ya, yb, yc, h_gate, w["wa"], w["wb"], w["wc"], w["wo"], final_g, _tile(t, 512), final)
    return x_out, ckv, k_r, pool_st[:, 1:], conv_st[:, CONV_HALO - (CONV_W - 1):], sp


def kernel(x_prompt, x_sample, cache_kv_latent, cache_k_rope, state_pool, state_conv, state_delta,
           page_table, norm_g, w_in, pool_mix, pool_scale, q_norm_g, w_uq, kv_norm_g, w_uk, w_uv,
           conv_w, a_log, dt_bias, dn_norm_g, w_br_pool, w_br_mla, w_br_dn, w_out, final_norm_g):
    bp, lp, d = x_prompt.shape
    db, ls, _ = x_sample.shape
    depth = w_in.shape[0]
    past_len = page_table.shape[1] * PAGE_SIZE
    final_g = final_norm_g[None, :]
    cache_krt = jnp.swapaxes(cache_k_rope, 2, 3)
    wt_all = jnp.swapaxes(w_in, 1, 2).astype(BF16)

    tabs_p = _rope_tables(lp, _tile(lp, 512), 0)
    tabs_s = _rope_tables(ls, ls, past_len)
    zero_pool = jnp.zeros((bp, POOL_HALO, POOL_WIDTH), F32)
    zero_conv = jnp.zeros((bp, CONV_HALO, CONV_CH), F32)
    zero_sp = jnp.zeros((bp, DN_HEADS, DN_DK, DN_DV), F32)

    xp = x_prompt.reshape(bp * lp, d)
    xs = x_sample.reshape(db * ls, d)
    outs_p, outs_s = [], []
    for l in range(depth):
        w = _layer_weights(l, norm_g, wt_all, pool_mix, pool_scale, q_norm_g, w_uq, kv_norm_g, w_uk, w_uv,
                           conv_w, a_log, dt_bias, dn_norm_g, w_br_pool, w_br_mla, w_br_dn, w_out)
        final = l == depth - 1
        xp, ckv, kr, pst, cst, sp = _group_layer(xp, bp, lp, 0, w, wt_all, l, tabs_p, final_g, final,
                                                 zero_pool, zero_conv, zero_sp, None)
        outs_p.append((ckv.reshape(bp, lp, KV_LORA), kr.reshape(bp, lp, QK_ROPE), pst, cst, sp))
        pool_buf = jnp.pad(state_pool[l], ((0, 0), (POOL_HALO - POOL_BUF, 0), (0, 0)))
        conv_buf = jnp.pad(state_conv[l], ((0, 0), (CONV_HALO - (CONV_W - 1), 0), (0, 0)))
        xs, ckv, kr, pst, cst, sp = _group_layer(xs, db, ls, past_len, w, wt_all, l, tabs_s, final_g, final,
                                                 pool_buf, conv_buf, state_delta[l],
                                                 (page_table, cache_kv_latent, cache_krt))
        outs_s.append((ckv.reshape(db, ls, KV_LORA), kr.reshape(db, ls, QK_ROPE), pst, cst, sp))
    stack = lambda outs, i: jnp.stack([o[i] for o in outs])
    return (xp.reshape(bp, lp, d), xs.reshape(db, ls, d),
            *(stack(outs_p, i) for i in range(5)), *(stack(outs_s, i) for i in range(5)))
```

```python
import functools

import numpy as np
import jax
import jax.numpy as jnp
from jax import lax
from jax.experimental import pallas as pl
from jax.experimental.pallas import tpu as pltpu

F32 = jnp.float32
BF16 = jnp.bfloat16

D_MODEL = 1024
EPS = 1e-6
POOL_WINDOWS = (2, 4, 8, 16)
POOL_GROUP_DIM = 128
POOL_WIDTH = 512
POOL_BUF = 15
MLA_HEADS = 8
QK_NOPE = 64
QK_ROPE = 32
V_HEAD = 64
Q_LORA = 384
KV_LORA = 256
MLA_WIDTH = 512
MLA_SCALE = (QK_NOPE + QK_ROPE) ** -0.5
ROPE_THETA = 10000.0
DN_HEADS = 8
DN_DK = 64
DN_DV = 64
DN_WIDTH = 512
CONV_W = 4
CONV_CH = 1536
DN_CHUNK = 64
PAGE_SIZE = 128
N_BRANCH = 3

LANES = 128
SUBLANES = 8
HEAD_BLOCK = LANES
ROPE_LANE0 = QK_NOPE
ROPE_HALF = QK_ROPE // 2
POOL_HALO = 16
CONV_HALO = 8
VMEM_LIMIT = 48 * 1024 * 1024
NEG = -1e30
LOG2E = 1.4426950408889634
Q_SCALE = MLA_SCALE * LOG2E
PAGES_PER_GROUP = 64
PAGED_SLOTS = 3
PAGED_SUBBLOCKS = 16
PAGED_ROW_CHUNK = 16


def _cparams(*sem):
    return pltpu.CompilerParams(dimension_semantics=sem, vmem_limit_bytes=VMEM_LIMIT)


def _const_spec(shape):
    nd = len(shape)
    return pl.BlockSpec(shape, lambda *_: (0,) * nd, pipeline_mode=pl.Buffered(1))


def _dot(a, b):
    return jnp.dot(a, b, preferred_element_type=F32)


def _dot_nt(a, b):
    return lax.dot_general(a, b, (((1,), (1,)), ((), ())), preferred_element_type=F32)


def _silu(x):
    return x * jax.nn.sigmoid(x)


def _rms(x, g):
    return x * lax.rsqrt(jnp.mean(x * x, axis=-1, keepdims=True) + EPS) * g


IN_SPLITS = (POOL_WIDTH, POOL_WIDTH, Q_LORA, KV_LORA, QK_ROPE, MLA_WIDTH, CONV_CH, DN_WIDTH,
             DN_HEADS, DN_HEADS, N_BRANCH * D_MODEL)
IN_OFFSETS = tuple(int(v) for v in np.cumsum((0,) + IN_SPLITS))
IN_WIDTH = IN_OFFSETS[-1]
KR_SPLIT, BETA_SPLIT, ALPHA_SPLIT = 4, 8, 9
IN_OUTPUTS = ((IN_OFFSETS[0], POOL_WIDTH), (IN_OFFSETS[1], POOL_WIDTH), (IN_OFFSETS[2], Q_LORA),
              (IN_OFFSETS[3], KV_LORA), (None, LANES), (IN_OFFSETS[5], MLA_WIDTH),
              (IN_OFFSETS[6], CONV_CH), (IN_OFFSETS[7], DN_WIDTH),
              (IN_OFFSETS[10], N_BRANCH * D_MODEL))
IN_DOT_CHUNK = 512


def _inproj_kernel(x_ref, g_ref, w_ref, small_ref, *o_refs):
    xn = _rms(x_ref[...], g_ref[...]).astype(BF16)
    for o_ref, (row0, width) in zip(o_refs, IN_OUTPUTS):
        if row0 is None:
            o_ref[...] = _dot_nt(xn, small_ref[...])
            continue
        for c0 in range(0, width, IN_DOT_CHUNK):
            c1 = min(c0 + IN_DOT_CHUNK, width)
            o_ref[:, c0:c1] = _dot_nt(xn, w_ref[row0 + c0:row0 + c1, :])


def _inproj(x2d, norm_g, wt_all, layer, small_blk, tm):
    t = x2d.shape[0]
    row = lambda w: pl.BlockSpec((tm, w), lambda i: (i, 0))
    w_spec = pl.BlockSpec((None,) + wt_all.shape[1:], lambda i: (layer, 0, 0), pipeline_mode=pl.Buffered(1))
    return pl.pallas_call(
        _inproj_kernel,
        grid=(t // tm,),
        in_specs=[row(D_MODEL), _const_spec((1, D_MODEL)), w_spec, _const_spec(small_blk.shape)],
        out_specs=[row(w) for _, w in IN_OUTPUTS],
        out_shape=[jax.ShapeDtypeStruct((t, w), F32) for _, w in IN_OUTPUTS],
        compiler_params=_cparams("parallel"),
        name="inproj",
    )(x2d, norm_g, wt_all, small_blk)


def _pool_kernel(u_ref, z_ref, buf_ref, mix_ref, scale_ref, y_ref, st_ref, xx_ref, *, tl, start, n_l):
    li = pl.program_id(1)

    @pl.when(li == 0)
    def _():
        xx_ref[0:POOL_HALO, :] = buf_ref[...]

    u = u_ref[...]
    xx_ref[POOL_HALO:POOL_HALO + tl, :] = u
    row = lax.broadcasted_iota(jnp.int32, (tl, 1), 0)
    pos1 = start + li * tl + row + 1
    for gi, w in enumerate(POOL_WINDOWS):
        lanes = slice(POOL_GROUP_DIM * gi, POOL_GROUP_DIM * (gi + 1))
        s = u[:, lanes]
        for o in range(1, w):
            s = s + xx_ref[POOL_HALO - o:POOL_HALO - o + tl, lanes]
        cnt = jnp.minimum(pos1, w).astype(F32)
        d = s / cnt - u[:, lanes]
        y = _dot(d.astype(BF16), mix_ref[gi]) * scale_ref[:, lanes]
        y_ref[:, lanes] = (y * _silu(z_ref[:, lanes])).astype(y_ref.dtype)

    tail = xx_ref[tl:tl + POOL_HALO, :]

    @pl.when(li == n_l - 1)
    def _():
        st_ref[...] = tail

    xx_ref[0:POOL_HALO, :] = tail


def _pool(h_pool, z_pool, buf16, mix, scale, b, l, tl, start, y_dtype):
    n_l = l // tl
    row = lambda w: pl.BlockSpec((tl, w), lambda bi, li: (bi * n_l + li, 0))
    st = pl.BlockSpec((None, POOL_HALO, POOL_WIDTH), lambda bi, li: (bi, 0, 0))
    return pl.pallas_call(
        functools.partial(_pool_kernel, tl=tl, start=start, n_l=n_l),
        grid=(b, n_l),
        in_specs=[row(POOL_WIDTH), row(POOL_WIDTH), st, _const_spec(mix.shape), _const_spec(scale.shape)],
        out_specs=[row(POOL_WIDTH), st],
        out_shape=[jax.ShapeDtypeStruct((b * l, POOL_WIDTH), y_dtype),
                   jax.ShapeDtypeStruct((b, POOL_HALO, POOL_WIDTH), F32)],
        scratch_shapes=[pltpu.VMEM((POOL_HALO + tl, POOL_WIDTH), F32)],
        compiler_params=_cparams("parallel", "arbitrary"),
        name="pool_mixer",
    )(h_pool, z_pool, buf16, mix, scale)


def _rope_table_kernel(inv_ref, c_ref, s1_ref, s2_ref, *, tl, start):
    i = pl.program_id(0)
    shape = (tl, LANES)
    pos = (start + i * tl + lax.broadcasted_iota(jnp.int32, shape, 0)).astype(F32)
    lane = lax.broadcasted_iota(jnp.int32, shape, 1)
    ang = pos * inv_ref[...]
    cos, sin = jnp.cos(ang), jnp.sin(ang)
    first = (lane >= ROPE_LANE0) & (lane < ROPE_LANE0 + ROPE_HALF)
    second = (lane >= ROPE_LANE0 + ROPE_HALF) & (lane < ROPE_LANE0 + QK_ROPE)
    c_ref[...] = jnp.where(first | second, cos, 1.0)
    s1_ref[...] = jnp.where(first, -sin, 0.0)
    s2_ref[...] = jnp.where(second, sin, 0.0)


def _rope_tables(l, tl, start):
    half = ROPE_HALF
    inv = jnp.power(ROPE_THETA, -jnp.arange(half, dtype=F32) / half)
    inv_lane = jnp.zeros((1, LANES), F32)
    inv_lane = inv_lane.at[0, ROPE_LANE0:ROPE_LANE0 + half].set(inv)
    inv_lane = inv_lane.at[0, ROPE_LANE0 + half:ROPE_LANE0 + QK_ROPE].set(inv)
    blk = pl.BlockSpec((tl, LANES), lambda i: (i, 0))
    return pl.pallas_call(
        functools.partial(_rope_table_kernel, tl=tl, start=start),
        grid=(l // tl,),
        in_specs=[_const_spec((1, LANES))],
        out_specs=[blk, blk, blk],
        out_shape=[jax.ShapeDtypeStruct((l, LANES), F32)] * 3,
        compiler_params=_cparams("parallel"),
        name="rope_tables",
    )(inv_lane)


def _rope(x, c, s1, s2):
    return x * c + pltpu.roll(x, LANES - ROPE_HALF, 1) * s1 + pltpu.roll(x, ROPE_HALF, 1) * s2


def _mla_prep_kernel(hq_ref, hkv_ref, hkr_ref, c_ref, s1_ref, s2_ref, qg_ref, wuq_ref, kvg_ref, *rest,
                     absorbed, n_prev):
    c, s1, s2 = c_ref[...], s1_ref[...], s2_ref[...]
    cq = _rms(hq_ref[...], qg_ref[...]).astype(BF16)
    q = _dot(cq, wuq_ref[...])
    ckv = _rms(hkv_ref[...], kvg_ref[...])
    lane128 = lax.broadcasted_iota(jnp.int32, c.shape, 1)
    on_rope = (lane128 >= ROPE_LANE0) & (lane128 < ROPE_LANE0 + QK_ROPE)
    kr = jnp.where(on_rope, _rope(hkr_ref[...], c, s1, s2), 0.0)
    if n_prev:
        pckv_ref, pkr_ref, *rest = rest
    if absorbed:
        wcat_ref, ckv_ref, kr_ref, qcat_ref = rest
    else:
        wuk_ref, wuv_ref, ckv_ref, kr_ref, q_ref, k_ref, v_ref = rest
        ckv16 = ckv.astype(BF16)
        knope = _dot(ckv16, wuk_ref[...])
        lane = lax.broadcasted_iota(jnp.int32, (1, MLA_HEADS * HEAD_BLOCK), 1)
        odd = (lane // HEAD_BLOCK) % 2
        ones_col = (lane % HEAD_BLOCK == V_HEAD * (1 - odd)).astype(F32)
        v_ref[...] = (_dot(ckv16, wuv_ref[...]) + ones_col).astype(BF16)
    ckv_ref[n_prev] = ckv
    kr_ref[n_prev] = pltpu.roll(kr, LANES - ROPE_LANE0, 1)[:, 0:QK_ROPE]
    if n_prev:
        ckv_ref[0:n_prev] = pckv_ref[...]
        kr_ref[0:n_prev] = pkr_ref[...]
    for h in range(MLA_HEADS):
        blk = slice(HEAD_BLOCK * h, HEAD_BLOCK * (h + 1))
        qh = (_rope(q[:, blk], c, s1, s2) * Q_SCALE).astype(BF16)
        if absorbed:
            w = wcat_ref.shape[-1]
            qcat_ref[:, w * h:w * (h + 1)] = _dot(qh, wcat_ref[h]).astype(BF16)
        else:
            q_ref[:, blk] = qh
            k_ref[:, blk] = (knope[:, blk] + kr).astype(BF16)


def _mla_prep(h_q, h_kv, h_kr, tabs, q_norm_g, wuq_p, kv_norm_g, extra_w, b, l, tm, absorbed, prev=()):
    n_l = l // tm
    t = b * l
    n_prev = prev[0].shape[0] if prev else 0
    row = lambda w: pl.BlockSpec((tm, w), lambda bi, li: (bi * n_l + li, 0))
    stk = lambda n, w: pl.BlockSpec((n, tm, w), lambda bi, li: (0, bi * n_l + li, 0))
    tab = pl.BlockSpec((tm, LANES), lambda bi, li: (li, 0))
    hw = MLA_HEADS * HEAD_BLOCK
    in_specs = [row(Q_LORA), row(KV_LORA), row(HEAD_BLOCK), tab, tab, tab,
                _const_spec(q_norm_g.shape), _const_spec(wuq_p.shape), _const_spec(kv_norm_g.shape)]
    if n_prev:
        in_specs += [stk(n_prev, KV_LORA), stk(n_prev, QK_ROPE)]
    in_specs += [_const_spec(w.shape) for w in extra_w]
    out_specs = [stk(n_prev + 1, KV_LORA), stk(n_prev + 1, QK_ROPE)]
    out_shape = [jax.ShapeDtypeStruct((n_prev + 1, t, KV_LORA), F32),
                 jax.ShapeDtypeStruct((n_prev + 1, t, QK_ROPE), F32)]
    if absorbed:
        wc = MLA_HEADS * extra_w[0].shape[-1]
        out_specs += [row(wc)]
        out_shape += [jax.ShapeDtypeStruct((t, wc), BF16)]
    else:
        out_specs += [row(hw), row(hw), row(hw)]
        out_shape += [jax.ShapeDtypeStruct((t, hw), BF16)] * 3
    return pl.pallas_call(
        functools.partial(_mla_prep_kernel, absorbed=absorbed, n_prev=n_prev),
        grid=(b, n_l),
        in_specs=in_specs, out_specs=out_specs, out_shape=out_shape,
        compiler_params=_cparams("parallel", "parallel"),
        name="mla_prep_absorbed" if absorbed else "mla_prep",
    )(h_q, h_kv, h_kr, *tabs, q_norm_g, wuq_p, kv_norm_g, *prev, *extra_w)


ATTN_ROW_CHUNK = 32


def _softmax_update(m_b, l_p, s, row_chunk):
    r, n = s.shape[0], s.shape[1] // LANES
    cols = [slice(LANES * c, LANES * (c + 1)) for c in range(n)]
    folded = s[:, cols[0]]
    for cb in cols[1:]:
        folded = jnp.maximum(folded, s[:, cb])
    mx = jnp.max(folded, axis=-1, keepdims=True)
    m_new = jnp.maximum(m_b, jnp.broadcast_to(mx, m_b.shape))
    a_b = jnp.exp2(m_b - m_new)
    l_rows, p_rows = [], []
    for c0 in range(0, r, row_chunk):
        rows = slice(c0, c0 + row_chunk)
        m_c = m_new[rows]
        pieces = [jnp.exp2(s[rows, cb] - m_c) for cb in cols]
        if l_p is not None:
            tot = pieces[0]
            for pc in pieces[1:]:
                tot = tot + pc
            l_rows.append(a_b[rows] * l_p[rows] + tot)
        p_rows.append(jnp.concatenate([pc.astype(BF16) for pc in pieces], axis=1))
    cat = lambda parts: parts[0] if len(parts) == 1 else jnp.concatenate(parts, axis=0)
    return m_new, (None if l_p is None else cat(l_rows)), a_b, cat(p_rows)


def _attn_kernel(q_ref, k_ref, v_ref, z_ref, y_ref, *, ta):
    qi = pl.program_id(1)
    rc = min(ATTN_ROW_CHUNK, ta)
    low_half = lax.broadcasted_iota(jnp.int32, (ta, LANES), 1) < V_HEAD
    lane_id = lax.broadcasted_iota(jnp.int32, (ta, LANES), 1)
    causal = (lax.broadcasted_iota(jnp.int32, (ta, ta), 1)
              <= lax.broadcasted_iota(jnp.int32, (ta, ta), 0))
    for p in range(MLA_HEADS // 2):
        pair = slice(LANES * p, LANES * (p + 1))
        blks = [slice(HEAD_BLOCK * h, HEAD_BLOCK * (h + 1)) for h in (2 * p, 2 * p + 1)]
        qhs = [q_ref[:, blk] for blk in blks]

        def step(kb, carry, masked, pair=pair, blks=blks, qhs=qhs):
            r0 = pl.multiple_of(kb * ta, ta)
            scores = [_dot_nt(qh, k_ref[pl.ds(r0, ta), blk]) for qh, blk in zip(qhs, blks)]
            out = []
            for (m_b, acc), s, blk in zip(carry, scores, blks):
                if masked:
                    s = jnp.where(causal, s, NEG)
                m_b, _, a_b, pm = _softmax_update(m_b, None, s, rc)
                out.append((m_b, a_b * acc + _dot(pm, v_ref[pl.ds(r0, ta), blk])))
            return tuple(out)

        init = (jnp.full((ta, LANES), NEG, F32), jnp.zeros((ta, LANES), F32))
        carry = lax.fori_loop(0, qi, functools.partial(step, masked=False), (init, init))
        (_, acc0), (_, acc1) = step(qi, carry, True)
        den = lambda acc, at: jnp.sum(jnp.where(lane_id == at, acc, 0.0), axis=-1, keepdims=True)
        o = jnp.where(low_half, acc0 / den(acc0, V_HEAD), acc1 / den(acc1, 0))
        y_ref[:, pair] = (o * _silu(z_ref[:, pair])).astype(y_ref.dtype)


def _attn_prompt(q, k, v, z_mla, b, l, ta):
    n_q = l // ta
    hw = MLA_HEADS * HEAD_BLOCK
    row = lambda w: pl.BlockSpec((ta, w), lambda bi, qi: (bi * n_q + qi, 0))
    seq = lambda w: pl.BlockSpec((l, w), lambda bi, qi: (bi, 0))
    return pl.pallas_call(
        functools.partial(_attn_kernel, ta=ta),
        grid=(b, n_q),
        in_specs=[row(hw), seq(hw), seq(hw), row(MLA_WIDTH)],
        out_specs=row(MLA_WIDTH),
        out_shape=jax.ShapeDtypeStruct((b * l, MLA_WIDTH), BF16),
        compiler_params=_cparams("parallel", "arbitrary"),
        name="mla_attention_prompt",
    )(q, k, v, z_mla)


def _attn_paged_kernel(pt_ref, qcat_ref, nkv_ref, nkr_ref, wuv_ref, z_ref, kv_hbm, krt_hbm, y_ref,
                       kvbuf, krbuf, sem, *, layer, n_pages, n_grp, l_new):
    b = pl.program_id(0)
    n_seq = pl.num_programs(0)
    n_groups = n_pages // n_grp
    rows = MLA_HEADS * l_new
    ahead = PAGED_SLOTS - 1

    sub_pages = n_grp // PAGED_SUBBLOCKS

    def group_copies(seq, j, pages):
        slot = (seq * n_groups + j) % PAGED_SLOTS
        cps = []
        for g in pages:
            page = pt_ref[seq * n_pages + j * n_grp + g]
            keys = pl.ds(PAGE_SIZE * g, PAGE_SIZE)
            cps.append(pltpu.make_async_copy(kv_hbm.at[layer, page], kvbuf.at[slot, keys, :], sem.at[0, slot]))
            cps.append(pltpu.make_async_copy(krt_hbm.at[layer, page], krbuf.at[slot, g], sem.at[1, slot]))
        return cps

    def start_pages(seq, j, pages):
        for cp in group_copies(seq, j, pages):
            cp.start()

    def wait_group(seq, j):
        for cp in group_copies(seq, j, range(n_grp)):
            cp.wait()

    def start_ahead(j, sb):
        nxt = j + ahead
        pages = range(sub_pages * sb, sub_pages * (sb + 1))
        if nxt < n_groups:
            start_pages(b, nxt, pages)
        else:
            @pl.when(b + 1 < n_seq)
            def _():
                start_pages(b + 1, nxt - n_groups, pages)

    @pl.when(b == 0)
    def _():
        for j0 in range(ahead):
            start_pages(b, j0, range(n_grp))

    qc = qcat_ref[...]
    ql, qr = qc[:, :KV_LORA], qc[:, KV_LORA:KV_LORA + QK_ROPE]

    def update(state, s, kv16):
        m_b, l_p, acc = state
        m_b, l_p, a_b, pm = _softmax_update(m_b, l_p, s, PAGED_ROW_CHUNK)
        a_wide = jnp.concatenate([a_b] * (KV_LORA // LANES), axis=1)
        return m_b, l_p, a_wide * acc + _dot(pm, kv16)

    state = (jnp.full((rows, LANES), NEG, F32), jnp.zeros((rows, LANES), F32), jnp.zeros((rows, KV_LORA), F32))
    sub_keys = n_grp * PAGE_SIZE // PAGED_SUBBLOCKS
    for j in range(n_groups):
        wait_group(b, j)
        slot = (b * n_groups + j) % PAGED_SLOTS
        subs = []
        for sb in range(PAGED_SUBBLOCKS):
            keys = slice(sub_keys * sb, sub_keys * (sb + 1))
            kv16 = kvbuf[slot, keys, :].astype(BF16)
            kr16 = jnp.concatenate([krbuf[slot, g] for g in range(sub_pages * sb, sub_pages * (sb + 1))],
                                   axis=1).astype(BF16)
            subs.append((_dot_nt(ql, kv16) + _dot(qr, kr16), kv16))
            start_ahead(j, sb)
        for s, kv16 in subs:
            state = update(state, s, kv16)

    pad = PAGE_SIZE - l_new
    nkv = jnp.concatenate([nkv_ref[...], jnp.zeros((pad, KV_LORA), F32)], axis=0).astype(BF16)
    nkr = jnp.concatenate([nkr_ref[...], jnp.zeros((pad, QK_ROPE), F32)], axis=0).astype(BF16)
    tok = lax.broadcasted_iota(jnp.int32, (rows, PAGE_SIZE), 0) % l_new
    key = lax.broadcasted_iota(jnp.int32, (rows, PAGE_SIZE), 1)
    _, l_p, acc = update(state, jnp.where(key <= tok, _dot_nt(ql, nkv) + _dot_nt(qr, nkr), NEG), nkv)
    o = (acc / jnp.sum(l_p, axis=-1, keepdims=True)).astype(BF16)
    full = _dot(o, wuv_ref[...])
    rh = lax.broadcasted_iota(jnp.int32, full.shape, 0) // l_new
    ch = lax.broadcasted_iota(jnp.int32, full.shape, 1) // V_HEAD
    full = jnp.where(rh == ch, full, 0.0)
    out = full[0:l_new]
    for h in range(1, MLA_HEADS):
        out = out + full[l_new * h:l_new * (h + 1)]
    y_ref[...] = (out * _silu(z_ref[...])).astype(y_ref.dtype)


def _attn_paged(page_table, qcat, cache_kv, cache_krt, layer, new_kv, new_kr, wuv, z_mla, n_grp):
    db, n_pages = page_table.shape
    l_new = new_kv.shape[1]
    rows = MLA_HEADS * l_new
    assert n_pages % n_grp == 0 and n_pages // n_grp >= PAGED_SLOTS - 1 and n_grp % PAGED_SUBBLOCKS == 0
    wq = qcat.shape[-1]
    per_b = lambda r, w: pl.BlockSpec((None, r, w), lambda b, pt: (b, 0, 0))
    hbm = pl.BlockSpec(memory_space=pl.ANY)
    grid_spec = pltpu.PrefetchScalarGridSpec(
        num_scalar_prefetch=1, grid=(db,),
        in_specs=[per_b(rows, wq), per_b(l_new, KV_LORA), per_b(l_new, QK_ROPE),
                  pl.BlockSpec(wuv.shape, lambda b, pt: (0, 0)), per_b(l_new, MLA_WIDTH), hbm, hbm],
        out_specs=per_b(l_new, MLA_WIDTH),
        scratch_shapes=[pltpu.VMEM((PAGED_SLOTS, n_grp * PAGE_SIZE, KV_LORA), F32),
                        pltpu.VMEM((PAGED_SLOTS, n_grp, QK_ROPE, PAGE_SIZE), F32),
                        pltpu.SemaphoreType.DMA((2, PAGED_SLOTS))])
    return pl.pallas_call(
        functools.partial(_attn_paged_kernel, layer=layer, n_pages=n_pages, n_grp=n_grp, l_new=l_new),
        grid_spec=grid_spec,
        out_shape=jax.ShapeDtypeStruct((db, l_new, MLA_WIDTH), F32),
        compiler_params=_cparams("arbitrary"),
        name="mla_attention_paged",
    )(page_table.reshape(-1), qcat, new_kv, new_kr, wuv, z_mla, cache_kv, cache_krt)


PAIR = 2 * DN_CHUNK
N_PAIRS = DN_HEADS // 2
SLOT_GROUP = 4
SAMPLE_SEQS_PER_STEP = 4


def _delta_consts(r):
    hid = np.arange(PAIR) // DN_DK
    bd = hid[:, None] == hid[None, :]
    rr = np.arange(r)
    tri = (rr[:, None] // DN_CHUNK == rr[None, :] // DN_CHUNK) & (rr[:, None] >= rr[None, :])
    return tuple(jnp.asarray(a.astype(np.float32), dtype=BF16) for a in (bd, tri))


def _split3(x):
    hi = x.astype(BF16)
    r1 = x - hi.astype(F32)
    mid = r1.astype(BF16)
    lo = (r1 - mid.astype(F32)).astype(BF16)
    return hi, mid, lo


def _sel_dot(sel, x):
    hi, mid, lo = _split3(x)
    return _dot(sel, hi) + _dot(sel, mid) + _dot(sel, lo)


def _head_sums(x2, bd):
    x16 = x2.astype(BF16)
    return jnp.concatenate([_dot(x16[:, LANES * p:LANES * (p + 1)], bd) for p in range(N_PAIRS)], axis=1)


def _delta_kernel(u_ref, z_ref, ba_ref, cbuf_ref, cw_ref, alog_ref, dtb_ref, sp0_ref, gdn_ref,
                  bd_ref, tri_ref,
                  y_ref, cst_ref, spo_ref,
                  xx_ref, sp_ref, q_s, k_s, kb_s, qe_s, vb_s, kbe_s, gcl_s, gf_s, u_s, w_s, o_s, qkd_s, kdt_s,
                  *, ns, tl, n_l):
    li = pl.program_id(1)
    cps = max(tl, DN_CHUNK) // DN_CHUNK
    neumann_steps = max(int(np.ceil(np.log2(min(tl, DN_CHUNK)))) - 1, 0)
    seq_rows = cps * DN_CHUNK
    n_slots = ns * cps

    @pl.when(li == 0)
    def _():
        xx_ref[:, 0:CONV_HALO, :] = cbuf_ref[...]
        zero = jnp.zeros((DN_DK, DN_DV), F32)
        for s in range(ns):
            for p in range(N_PAIRS):
                top = jnp.concatenate([sp0_ref[s, 2 * p], zero], axis=1)
                bot = jnp.concatenate([zero, sp0_ref[s, 2 * p + 1]], axis=1)
                sp_ref[s, p] = jnp.concatenate([top, bot], axis=0)

    def pad_rows(a):
        if tl == seq_rows:
            return a
        return jnp.concatenate([a, jnp.zeros((seq_rows - tl, a.shape[1]), a.dtype)], axis=0)

    def per_seq(a):
        if ns == 1:
            return pad_rows(a)
        return jnp.concatenate([pad_rows(a[s * tl:(s + 1) * tl]) for s in range(ns)], axis=0)

    pieces = []
    for s in range(ns):
        xx_ref[s, CONV_HALO:CONV_HALO + tl, :] = u_ref[s * tl:(s + 1) * tl, :]
        ext = xx_ref[s]
        acc = None
        for j in range(CONV_W):
            back = CONV_W - 1 - j
            shifted = ext if back == 0 else pltpu.roll(ext, back, 0)
            term = shifted[CONV_HALO:CONV_HALO + tl] * cw_ref[j:j + 1, :]
            acc = term if acc is None else acc + term
        pieces.append(pad_rows(_silu(acc)))
    qkv = pieces[0] if ns == 1 else jnp.concatenate(pieces, axis=0)
    tail = xx_ref[:, tl:tl + CONV_HALO, :]

    @pl.when(li == n_l - 1)
    def _():
        cst_ref[...] = tail

    xx_ref[:, 0:CONV_HALO, :] = tail

    bd = bd_ref[...]
    q = qkv[:, 0:DN_WIDTH]
    k = qkv[:, DN_WIDTH:2 * DN_WIDTH]
    v = qkv[:, 2 * DN_WIDTH:3 * DN_WIDTH]
    qn = q * lax.rsqrt(_head_sums(q * q, bd) + EPS) * (DN_DK ** -0.5)
    kn = k * lax.rsqrt(_head_sums(k * k, bd) + EPS)
    ba = ba_ref[...]
    lane = lax.broadcasted_iota(jnp.int32, ba.shape, 1)
    beta = jnp.where(lane < DN_HEADS, jax.nn.sigmoid(ba), 0.0)
    xa = ba + dtb_ref[...]
    softplus = jnp.maximum(xa, 0.0) + jnp.log1p(jnp.exp(-jnp.abs(xa)))
    g = jnp.where((lane >= DN_HEADS) & (lane < 2 * DN_HEADS), -jnp.exp(alog_ref[...]) * softplus, 0.0)
    beta, g = per_seq(beta), per_seq(g)
    gc = _sel_dot(tri_ref[...], g)
    n_rows = gc.shape[0]
    first_r = lax.broadcasted_iota(jnp.int32, (n_rows, LANES), 1) < DN_DK

    def head_lanes(a, lane0):
        full = [jnp.broadcast_to(a[:, lane0 + h:lane0 + h + 1], (n_rows, LANES)) for h in range(DN_HEADS)]
        pairs = [jnp.where(first_r, full[2 * p], full[2 * p + 1]) for p in range(N_PAIRS)]
        return jnp.concatenate(pairs, axis=1), full

    beta_l, _ = head_lanes(beta, 0)
    gc_l, gc_full = head_lanes(gc, DN_HEADS)
    egc = jnp.exp(gc_l)
    kb = kn * beta_l
    q_s[...] = qn
    k_s[...] = kn
    kb_s[...] = kb
    qe_s[...] = qn * egc
    vb_s[...] = v * beta_l
    kbe_s[...] = kb * egc
    gcl_s[...] = gc_l
    for h in range(DN_HEADS):
        gf_s[:, LANES * h:LANES * (h + 1)] = gc_full[h]

    ri = lax.broadcasted_iota(jnp.int32, (PAIR, PAIR), 0)
    ci = lax.broadcasted_iota(jnp.int32, (PAIR, PAIR), 1)
    same = (ri // DN_CHUNK) == (ci // DN_CHUNK)
    incl = same & (ri >= ci)
    strict = same & (ri > ci)
    eye = (ri == ci).astype(F32)
    first = lax.broadcasted_iota(jnp.int32, (DN_CHUNK, LANES), 1) < DN_DK

    def stack(a):
        return jnp.concatenate([jnp.where(first, a, 0.0), jnp.where(first, 0.0, a)], axis=0)

    def fold(a):
        return a[0:DN_CHUNK] + a[DN_CHUNK:PAIR]

    rows = lambda c: slice(DN_CHUNK * c, DN_CHUNK * (c + 1))
    lanes = lambda p: slice(LANES * p, LANES * (p + 1))

    def solve_slots(slots):
        chains = [(c, p) for c in slots for p in range(N_PAIRS)]
        kst = [stack(k_s[rows(c), lanes(p)]).astype(BF16) for c, p in chains]
        a_mat = [_dot_nt(stack(kb_s[rows(c), lanes(p)]).astype(BF16), ks) for (c, p), ks in zip(chains, kst)]
        qk = [_dot_nt(stack(q_s[rows(c), lanes(p)]).astype(BF16), ks) for (c, p), ks in zip(chains, kst)]
        dec = []
        for c, p in chains:
            g_col = jnp.concatenate([gf_s[rows(c), lanes(2 * p)], gf_s[rows(c), lanes(2 * p + 1)]], axis=0)
            diff = g_col - g_col.T
            dec.append(jnp.where(incl, jnp.exp(jnp.where(incl, diff, 0.0)), 0.0))
        pw = [jnp.where(strict, -(a * d), 0.0) for a, d in zip(a_mat, dec)]
        for (c, p), x, d in zip(chains, qk, dec):
            qkd_s[c * N_PAIRS + p] = (x * d).astype(BF16)
        t_inv = [eye + n for n in pw]
        for _ in range(neumann_steps):
            pw16 = [x.astype(BF16) for x in pw]
            pw = [_dot(x, x) for x in pw16]
            t_inv = [t + _dot(t.astype(BF16), x.astype(BF16)) for t, x in zip(t_inv, pw)]
        for (c, p), t in zip(chains, t_inv):
            rhs = jnp.concatenate([stack(vb_s[rows(c), lanes(p)]), stack(kbe_s[rows(c), lanes(p)])], axis=1)
            uw = fold(_dot(t.astype(BF16), rhs.astype(BF16)))
            u_s[rows(c), lanes(p)] = uw[:, 0:LANES]
            w_s[rows(c), lanes(p)] = uw[:, LANES:2 * LANES]
            gcl = gcl_s[rows(c), lanes(p)]
            k_dec = k_s[rows(c), lanes(p)] * jnp.exp(gcl[DN_CHUNK - 1:DN_CHUNK, :] - gcl)
            kdt_s[c * N_PAIRS + p] = k_dec.T.astype(BF16)

    for c0 in range(0, n_slots, SLOT_GROUP):
        solve_slots(range(c0, min(c0 + SLOT_GROUP, n_slots)))

    pairs = range(N_PAIRS)
    for c in range(n_slots):
        sq = c // cps
        sps = [sp_ref[sq, p] for p in pairs]
        wq = [jnp.concatenate([w_s[rows(c), lanes(p)], qe_s[rows(c), lanes(p)]], axis=0).astype(BF16)
              for p in pairs]
        res = [_dot(x, sp.astype(BF16)) for x, sp in zip(wq, sps)]
        v_new = [u_s[rows(c), lanes(p)] - res[p][0:DN_CHUNK] for p in pairs]
        intra = [fold(_dot(qkd_s[c * N_PAIRS + p], stack(v_new[p]).astype(BF16))) for p in pairs]
        upd = [_dot(kdt_s[c * N_PAIRS + p], v_new[p].astype(BF16)) for p in pairs]
        for p in pairs:
            o_s[rows(c), lanes(p)] = res[p][DN_CHUNK:PAIR] + intra[p]
            g_last = gcl_s[DN_CHUNK * (c + 1) - 1:DN_CHUNK * (c + 1), lanes(p)]
            sp_ref[sq, p] = sps[p] * jnp.exp(g_last) + jnp.where(same, upd[p], 0.0)

    o = o_s[...]
    y = o * lax.rsqrt(_head_sums(o * o, bd) * (1.0 / DN_DV) + EPS) * gdn_ref[...]
    for s in range(ns):
        zs = z_ref[s * tl:(s + 1) * tl, :]
        y_ref[s * tl:(s + 1) * tl, :] = (y[s * seq_rows:s * seq_rows + tl] * _silu(zs)).astype(y_ref.dtype)

    @pl.when(li == n_l - 1)
    def _():
        for s in range(ns):
            for p in range(N_PAIRS):
                sp = sp_ref[s, p]
                spo_ref[s, 2 * p] = sp[0:DN_DK, 0:DN_DV]
                spo_ref[s, 2 * p + 1] = pltpu.roll(sp[DN_DK:PAIR], DN_DV, 1)[:, 0:DN_DV]


def _delta(h_qkv, z_dn, h_ba, cbuf8, cw8, alog_row, dtb_row, sp0, gdn_row, b, l, tl, ns, y_dtype):
    n_l = l // tl
    assert ns == 1 or n_l == 1
    n_slots = ns * (max(tl, DN_CHUNK) // DN_CHUNK)
    r = n_slots * DN_CHUNK
    consts = _delta_consts(r)
    row = lambda w: pl.BlockSpec((ns * tl, w), lambda bi, li: (bi * n_l + li, 0))
    cst = pl.BlockSpec((ns, CONV_HALO, CONV_CH), lambda bi, li: (bi, 0, 0))
    spb = pl.BlockSpec((ns, DN_HEADS, DN_DK, DN_DV), lambda bi, li: (bi, 0, 0, 0))
    wide = lambda n: pltpu.VMEM((r, n), F32)
    return pl.pallas_call(
        functools.partial(_delta_kernel, ns=ns, tl=tl, n_l=n_l),
        grid=(b // ns, n_l),
        in_specs=[row(CONV_CH), row(DN_WIDTH), row(LANES), cst, _const_spec(cw8.shape),
                  _const_spec(alog_row.shape), _const_spec(dtb_row.shape), spb, _const_spec(gdn_row.shape)]
                 + [_const_spec(c.shape) for c in consts],
        out_specs=[row(DN_WIDTH), cst, spb],
        out_shape=[jax.ShapeDtypeStruct((b * l, DN_WIDTH), y_dtype),
                   jax.ShapeDtypeStruct((b, CONV_HALO, CONV_CH), F32),
                   jax.ShapeDtypeStruct((b, DN_HEADS, DN_DK, DN_DV), F32)],
        scratch_shapes=[pltpu.VMEM((ns, CONV_HALO + tl, CONV_CH), F32), pltpu.VMEM((ns, N_PAIRS, PAIR, PAIR), F32)]
                       + [wide(DN_WIDTH)] * 7 + [wide(DN_HEADS * LANES)] + [wide(DN_WIDTH)] * 3
                       + [pltpu.VMEM((n_slots * N_PAIRS, PAIR, PAIR), BF16),
                          pltpu.VMEM((n_slots * N_PAIRS, PAIR, DN_CHUNK), BF16)],
        compiler_params=_cparams("parallel", "arbitrary"),
        name="gated_deltanet",
    )(h_qkv, z_dn, h_ba, cbuf8, cw8, alog_row, dtb_row, sp0, gdn_row, *consts)


def _merge_kernel(x_ref, ya_ref, yb_ref, yc_ref, g_ref, wa_ref, wb_ref, wc_ref, wo_ref, fg_ref, o_ref, *, final):
    merged = None
    for j, (y_ref, w_ref) in enumerate(((ya_ref, wa_ref), (yb_ref, wb_ref), (yc_ref, wc_ref))):
        br = _dot(y_ref[...].astype(BF16), w_ref[...])
        t = jax.nn.sigmoid(g_ref[:, D_MODEL * j:D_MODEL * (j + 1)]) * br
        merged = t if merged is None else merged + t
    x = x_ref[...] + _dot(merged.astype(BF16), wo_ref[...])
    o_ref[...] = _rms(x, fg_ref[...]) if final else x


def _merge(x2d, ya, yb, yc, h_gate, wa, wb, wc, wo, final_g, tm, final):
    t = x2d.shape[0]
    row = lambda w: pl.BlockSpec((tm, w), lambda i: (i, 0))
    return pl.pallas_call(
        functools.partial(_merge_kernel, final=final),
        grid=(t // tm,),
        in_specs=[row(D_MODEL), row(POOL_WIDTH), row(MLA_WIDTH), row(DN_WIDTH), row(N_BRANCH * D_MODEL)]
                 + [_const_spec(w.shape) for w in (wa, wb, wc, wo, final_g)],
        out_specs=row(D_MODEL),
        out_shape=jax.ShapeDtypeStruct((t, D_MODEL), F32),
        compiler_params=_cparams("parallel"),
        name="merge_out",
    )(x2d, ya, yb, yc, h_gate, wa, wb, wc, wo, final_g)


def _layer_weights(l, norm_g, wt_all, pool_mix, pool_scale, q_norm_g, w_uq, kv_norm_g, w_uk, w_uv,
                   conv_w, a_log, dt_bias, dn_norm_g, w_br_pool, w_br_mla, w_br_dn, w_out):
    o = IN_OFFSETS
    kr_rows = wt_all[l, o[KR_SPLIT]:o[KR_SPLIT + 1]]
    ba_rows = wt_all[l, o[BETA_SPLIT]:o[ALPHA_SPLIT + 1]]
    gap = jnp.zeros((ROPE_LANE0 - 2 * DN_HEADS, wt_all.shape[-1]), wt_all.dtype)
    tail = jnp.zeros((LANES - ROPE_LANE0 - QK_ROPE, wt_all.shape[-1]), wt_all.dtype)
    small_blk = jnp.concatenate([ba_rows, gap, kr_rows, tail], axis=0)

    dq = QK_NOPE + QK_ROPE
    wuq = w_uq[l].reshape(Q_LORA, MLA_HEADS, dq)
    wuq_p = jnp.pad(wuq, ((0, 0), (0, 0), (0, HEAD_BLOCK - dq))).reshape(Q_LORA, -1).astype(BF16)
    wuk_p = jnp.pad(w_uk[l], ((0, 0), (0, 0), (0, HEAD_BLOCK - QK_NOPE))).reshape(KV_LORA, -1).astype(BF16)
    wuv_f = w_uv[l].reshape(KV_LORA, MLA_WIDTH).astype(BF16)
    wuv_e = jnp.pad(w_uv[l][:, 0::2], ((0, 0), (0, 0), (0, HEAD_BLOCK - V_HEAD)))
    wuv_o = jnp.pad(w_uv[l][:, 1::2], ((0, 0), (0, 0), (HEAD_BLOCK - V_HEAD, 0)))
    wuv_p = jnp.stack([wuv_e, wuv_o], axis=2).reshape(KV_LORA, -1).astype(BF16)
    wcat = jnp.zeros((MLA_HEADS, HEAD_BLOCK, KV_LORA + LANES), F32)
    wcat = wcat.at[:, 0:QK_NOPE, 0:KV_LORA].set(jnp.transpose(w_uk[l], (1, 2, 0)))
    sel = jnp.eye(QK_ROPE, dtype=F32)
    wcat = wcat.at[:, ROPE_LANE0:ROPE_LANE0 + QK_ROPE, KV_LORA:KV_LORA + QK_ROPE].set(sel)
    lanes16 = slice(DN_HEADS, 2 * DN_HEADS)
    return dict(
        norm_g=norm_g[l][None, :], small_blk=small_blk,
        pool_mix=pool_mix[l].astype(BF16), pool_scale=pool_scale[l][None, :],
        q_norm_g=q_norm_g[l][None, :], wuq_p=wuq_p, kv_norm_g=kv_norm_g[l][None, :],
        wuk_p=wuk_p, wuv_f=wuv_f, wuv_p=wuv_p, wcat=wcat.astype(BF16),
        cw8=jnp.pad(conv_w[l], ((0, SUBLANES - CONV_W), (0, 0))),
        alog_row=jnp.zeros((1, LANES), F32).at[0, lanes16].set(a_log[l]),
        dtb_row=jnp.zeros((1, LANES), F32).at[0, lanes16].set(dt_bias[l]),
        gdn_row=jnp.tile(dn_norm_g[l], DN_HEADS)[None, :],
        wa=w_br_pool[l].astype(BF16), wb=w_br_mla[l].astype(BF16), wc=w_br_dn[l].astype(BF16),
        wo=w_out[l].astype(BF16))


def _tile(n, target):
    t = min(n, target)
    while n % t:
        t -= SUBLANES
    return t


def _group_layer(x2d, b, l, start, w, wt_all, layer, tabs, final_g, final, pool_buf, conv_buf, sp0, paged,
                 prev_kv=()):
    t = b * l
    small = l < 2 * SUBLANES
    y_dtype = F32 if small else BF16
    h_pool, z_pool, h_q, h_kv, h_small, z_mla, h_qkv, z_dn, h_gate = _inproj(
        x2d, w["norm_g"], wt_all, layer, w["small_blk"], _tile(t, 256))
    h_kr = h_ba = h_small

    ya, pool_st = _pool(h_pool, z_pool, pool_buf, w["pool_mix"], w["pool_scale"], b, l, _tile(l, 512), start, y_dtype)

    tm = _tile(l, 512)
    if paged is None:
        ckv, k_r, q, k, v = _mla_prep(h_q, h_kv, h_kr, tabs, w["q_norm_g"], w["wuq_p"], w["kv_norm_g"],
                                         (w["wuk_p"], w["wuv_p"]), b, l, tm, absorbed=False, prev=prev_kv)
        yb = _attn_prompt(q, k, v, z_mla, b, l, _tile(l, 512))
    else:
        page_table, cache_kv, cache_krt = paged
        tabs_all = [jnp.tile(tab, (b, 1)) for tab in tabs]
        ckv, k_r, qcat = _mla_prep(h_q, h_kv, h_kr, tabs_all, w["q_norm_g"], w["wuq_p"], w["kv_norm_g"],
                                      (w["wcat"],), 1, t, t, absorbed=True)
        ckv, k_r = ckv[0], k_r[0]
        wq = qcat.shape[-1] // MLA_HEADS
        qcat = qcat.reshape(b, l, MLA_HEADS, wq).transpose(0, 2, 1, 3).reshape(b, MLA_HEADS * l, wq)
        n_pages = page_table.shape[1]
        n_grp = PAGES_PER_GROUP
        while n_pages % (2 * n_grp):
            n_grp //= 2
        yb = _attn_paged(page_table, qcat, cache_kv, cache_krt, layer, ckv.reshape(b, l, KV_LORA),
                         k_r.reshape(b, l, QK_ROPE), w["wuv_f"], z_mla.reshape(b, l, MLA_WIDTH), n_grp)
        yb = yb.reshape(t, MLA_WIDTH)

    if l >= DN_CHUNK:
        tl, ns = _tile(l, 256), 1
    else:
        tl, ns = l, SAMPLE_SEQS_PER_STEP
        while b % ns:
            ns //= 2
    yc, conv_st, sp = _delta(h_qkv, z_dn, h_ba, conv_buf, w["cw8"], w["alog_row"], w["dtb_row"], sp0,
                             w["gdn_row"], b, l, tl, ns, y_dtype)

    x_out = _merge(x2d, ya, yb, yc, h_gate, w["wa"], w["wb"], w["wc"], w["wo"], final_g, _tile(t, 512), final)
    return x_out, ckv, k_r, pool_st[:, 1:], conv_st[:, CONV_HALO - (CONV_W - 1):], sp


def kernel(x_prompt, x_sample, cache_kv_latent, cache_k_rope, state_pool, state_conv, state_delta,
           page_table, norm_g, w_in, pool_mix, pool_scale, q_norm_g, w_uq, kv_norm_g, w_uk, w_uv,
           conv_w, a_log, dt_bias, dn_norm_g, w_br_pool, w_br_mla, w_br_dn, w_out, final_norm_g):
    bp, lp, d = x_prompt.shape
    db, ls, _ = x_sample.shape
    depth = w_in.shape[0]
    past_len = page_table.shape[1] * PAGE_SIZE
    final_g = final_norm_g[None, :]
    cache_krt = jnp.swapaxes(cache_k_rope, 2, 3)
    wt_all = jnp.swapaxes(w_in, 1, 2).astype(BF16)

    tabs_p = _rope_tables(lp, _tile(lp, 512), 0)
    tabs_s = _rope_tables(ls, ls, past_len)
    zero_pool = jnp.zeros((bp, POOL_HALO, POOL_WIDTH), F32)
    zero_conv = jnp.zeros((bp, CONV_HALO, CONV_CH), F32)
    zero_sp = jnp.zeros((bp, DN_HEADS, DN_DK, DN_DV), F32)

    xp = x_prompt.reshape(bp * lp, d)
    xs = x_sample.reshape(db * ls, d)
    outs_p, outs_s = [], []
    kv_stack = ()
    for l in range(depth):
        w = _layer_weights(l, norm_g, wt_all, pool_mix, pool_scale, q_norm_g, w_uq, kv_norm_g, w_uk, w_uv,
                           conv_w, a_log, dt_bias, dn_norm_g, w_br_pool, w_br_mla, w_br_dn, w_out)
        final = l == depth - 1
        xp, ckv, kr, pst, cst, sp = _group_layer(xp, bp, lp, 0, w, wt_all, l, tabs_p, final_g, final,
                                                 zero_pool, zero_conv, zero_sp, None, kv_stack)
        kv_stack = (ckv, kr)
        outs_p.append((pst, cst, sp))
        pool_buf = jnp.pad(state_pool[l], ((0, 0), (POOL_HALO - POOL_BUF, 0), (0, 0)))
        conv_buf = jnp.pad(state_conv[l], ((0, 0), (CONV_HALO - (CONV_W - 1), 0), (0, 0)))
        xs, ckv, kr, pst, cst, sp = _group_layer(xs, db, ls, past_len, w, wt_all, l, tabs_s, final_g, final,
                                                 pool_buf, conv_buf, state_delta[l],
                                                 (page_table, cache_kv_latent, cache_krt))
        outs_s.append((ckv.reshape(db, ls, KV_LORA), kr.reshape(db, ls, QK_ROPE), pst, cst, sp))
    stack = lambda outs, i: jnp.stack([o[i] for o in outs])
    p_kv = kv_stack[0].reshape(depth, bp, lp, KV_LORA)
    p_kr = kv_stack[1].reshape(depth, bp, lp, QK_ROPE)
    return (xp.reshape(bp, lp, d), xs.reshape(db, ls, d),
            p_kv, p_kr, *(stack(outs_p, i) for i in range(3)), *(stack(outs_s, i) for i in range(5)))
```

```python
import functools

import numpy as np
import jax
import jax.numpy as jnp
from jax import lax
from jax.experimental import pallas as pl
from jax.experimental.pallas import tpu as pltpu

F32 = jnp.float32
BF16 = jnp.bfloat16

D_MODEL = 1024
EPS = 1e-6
POOL_WINDOWS = (2, 4, 8, 16)
POOL_GROUP_DIM = 128
POOL_WIDTH = 512
POOL_BUF = 15
MLA_HEADS = 8
QK_NOPE = 64
QK_ROPE = 32
V_HEAD = 64
Q_LORA = 384
KV_LORA = 256
MLA_WIDTH = 512
MLA_SCALE = (QK_NOPE + QK_ROPE) ** -0.5
ROPE_THETA = 10000.0
DN_HEADS = 8
DN_DK = 64
DN_DV = 64
DN_WIDTH = 512
CONV_W = 4
CONV_CH = 1536
DN_CHUNK = 64
PAGE_SIZE = 128
N_BRANCH = 3

LANES = 128
SUBLANES = 8
HEAD_BLOCK = LANES
ROPE_LANE0 = QK_NOPE
ROPE_HALF = QK_ROPE // 2
POOL_HALO = 16
CONV_HALO = 8
VMEM_LIMIT = 48 * 1024 * 1024
NEG = -1e30
LOG2E = 1.4426950408889634
Q_SCALE = MLA_SCALE * LOG2E
PAGES_PER_GROUP = 64
PAGED_SLOTS = 3
PAGED_SUBBLOCKS = 16
PAGED_ROW_CHUNK = 16


def _cparams(*sem):
    return pltpu.CompilerParams(dimension_semantics=sem, vmem_limit_bytes=VMEM_LIMIT)


def _const_spec(shape):
    nd = len(shape)
    return pl.BlockSpec(shape, lambda *_: (0,) * nd, pipeline_mode=pl.Buffered(1))


def _dot(a, b):
    return jnp.dot(a, b, preferred_element_type=F32)


def _dot_nt(a, b):
    return lax.dot_general(a, b, (((1,), (1,)), ((), ())), preferred_element_type=F32)


def _silu(x):
    return x * jax.nn.sigmoid(x)


def _rms(x, g):
    return x * lax.rsqrt(jnp.mean(x * x, axis=-1, keepdims=True) + EPS) * g


IN_SPLITS = (POOL_WIDTH, POOL_WIDTH, Q_LORA, KV_LORA, QK_ROPE, MLA_WIDTH, CONV_CH, DN_WIDTH,
             DN_HEADS, DN_HEADS, N_BRANCH * D_MODEL)
IN_OFFSETS = tuple(int(v) for v in np.cumsum((0,) + IN_SPLITS))
IN_WIDTH = IN_OFFSETS[-1]
KR_SPLIT, BETA_SPLIT, ALPHA_SPLIT = 4, 8, 9
IN_OUTPUTS = ((IN_OFFSETS[0], POOL_WIDTH), (IN_OFFSETS[1], POOL_WIDTH), (IN_OFFSETS[2], Q_LORA),
              (IN_OFFSETS[3], KV_LORA), (None, LANES), (IN_OFFSETS[5], MLA_WIDTH),
              (IN_OFFSETS[6], CONV_CH), (IN_OFFSETS[7], DN_WIDTH),
              (IN_OFFSETS[10], N_BRANCH * D_MODEL))
IN_DOT_CHUNK = 512


def _inproj_kernel(x_ref, g_ref, w_ref, small_ref, *o_refs):
    xn = _rms(x_ref[...], g_ref[...]).astype(BF16)
    for o_ref, (row0, width) in zip(o_refs, IN_OUTPUTS):
        if row0 is None:
            o_ref[...] = _dot_nt(xn, small_ref[...])
            continue
        for c0 in range(0, width, IN_DOT_CHUNK):
            c1 = min(c0 + IN_DOT_CHUNK, width)
            o_ref[:, c0:c1] = _dot_nt(xn, w_ref[row0 + c0:row0 + c1, :])


def _inproj(x2d, norm_g, wt_all, layer, small_blk, tm):
    t = x2d.shape[0]
    row = lambda w: pl.BlockSpec((tm, w), lambda i: (i, 0))
    w_spec = pl.BlockSpec((None,) + wt_all.shape[1:], lambda i: (layer, 0, 0), pipeline_mode=pl.Buffered(1))
    return pl.pallas_call(
        _inproj_kernel,
        grid=(t // tm,),
        in_specs=[row(D_MODEL), _const_spec((1, D_MODEL)), w_spec, _const_spec(small_blk.shape)],
        out_specs=[row(w) for _, w in IN_OUTPUTS],
        out_shape=[jax.ShapeDtypeStruct((t, w), F32) for _, w in IN_OUTPUTS],
        compiler_params=_cparams("parallel"),
        name="inproj",
    )(x2d, norm_g, wt_all, small_blk)


def _pool_kernel(u_ref, z_ref, buf_ref, mix_ref, scale_ref, y_ref, st_ref, xx_ref, *, tl, start, n_l):
    li = pl.program_id(1)

    @pl.when(li == 0)
    def _():
        xx_ref[0:POOL_HALO, :] = buf_ref[...]

    u = u_ref[...]
    xx_ref[POOL_HALO:POOL_HALO + tl, :] = u
    row = lax.broadcasted_iota(jnp.int32, (tl, 1), 0)
    pos1 = start + li * tl + row + 1
    for gi, w in enumerate(POOL_WINDOWS):
        lanes = slice(POOL_GROUP_DIM * gi, POOL_GROUP_DIM * (gi + 1))
        s = xx_ref[:, lanes]
        k = 1
        while k < w:
            s = s + pltpu.roll(s, k, 0)
            k *= 2
        s = s[POOL_HALO:POOL_HALO + tl]
        cnt = jnp.minimum(pos1, w).astype(F32)
        d = s / cnt - u[:, lanes]
        y = _dot(d.astype(BF16), mix_ref[gi]) * scale_ref[:, lanes]
        y_ref[:, lanes] = (y * _silu(z_ref[:, lanes])).astype(y_ref.dtype)

    tail = xx_ref[tl:tl + POOL_HALO, :]

    @pl.when(li == n_l - 1)
    def _():
        st_ref[...] = tail

    xx_ref[0:POOL_HALO, :] = tail


def _pool(h_pool, z_pool, buf16, mix, scale, b, l, tl, start, y_dtype):
    n_l = l // tl
    row = lambda w: pl.BlockSpec((tl, w), lambda bi, li: (bi * n_l + li, 0))
    st = pl.BlockSpec((None, POOL_HALO, POOL_WIDTH), lambda bi, li: (bi, 0, 0))
    return pl.pallas_call(
        functools.partial(_pool_kernel, tl=tl, start=start, n_l=n_l),
        grid=(b, n_l),
        in_specs=[row(POOL_WIDTH), row(POOL_WIDTH), st, _const_spec(mix.shape), _const_spec(scale.shape)],
        out_specs=[row(POOL_WIDTH), st],
        out_shape=[jax.ShapeDtypeStruct((b * l, POOL_WIDTH), y_dtype),
                   jax.ShapeDtypeStruct((b, POOL_HALO, POOL_WIDTH), F32)],
        scratch_shapes=[pltpu.VMEM((POOL_HALO + tl, POOL_WIDTH), F32)],
        compiler_params=_cparams("parallel", "arbitrary"),
        name="pool_mixer",
    )(h_pool, z_pool, buf16, mix, scale)


def _rope_table_kernel(inv_ref, c_ref, s1_ref, s2_ref, *, tl, start):
    i = pl.program_id(0)
    shape = (tl, LANES)
    pos = (start + i * tl + lax.broadcasted_iota(jnp.int32, shape, 0)).astype(F32)
    lane = lax.broadcasted_iota(jnp.int32, shape, 1)
    ang = pos * inv_ref[...]
    cos, sin = jnp.cos(ang), jnp.sin(ang)
    first = (lane >= ROPE_LANE0) & (lane < ROPE_LANE0 + ROPE_HALF)
    second = (lane >= ROPE_LANE0 + ROPE_HALF) & (lane < ROPE_LANE0 + QK_ROPE)
    c_ref[...] = jnp.where(first | second, cos, 1.0)
    s1_ref[...] = jnp.where(first, -sin, 0.0)
    s2_ref[...] = jnp.where(second, sin, 0.0)


def _rope_tables(l, tl, start):
    half = ROPE_HALF
    inv = jnp.power(ROPE_THETA, -jnp.arange(half, dtype=F32) / half)
    inv_lane = jnp.zeros((1, LANES), F32)
    inv_lane = inv_lane.at[0, ROPE_LANE0:ROPE_LANE0 + half].set(inv)
    inv_lane = inv_lane.at[0, ROPE_LANE0 + half:ROPE_LANE0 + QK_ROPE].set(inv)
    blk = pl.BlockSpec((tl, LANES), lambda i: (i, 0))
    return pl.pallas_call(
        functools.partial(_rope_table_kernel, tl=tl, start=start),
        grid=(l // tl,),
        in_specs=[_const_spec((1, LANES))],
        out_specs=[blk, blk, blk],
        out_shape=[jax.ShapeDtypeStruct((l, LANES), F32)] * 3,
        compiler_params=_cparams("parallel"),
        name="rope_tables",
    )(inv_lane)


def _rope(x, c, s1, s2):
    return x * c + pltpu.roll(x, LANES - ROPE_HALF, 1) * s1 + pltpu.roll(x, ROPE_HALF, 1) * s2


def _mla_prep_kernel(hq_ref, hkv_ref, hkr_ref, c_ref, s1_ref, s2_ref, qg_ref, wuq_ref, kvg_ref, *rest,
                     absorbed, n_prev):
    c, s1, s2 = c_ref[...], s1_ref[...], s2_ref[...]
    cq = _rms(hq_ref[...], qg_ref[...]).astype(BF16)
    q = _dot(cq, wuq_ref[...])
    ckv = _rms(hkv_ref[...], kvg_ref[...])
    lane128 = lax.broadcasted_iota(jnp.int32, c.shape, 1)
    on_rope = (lane128 >= ROPE_LANE0) & (lane128 < ROPE_LANE0 + QK_ROPE)
    kr = jnp.where(on_rope, _rope(hkr_ref[...], c, s1, s2), 0.0)
    if n_prev:
        pckv_ref, pkr_ref, *rest = rest
    if absorbed:
        wcat_ref, ckv_ref, kr_ref, qcat_ref = rest
    else:
        wuk_ref, wuv_ref, ckv_ref, kr_ref, q_ref, k_ref, v_ref = rest
        ckv16 = ckv.astype(BF16)
        knope = _dot(ckv16, wuk_ref[...])
        lane = lax.broadcasted_iota(jnp.int32, (1, MLA_HEADS * HEAD_BLOCK), 1)
        odd = (lane // HEAD_BLOCK) % 2
        ones_col = (lane % HEAD_BLOCK == V_HEAD * (1 - odd)).astype(F32)
        v_ref[...] = (_dot(ckv16, wuv_ref[...]) + ones_col).astype(BF16)
    ckv_ref[n_prev] = ckv
    if absorbed:
        kr_ref[n_prev] = pltpu.roll(kr, LANES - ROPE_LANE0, 1)[:, 0:QK_ROPE]
    else:
        kr_ref[n_prev] = kr.T[ROPE_LANE0:ROPE_LANE0 + QK_ROPE, :]
    if n_prev:
        ckv_ref[0:n_prev] = pckv_ref[...]
        kr_ref[0:n_prev] = pkr_ref[...]
    for h in range(MLA_HEADS):
        blk = slice(HEAD_BLOCK * h, HEAD_BLOCK * (h + 1))
        qh = (_rope(q[:, blk], c, s1, s2) * Q_SCALE).astype(BF16)
        if absorbed:
            w = wcat_ref.shape[-1]
            qcat_ref[:, w * h:w * (h + 1)] = _dot(qh, wcat_ref[h]).astype(BF16)
        else:
            q_ref[:, blk] = qh
            k_ref[:, blk] = (knope[:, blk] + kr).astype(BF16)


def _mla_prep(h_q, h_kv, h_kr, tabs, q_norm_g, wuq_p, kv_norm_g, extra_w, b, l, tm, absorbed, prev=()):
    n_l = l // tm
    t = b * l
    n_prev = prev[0].shape[0] if prev else 0
    row = lambda w: pl.BlockSpec((tm, w), lambda bi, li: (bi * n_l + li, 0))
    stk = lambda n, w: pl.BlockSpec((n, tm, w), lambda bi, li: (0, bi * n_l + li, 0))
    tab = pl.BlockSpec((tm, LANES), lambda bi, li: (li, 0))
    hw = MLA_HEADS * HEAD_BLOCK
    in_specs = [row(Q_LORA), row(KV_LORA), row(HEAD_BLOCK), tab, tab, tab,
                _const_spec(q_norm_g.shape), _const_spec(wuq_p.shape), _const_spec(kv_norm_g.shape)]
    if absorbed:
        kr_spec = lambda n: stk(n, QK_ROPE)
        kr_shape = (n_prev + 1, t, QK_ROPE)
    else:
        kr_spec = lambda n: pl.BlockSpec((n, None, QK_ROPE, tm), lambda bi, li: (0, bi, 0, li))
        kr_shape = (n_prev + 1, b, QK_ROPE, l)
    if n_prev:
        in_specs += [stk(n_prev, KV_LORA), kr_spec(n_prev)]
    in_specs += [_const_spec(w.shape) for w in extra_w]
    out_specs = [stk(n_prev + 1, KV_LORA), kr_spec(n_prev + 1)]
    out_shape = [jax.ShapeDtypeStruct((n_prev + 1, t, KV_LORA), F32), jax.ShapeDtypeStruct(kr_shape, F32)]
    if absorbed:
        wc = MLA_HEADS * extra_w[0].shape[-1]
        out_specs += [row(wc)]
        out_shape += [jax.ShapeDtypeStruct((t, wc), BF16)]
    else:
        out_specs += [row(hw), row(hw), row(hw)]
        out_shape += [jax.ShapeDtypeStruct((t, hw), BF16)] * 3
    return pl.pallas_call(
        functools.partial(_mla_prep_kernel, absorbed=absorbed, n_prev=n_prev),
        grid=(b, n_l),
        in_specs=in_specs, out_specs=out_specs, out_shape=out_shape,
        compiler_params=_cparams("parallel", "parallel"),
        name="mla_prep_absorbed" if absorbed else "mla_prep",
    )(h_q, h_kv, h_kr, *tabs, q_norm_g, wuq_p, kv_norm_g, *prev, *extra_w)


ATTN_ROW_CHUNK = 32


def _softmax_update(m_b, l_p, s, row_chunk):
    r, n = s.shape[0], s.shape[1] // LANES
    cols = [slice(LANES * c, LANES * (c + 1)) for c in range(n)]
    folded = s[:, cols[0]]
    for cb in cols[1:]:
        folded = jnp.maximum(folded, s[:, cb])
    mx = jnp.max(folded, axis=-1, keepdims=True)
    m_new = jnp.maximum(m_b, jnp.broadcast_to(mx, m_b.shape))
    a_b = jnp.exp2(m_b - m_new)
    l_rows, p_rows = [], []
    for c0 in range(0, r, row_chunk):
        rows = slice(c0, c0 + row_chunk)
        m_c = m_new[rows]
        pieces = [jnp.exp2(s[rows, cb] - m_c) for cb in cols]
        if l_p is not None:
            tot = pieces[0]
            for pc in pieces[1:]:
                tot = tot + pc
            l_rows.append(a_b[rows] * l_p[rows] + tot)
        p_rows.append(jnp.concatenate([pc.astype(BF16) for pc in pieces], axis=1))
    cat = lambda parts: parts[0] if len(parts) == 1 else jnp.concatenate(parts, axis=0)
    return m_new, (None if l_p is None else cat(l_rows)), a_b, cat(p_rows)


def _attn_kernel(q_ref, k_ref, v_ref, z_ref, y_ref, *, ta):
    qi = pl.program_id(1)
    rc = min(ATTN_ROW_CHUNK, ta)
    low_half = lax.broadcasted_iota(jnp.int32, (ta, LANES), 1) < V_HEAD
    lane_id = lax.broadcasted_iota(jnp.int32, (ta, LANES), 1)
    causal = (lax.broadcasted_iota(jnp.int32, (ta, ta), 1)
              <= lax.broadcasted_iota(jnp.int32, (ta, ta), 0))
    for p in range(MLA_HEADS // 2):
        pair = slice(LANES * p, LANES * (p + 1))
        blks = [slice(HEAD_BLOCK * h, HEAD_BLOCK * (h + 1)) for h in (2 * p, 2 * p + 1)]
        qhs = [q_ref[:, blk] for blk in blks]

        def step(kb, carry, masked, pair=pair, blks=blks, qhs=qhs):
            r0 = pl.multiple_of(kb * ta, ta)
            scores = [_dot_nt(qh, k_ref[pl.ds(r0, ta), blk]) for qh, blk in zip(qhs, blks)]
            out = []
            for (m_b, acc), s, blk in zip(carry, scores, blks):
                if masked:
                    s = jnp.where(causal, s, NEG)
                m_b, _, a_b, pm = _softmax_update(m_b, None, s, rc)
                out.append((m_b, a_b * acc + _dot(pm, v_ref[pl.ds(r0, ta), blk])))
            return tuple(out)

        init = (jnp.full((ta, LANES), NEG, F32), jnp.zeros((ta, LANES), F32))
        carry = lax.fori_loop(0, qi, functools.partial(step, masked=False), (init, init))
        (_, acc0), (_, acc1) = step(qi, carry, True)
        den = lambda acc, at: jnp.sum(jnp.where(lane_id == at, acc, 0.0), axis=-1, keepdims=True)
        o = jnp.where(low_half, acc0 / den(acc0, V_HEAD), acc1 / den(acc1, 0))
        y_ref[:, pair] = (o * _silu(z_ref[:, pair])).astype(y_ref.dtype)


def _attn_prompt(q, k, v, z_mla, b, l, ta):
    n_q = l // ta
    hw = MLA_HEADS * HEAD_BLOCK
    row = lambda w: pl.BlockSpec((ta, w), lambda bi, qi: (bi * n_q + qi, 0))
    seq = lambda w: pl.BlockSpec((l, w), lambda bi, qi: (bi, 0))
    return pl.pallas_call(
        functools.partial(_attn_kernel, ta=ta),
        grid=(b, n_q),
        in_specs=[row(hw), seq(hw), seq(hw), row(MLA_WIDTH)],
        out_specs=row(MLA_WIDTH),
        out_shape=jax.ShapeDtypeStruct((b * l, MLA_WIDTH), BF16),
        compiler_params=_cparams("parallel", "arbitrary"),
        name="mla_attention_prompt",
    )(q, k, v, z_mla)


def _attn_paged_kernel(pt_ref, qcat_ref, nkv_ref, nkr_ref, wuv_ref, z_ref, kv_hbm, krt_hbm, y_ref,
                       kvbuf, krbuf, sem, *, layer, n_pages, n_grp, l_new):
    b = pl.program_id(0)
    n_seq = pl.num_programs(0)
    n_groups = n_pages // n_grp
    rows = MLA_HEADS * l_new
    ahead = PAGED_SLOTS - 1

    sub_pages = n_grp // PAGED_SUBBLOCKS

    def group_copies(seq, j, pages):
        slot = (seq * n_groups + j) % PAGED_SLOTS
        cps = []
        for g in pages:
            page = pt_ref[seq * n_pages + j * n_grp + g]
            keys = pl.ds(PAGE_SIZE * g, PAGE_SIZE)
            cps.append(pltpu.make_async_copy(kv_hbm.at[layer, page], kvbuf.at[slot, keys, :], sem.at[0, slot]))
            cps.append(pltpu.make_async_copy(krt_hbm.at[layer, page], krbuf.at[slot, g], sem.at[1, slot]))
        return cps

    def start_pages(seq, j, pages):
        for cp in group_copies(seq, j, pages):
            cp.start()

    def wait_group(seq, j):
        for cp in group_copies(seq, j, range(n_grp)):
            cp.wait()

    def start_ahead(j, sb):
        nxt = j + ahead
        pages = range(sub_pages * sb, sub_pages * (sb + 1))
        if nxt < n_groups:
            start_pages(b, nxt, pages)
        else:
            @pl.when(b + 1 < n_seq)
            def _():
                start_pages(b + 1, nxt - n_groups, pages)

    @pl.when(b == 0)
    def _():
        for j0 in range(ahead):
            start_pages(b, j0, range(n_grp))

    qc = qcat_ref[...]
    ql, qr = qc[:, :KV_LORA], qc[:, KV_LORA:KV_LORA + QK_ROPE]

    def update(state, s, kv16):
        m_b, l_p, acc = state
        m_b, l_p, a_b, pm = _softmax_update(m_b, l_p, s, PAGED_ROW_CHUNK)
        a_wide = jnp.concatenate([a_b] * (KV_LORA // LANES), axis=1)
        return m_b, l_p, a_wide * acc + _dot(pm, kv16)

    state = (jnp.full((rows, LANES), NEG, F32), jnp.zeros((rows, LANES), F32), jnp.zeros((rows, KV_LORA), F32))
    sub_keys = n_grp * PAGE_SIZE // PAGED_SUBBLOCKS
    for j in range(n_groups):
        wait_group(b, j)
        slot = (b * n_groups + j) % PAGED_SLOTS
        subs = []
        for sb in range(PAGED_SUBBLOCKS):
            keys = slice(sub_keys * sb, sub_keys * (sb + 1))
            kv16 = kvbuf[slot, keys, :].astype(BF16)
            kr16 = jnp.concatenate([krbuf[slot, g] for g in range(sub_pages * sb, sub_pages * (sb + 1))],
                                   axis=1).astype(BF16)
            subs.append((_dot_nt(ql, kv16) + _dot(qr, kr16), kv16))
            start_ahead(j, sb)
        for s, kv16 in subs:
            state = update(state, s, kv16)

    pad = PAGE_SIZE - l_new
    nkv = jnp.concatenate([nkv_ref[...], jnp.zeros((pad, KV_LORA), F32)], axis=0).astype(BF16)
    nkr = jnp.concatenate([nkr_ref[...], jnp.zeros((pad, QK_ROPE), F32)], axis=0).astype(BF16)
    tok = lax.broadcasted_iota(jnp.int32, (rows, PAGE_SIZE), 0) % l_new
    key = lax.broadcasted_iota(jnp.int32, (rows, PAGE_SIZE), 1)
    _, l_p, acc = update(state, jnp.where(key <= tok, _dot_nt(ql, nkv) + _dot_nt(qr, nkr), NEG), nkv)
    o = (acc / jnp.sum(l_p, axis=-1, keepdims=True)).astype(BF16)
    full = _dot(o, wuv_ref[...])
    rh = lax.broadcasted_iota(jnp.int32, full.shape, 0) // l_new
    ch = lax.broadcasted_iota(jnp.int32, full.shape, 1) // V_HEAD
    full = jnp.where(rh == ch, full, 0.0)
    out = full[0:l_new]
    for h in range(1, MLA_HEADS):
        out = out + full[l_new * h:l_new * (h + 1)]
    y_ref[...] = (out * _silu(z_ref[...])).astype(y_ref.dtype)


def _attn_paged(page_table, qcat, cache_kv, cache_krt, layer, new_kv, new_kr, wuv, z_mla, n_grp):
    db, n_pages = page_table.shape
    l_new = new_kv.shape[1]
    rows = MLA_HEADS * l_new
    assert n_pages % n_grp == 0 and n_pages // n_grp >= PAGED_SLOTS - 1 and n_grp % PAGED_SUBBLOCKS == 0
    wq = qcat.shape[-1]
    per_b = lambda r, w: pl.BlockSpec((None, r, w), lambda b, pt: (b, 0, 0))
    hbm = pl.BlockSpec(memory_space=pl.ANY)
    grid_spec = pltpu.PrefetchScalarGridSpec(
        num_scalar_prefetch=1, grid=(db,),
        in_specs=[per_b(rows, wq), per_b(l_new, KV_LORA), per_b(l_new, QK_ROPE),
                  pl.BlockSpec(wuv.shape, lambda b, pt: (0, 0)), per_b(l_new, MLA_WIDTH), hbm, hbm],
        out_specs=per_b(l_new, MLA_WIDTH),
        scratch_shapes=[pltpu.VMEM((PAGED_SLOTS, n_grp * PAGE_SIZE, KV_LORA), F32),
                        pltpu.VMEM((PAGED_SLOTS, n_grp, QK_ROPE, PAGE_SIZE), F32),
                        pltpu.SemaphoreType.DMA((2, PAGED_SLOTS))])
    return pl.pallas_call(
        functools.partial(_attn_paged_kernel, layer=layer, n_pages=n_pages, n_grp=n_grp, l_new=l_new),
        grid_spec=grid_spec,
        out_shape=jax.ShapeDtypeStruct((db, l_new, MLA_WIDTH), F32),
        compiler_params=_cparams("arbitrary"),
        name="mla_attention_paged",
    )(page_table.reshape(-1), qcat, new_kv, new_kr, wuv, z_mla, cache_kv, cache_krt)


PAIR = 2 * DN_CHUNK
N_PAIRS = DN_HEADS // 2
SLOT_GROUP = 4
SAMPLE_SEQS_PER_STEP = 4


def _delta_consts(r):
    hid = np.arange(PAIR) // DN_DK
    bd = hid[:, None] == hid[None, :]
    rr = np.arange(r)
    tri = (rr[:, None] // DN_CHUNK == rr[None, :] // DN_CHUNK) & (rr[:, None] >= rr[None, :])
    return tuple(jnp.asarray(a.astype(np.float32), dtype=BF16) for a in (bd, tri))


def _split3(x):
    hi = x.astype(BF16)
    r1 = x - hi.astype(F32)
    mid = r1.astype(BF16)
    lo = (r1 - mid.astype(F32)).astype(BF16)
    return hi, mid, lo


def _sel_dot(sel, x):
    hi, mid, lo = _split3(x)
    return _dot(sel, hi) + _dot(sel, mid) + _dot(sel, lo)


def _head_sums(x2, bd):
    x16 = x2.astype(BF16)
    return jnp.concatenate([_dot(x16[:, LANES * p:LANES * (p + 1)], bd) for p in range(N_PAIRS)], axis=1)


def _delta_kernel(u_ref, z_ref, ba_ref, cbuf_ref, cw_ref, alog_ref, dtb_ref, sp0_ref, gdn_ref,
                  bd_ref, tri_ref,
                  y_ref, cst_ref, spo_ref,
                  xx_ref, sp_ref, q_s, k_s, kb_s, qe_s, vb_s, kbe_s, gcl_s, gf_s, u_s, w_s, o_s, qkd_s, kdt_s,
                  *, ns, tl, n_l):
    li = pl.program_id(1)
    cps = max(tl, DN_CHUNK) // DN_CHUNK
    neumann_steps = max(int(np.ceil(np.log2(min(tl, DN_CHUNK)))) - 1, 0)
    seq_rows = cps * DN_CHUNK
    n_slots = ns * cps

    @pl.when(li == 0)
    def _():
        xx_ref[:, 0:CONV_HALO, :] = cbuf_ref[...]
        zero = jnp.zeros((DN_DK, DN_DV), F32)
        for s in range(ns):
            for p in range(N_PAIRS):
                top = jnp.concatenate([sp0_ref[s, 2 * p], zero], axis=1)
                bot = jnp.concatenate([zero, sp0_ref[s, 2 * p + 1]], axis=1)
                sp_ref[s, p] = jnp.concatenate([top, bot], axis=0)

    def pad_rows(a):
        if tl == seq_rows:
            return a
        return jnp.concatenate([a, jnp.zeros((seq_rows - tl, a.shape[1]), a.dtype)], axis=0)

    def per_seq(a):
        if ns == 1:
            return pad_rows(a)
        return jnp.concatenate([pad_rows(a[s * tl:(s + 1) * tl]) for s in range(ns)], axis=0)

    pieces = []
    for s in range(ns):
        xx_ref[s, CONV_HALO:CONV_HALO + tl, :] = u_ref[s * tl:(s + 1) * tl, :]
        ext = xx_ref[s]
        acc = None
        for j in range(CONV_W):
            back = CONV_W - 1 - j
            shifted = ext if back == 0 else pltpu.roll(ext, back, 0)
            term = shifted[CONV_HALO:CONV_HALO + tl] * cw_ref[j:j + 1, :]
            acc = term if acc is None else acc + term
        pieces.append(pad_rows(_silu(acc)))
    qkv = pieces[0] if ns == 1 else jnp.concatenate(pieces, axis=0)
    tail = xx_ref[:, tl:tl + CONV_HALO, :]

    @pl.when(li == n_l - 1)
    def _():
        cst_ref[...] = tail

    xx_ref[:, 0:CONV_HALO, :] = tail

    bd = bd_ref[...]
    q = qkv[:, 0:DN_WIDTH]
    k = qkv[:, DN_WIDTH:2 * DN_WIDTH]
    v = qkv[:, 2 * DN_WIDTH:3 * DN_WIDTH]
    qn = q * lax.rsqrt(_head_sums(q * q, bd) + EPS) * (DN_DK ** -0.5)
    kn = k * lax.rsqrt(_head_sums(k * k, bd) + EPS)
    ba = ba_ref[...]
    lane = lax.broadcasted_iota(jnp.int32, ba.shape, 1)
    beta = jnp.where(lane < DN_HEADS, jax.nn.sigmoid(ba), 0.0)
    xa = ba + dtb_ref[...]
    softplus = jnp.maximum(xa, 0.0) + jnp.log1p(jnp.exp(-jnp.abs(xa)))
    g = jnp.where((lane >= DN_HEADS) & (lane < 2 * DN_HEADS), -jnp.exp(alog_ref[...]) * softplus, 0.0)
    beta, g = per_seq(beta), per_seq(g)
    gc = _sel_dot(tri_ref[...], g)
    n_rows = gc.shape[0]
    first_r = lax.broadcasted_iota(jnp.int32, (n_rows, LANES), 1) < DN_DK

    def head_lanes(a, lane0):
        full = [jnp.broadcast_to(a[:, lane0 + h:lane0 + h + 1], (n_rows, LANES)) for h in range(DN_HEADS)]
        pairs = [jnp.where(first_r, full[2 * p], full[2 * p + 1]) for p in range(N_PAIRS)]
        return jnp.concatenate(pairs, axis=1), full

    beta_l, _ = head_lanes(beta, 0)
    gc_l, gc_full = head_lanes(gc, DN_HEADS)
    egc = jnp.exp(gc_l)
    kb = kn * beta_l
    q_s[...] = qn
    k_s[...] = kn
    kb_s[...] = kb
    qe_s[...] = qn * egc
    vb_s[...] = v * beta_l
    kbe_s[...] = kb * egc
    gcl_s[...] = gc_l
    for h in range(DN_HEADS):
        gf_s[:, LANES * h:LANES * (h + 1)] = gc_full[h]

    ri = lax.broadcasted_iota(jnp.int32, (PAIR, PAIR), 0)
    ci = lax.broadcasted_iota(jnp.int32, (PAIR, PAIR), 1)
    same = (ri // DN_CHUNK) == (ci // DN_CHUNK)
    incl = same & (ri >= ci)
    strict = same & (ri > ci)
    eye = (ri == ci).astype(F32)
    first = lax.broadcasted_iota(jnp.int32, (DN_CHUNK, LANES), 1) < DN_DK

    def stack(a):
        return jnp.concatenate([jnp.where(first, a, 0.0), jnp.where(first, 0.0, a)], axis=0)

    def fold(a):
        return a[0:DN_CHUNK] + a[DN_CHUNK:PAIR]

    rows = lambda c: slice(DN_CHUNK * c, DN_CHUNK * (c + 1))
    lanes = lambda p: slice(LANES * p, LANES * (p + 1))

    def solve_slots(slots):
        chains = [(c, p) for c in slots for p in range(N_PAIRS)]
        kst = [stack(k_s[rows(c), lanes(p)]).astype(BF16) for c, p in chains]
        a_mat = [_dot_nt(stack(kb_s[rows(c), lanes(p)]).astype(BF16), ks) for (c, p), ks in zip(chains, kst)]
        qk = [_dot_nt(stack(q_s[rows(c), lanes(p)]).astype(BF16), ks) for (c, p), ks in zip(chains, kst)]
        dec = []
        for c, p in chains:
            g_col = jnp.concatenate([gf_s[rows(c), lanes(2 * p)], gf_s[rows(c), lanes(2 * p + 1)]], axis=0)
            diff = g_col - g_col.T
            dec.append(jnp.where(incl, jnp.exp(jnp.where(incl, diff, 0.0)), 0.0))
        pw = [jnp.where(strict, -(a * d), 0.0) for a, d in zip(a_mat, dec)]
        for (c, p), x, d in zip(chains, qk, dec):
            qkd_s[c * N_PAIRS + p] = (x * d).astype(BF16)
        t_inv = [eye + n for n in pw]
        for _ in range(neumann_steps):
            pw16 = [x.astype(BF16) for x in pw]
            pw = [_dot(x, x) for x in pw16]
            t_inv = [t + _dot(t.astype(BF16), x.astype(BF16)) for t, x in zip(t_inv, pw)]
        for (c, p), t in zip(chains, t_inv):
            rhs = jnp.concatenate([stack(vb_s[rows(c), lanes(p)]), stack(kbe_s[rows(c), lanes(p)])], axis=1)
            uw = fold(_dot(t.astype(BF16), rhs.astype(BF16)))
            u_s[rows(c), lanes(p)] = uw[:, 0:LANES]
            w_s[rows(c), lanes(p)] = uw[:, LANES:2 * LANES]
            gcl = gcl_s[rows(c), lanes(p)]
            k_dec = k_s[rows(c), lanes(p)] * jnp.exp(gcl[DN_CHUNK - 1:DN_CHUNK, :] - gcl)
            kdt_s[c * N_PAIRS + p] = k_dec.T.astype(BF16)

    for c0 in range(0, n_slots, SLOT_GROUP):
        solve_slots(range(c0, min(c0 + SLOT_GROUP, n_slots)))

    pairs = range(N_PAIRS)
    for c in range(n_slots):
        sq = c // cps
        sps = [sp_ref[sq, p] for p in pairs]
        wq = [jnp.concatenate([w_s[rows(c), lanes(p)], qe_s[rows(c), lanes(p)]], axis=0).astype(BF16)
              for p in pairs]
        res = [_dot(x, sp.astype(BF16)) for x, sp in zip(wq, sps)]
        v_new = [u_s[rows(c), lanes(p)] - res[p][0:DN_CHUNK] for p in pairs]
        intra = [fold(_dot(qkd_s[c * N_PAIRS + p], stack(v_new[p]).astype(BF16))) for p in pairs]
        upd = [_dot(kdt_s[c * N_PAIRS + p], v_new[p].astype(BF16)) for p in pairs]
        for p in pairs:
            o_s[rows(c), lanes(p)] = res[p][DN_CHUNK:PAIR] + intra[p]
            g_last = gcl_s[DN_CHUNK * (c + 1) - 1:DN_CHUNK * (c + 1), lanes(p)]
            sp_ref[sq, p] = sps[p] * jnp.exp(g_last) + jnp.where(same, upd[p], 0.0)

    o = o_s[...]
    y = o * lax.rsqrt(_head_sums(o * o, bd) * (1.0 / DN_DV) + EPS) * gdn_ref[...]
    for s in range(ns):
        zs = z_ref[s * tl:(s + 1) * tl, :]
        y_ref[s * tl:(s + 1) * tl, :] = (y[s * seq_rows:s * seq_rows + tl] * _silu(zs)).astype(y_ref.dtype)

    @pl.when(li == n_l - 1)
    def _():
        for s in range(ns):
            for p in range(N_PAIRS):
                sp = sp_ref[s, p]
                spo_ref[s, 2 * p] = sp[0:DN_DK, 0:DN_DV]
                spo_ref[s, 2 * p + 1] = pltpu.roll(sp[DN_DK:PAIR], DN_DV, 1)[:, 0:DN_DV]


def _delta(h_qkv, z_dn, h_ba, cbuf8, cw8, alog_row, dtb_row, sp0_all, sp0_layer, gdn_row, b, l, tl, ns, y_dtype):
    n_l = l // tl
    assert ns == 1 or n_l == 1
    n_slots = ns * (max(tl, DN_CHUNK) // DN_CHUNK)
    r = n_slots * DN_CHUNK
    consts = _delta_consts(r)
    row = lambda w: pl.BlockSpec((ns * tl, w), lambda bi, li: (bi * n_l + li, 0))
    cst = pl.BlockSpec((ns, CONV_HALO, CONV_CH), lambda bi, li: (bi, 0, 0))
    spb = pl.BlockSpec((ns, DN_HEADS, DN_DK, DN_DV), lambda bi, li: (bi, 0, 0, 0))
    sp_in = pl.BlockSpec((None, ns, DN_HEADS, DN_DK, DN_DV), lambda bi, li: (sp0_layer, bi, 0, 0, 0))
    wide = lambda n: pltpu.VMEM((r, n), F32)
    return pl.pallas_call(
        functools.partial(_delta_kernel, ns=ns, tl=tl, n_l=n_l),
        grid=(b // ns, n_l),
        in_specs=[row(CONV_CH), row(DN_WIDTH), row(LANES), cst, _const_spec(cw8.shape),
                  _const_spec(alog_row.shape), _const_spec(dtb_row.shape), sp_in, _const_spec(gdn_row.shape)]
                 + [_const_spec(c.shape) for c in consts],
        out_specs=[row(DN_WIDTH), cst, spb],
        out_shape=[jax.ShapeDtypeStruct((b * l, DN_WIDTH), y_dtype),
                   jax.ShapeDtypeStruct((b, CONV_HALO, CONV_CH), F32),
                   jax.ShapeDtypeStruct((b, DN_HEADS, DN_DK, DN_DV), F32)],
        scratch_shapes=[pltpu.VMEM((ns, CONV_HALO + tl, CONV_CH), F32), pltpu.VMEM((ns, N_PAIRS, PAIR, PAIR), F32)]
                       + [wide(DN_WIDTH)] * 7 + [wide(DN_HEADS * LANES)] + [wide(DN_WIDTH)] * 3
                       + [pltpu.VMEM((n_slots * N_PAIRS, PAIR, PAIR), BF16),
                          pltpu.VMEM((n_slots * N_PAIRS, PAIR, DN_CHUNK), BF16)],
        compiler_params=_cparams("parallel", "arbitrary"),
        name="gated_deltanet",
    )(h_qkv, z_dn, h_ba, cbuf8, cw8, alog_row, dtb_row, sp0_all, gdn_row, *consts)


def _merge_kernel(x_ref, ya_ref, yb_ref, yc_ref, g_ref, wa_ref, wb_ref, wc_ref, wo_ref, fg_ref, o_ref, *, final):
    merged = None
    for j, (y_ref, w_ref) in enumerate(((ya_ref, wa_ref), (yb_ref, wb_ref), (yc_ref, wc_ref))):
        br = _dot(y_ref[...].astype(BF16), w_ref[...])
        t = jax.nn.sigmoid(g_ref[:, D_MODEL * j:D_MODEL * (j + 1)]) * br
        merged = t if merged is None else merged + t
    x = x_ref[...] + _dot(merged.astype(BF16), wo_ref[...])
    o_ref[...] = _rms(x, fg_ref[...]) if final else x


def _merge(x2d, ya, yb, yc, h_gate, wa, wb, wc, wo, final_g, tm, final):
    t = x2d.shape[0]
    row = lambda w: pl.BlockSpec((tm, w), lambda i: (i, 0))
    return pl.pallas_call(
        functools.partial(_merge_kernel, final=final),
        grid=(t // tm,),
        in_specs=[row(D_MODEL), row(POOL_WIDTH), row(MLA_WIDTH), row(DN_WIDTH), row(N_BRANCH * D_MODEL)]
                 + [_const_spec(w.shape) for w in (wa, wb, wc, wo, final_g)],
        out_specs=row(D_MODEL),
        out_shape=jax.ShapeDtypeStruct((t, D_MODEL), F32),
        compiler_params=_cparams("parallel"),
        name="merge_out",
    )(x2d, ya, yb, yc, h_gate, wa, wb, wc, wo, final_g)


def _layer_weights(l, norm_g, wt_all, pool_mix, pool_scale, q_norm_g, w_uq, kv_norm_g, w_uk, w_uv,
                   conv_w, a_log, dt_bias, dn_norm_g, w_br_pool, w_br_mla, w_br_dn, w_out):
    o = IN_OFFSETS
    kr_rows = wt_all[l, o[KR_SPLIT]:o[KR_SPLIT + 1]]
    ba_rows = wt_all[l, o[BETA_SPLIT]:o[ALPHA_SPLIT + 1]]
    gap = jnp.zeros((ROPE_LANE0 - 2 * DN_HEADS, wt_all.shape[-1]), wt_all.dtype)
    tail = jnp.zeros((LANES - ROPE_LANE0 - QK_ROPE, wt_all.shape[-1]), wt_all.dtype)
    small_blk = jnp.concatenate([ba_rows, gap, kr_rows, tail], axis=0)

    dq = QK_NOPE + QK_ROPE
    wuq = w_uq[l].reshape(Q_LORA, MLA_HEADS, dq)
    wuq_p = jnp.pad(wuq, ((0, 0), (0, 0), (0, HEAD_BLOCK - dq))).reshape(Q_LORA, -1).astype(BF16)
    wuk_p = jnp.pad(w_uk[l], ((0, 0), (0, 0), (0, HEAD_BLOCK - QK_NOPE))).reshape(KV_LORA, -1).astype(BF16)
    wuv_f = w_uv[l].reshape(KV_LORA, MLA_WIDTH).astype(BF16)
    wuv_e = jnp.pad(w_uv[l][:, 0::2], ((0, 0), (0, 0), (0, HEAD_BLOCK - V_HEAD)))
    wuv_o = jnp.pad(w_uv[l][:, 1::2], ((0, 0), (0, 0), (HEAD_BLOCK - V_HEAD, 0)))
    wuv_p = jnp.stack([wuv_e, wuv_o], axis=2).reshape(KV_LORA, -1).astype(BF16)
    wcat = jnp.zeros((MLA_HEADS, HEAD_BLOCK, KV_LORA + LANES), F32)
    wcat = wcat.at[:, 0:QK_NOPE, 0:KV_LORA].set(jnp.transpose(w_uk[l], (1, 2, 0)))
    sel = jnp.eye(QK_ROPE, dtype=F32)
    wcat = wcat.at[:, ROPE_LANE0:ROPE_LANE0 + QK_ROPE, KV_LORA:KV_LORA + QK_ROPE].set(sel)
    lanes16 = slice(DN_HEADS, 2 * DN_HEADS)
    return dict(
        norm_g=norm_g[l][None, :], small_blk=small_blk,
        pool_mix=pool_mix[l].astype(BF16), pool_scale=pool_scale[l][None, :],
        q_norm_g=q_norm_g[l][None, :], wuq_p=wuq_p, kv_norm_g=kv_norm_g[l][None, :],
        wuk_p=wuk_p, wuv_f=wuv_f, wuv_p=wuv_p, wcat=wcat.astype(BF16),
        cw8=jnp.pad(conv_w[l], ((0, SUBLANES - CONV_W), (0, 0))),
        alog_row=jnp.zeros((1, LANES), F32).at[0, lanes16].set(a_log[l]),
        dtb_row=jnp.zeros((1, LANES), F32).at[0, lanes16].set(dt_bias[l]),
        gdn_row=jnp.tile(dn_norm_g[l], DN_HEADS)[None, :],
        wa=w_br_pool[l].astype(BF16), wb=w_br_mla[l].astype(BF16), wc=w_br_dn[l].astype(BF16),
        wo=w_out[l].astype(BF16))


def _tile(n, target):
    t = min(n, target)
    while n % t:
        t -= SUBLANES
    return t


def _group_layer(x2d, b, l, start, w, wt_all, layer, tabs, final_g, final, pool_buf, conv_buf, sp0, paged,
                 prev_kv=()):
    t = b * l
    small = l < 2 * SUBLANES
    y_dtype = F32 if small else BF16
    h_pool, z_pool, h_q, h_kv, h_small, z_mla, h_qkv, z_dn, h_gate = _inproj(
        x2d, w["norm_g"], wt_all, layer, w["small_blk"], _tile(t, 256))
    h_kr = h_ba = h_small

    ya, pool_st = _pool(h_pool, z_pool, pool_buf, w["pool_mix"], w["pool_scale"], b, l, _tile(l, 512), start, y_dtype)

    tm = _tile(l, 512)
    if paged is None:
        ckv, k_r, q, k, v = _mla_prep(h_q, h_kv, h_kr, tabs, w["q_norm_g"], w["wuq_p"], w["kv_norm_g"],
                                         (w["wuk_p"], w["wuv_p"]), b, l, tm, absorbed=False, prev=prev_kv)
        yb = _attn_prompt(q, k, v, z_mla, b, l, _tile(l, 512))
    else:
        page_table, cache_kv, cache_krt = paged
        tabs_all = [jnp.tile(tab, (b, 1)) for tab in tabs]
        ckv, k_r, qcat = _mla_prep(h_q, h_kv, h_kr, tabs_all, w["q_norm_g"], w["wuq_p"], w["kv_norm_g"],
                                      (w["wcat"],), 1, t, t, absorbed=True)
        ckv, k_r = ckv[0], k_r[0]
        wq = qcat.shape[-1] // MLA_HEADS
        qcat = qcat.reshape(b, l, MLA_HEADS, wq).transpose(0, 2, 1, 3).reshape(b, MLA_HEADS * l, wq)
        n_pages = page_table.shape[1]
        n_grp = PAGES_PER_GROUP
        while n_pages % (2 * n_grp):
            n_grp //= 2
        yb = _attn_paged(page_table, qcat, cache_kv, cache_krt, layer, ckv.reshape(b, l, KV_LORA),
                         k_r.reshape(b, l, QK_ROPE), w["wuv_f"], z_mla.reshape(b, l, MLA_WIDTH), n_grp)
        yb = yb.reshape(t, MLA_WIDTH)

    if l >= DN_CHUNK:
        tl, ns = _tile(l, 256), 1
    else:
        tl, ns = l, SAMPLE_SEQS_PER_STEP
        while b % ns:
            ns //= 2
    sp0_all, sp0_layer = sp0
    yc, conv_st, sp = _delta(h_qkv, z_dn, h_ba, conv_buf, w["cw8"], w["alog_row"], w["dtb_row"], sp0_all, sp0_layer,
                             w["gdn_row"], b, l, tl, ns, y_dtype)

    x_out = _merge(x2d, ya, yb, yc, h_gate, w["wa"], w["wb"], w["wc"], w["wo"], final_g, _tile(t, 512), final)
    return x_out, ckv, k_r, pool_st[:, 1:], conv_st[:, CONV_HALO - (CONV_W - 1):], sp


def kernel(x_prompt, x_sample, cache_kv_latent, cache_k_rope, state_pool, state_conv, state_delta,
           page_table, norm_g, w_in, pool_mix, pool_scale, q_norm_g, w_uq, kv_norm_g, w_uk, w_uv,
           conv_w, a_log, dt_bias, dn_norm_g, w_br_pool, w_br_mla, w_br_dn, w_out, final_norm_g):
    bp, lp, d = x_prompt.shape
    db, ls, _ = x_sample.shape
    depth = w_in.shape[0]
    past_len = page_table.shape[1] * PAGE_SIZE
    final_g = final_norm_g[None, :]
    cache_krt = jnp.swapaxes(cache_k_rope, 2, 3)
    wt_all = jnp.swapaxes(w_in, 1, 2).astype(BF16)

    tabs_p = _rope_tables(lp, _tile(lp, 512), 0)
    tabs_s = _rope_tables(ls, ls, past_len)
    zero_pool = jnp.zeros((bp, POOL_HALO, POOL_WIDTH), F32)
    zero_conv = jnp.zeros((bp, CONV_HALO, CONV_CH), F32)
    zero_sp = (jnp.zeros((1, bp, DN_HEADS, DN_DK, DN_DV), F32), 0)

    xp = x_prompt.reshape(bp * lp, d)
    xs = x_sample.reshape(db * ls, d)
    outs_p, outs_s = [], []
    kv_stack = ()
    for l in range(depth):
        w = _layer_weights(l, norm_g, wt_all, pool_mix, pool_scale, q_norm_g, w_uq, kv_norm_g, w_uk, w_uv,
                           conv_w, a_log, dt_bias, dn_norm_g, w_br_pool, w_br_mla, w_br_dn, w_out)
        final = l == depth - 1
        xp, ckv, kr, pst, cst, sp = _group_layer(xp, bp, lp, 0, w, wt_all, l, tabs_p, final_g, final,
                                                 zero_pool, zero_conv, zero_sp, None, kv_stack)
        kv_stack = (ckv, kr)
        outs_p.append((pst, cst, sp))
        pool_buf = jnp.pad(state_pool[l], ((0, 0), (POOL_HALO - POOL_BUF, 0), (0, 0)))
        conv_buf = jnp.pad(state_conv[l], ((0, 0), (CONV_HALO - (CONV_W - 1), 0), (0, 0)))
        xs, ckv, kr, pst, cst, sp = _group_layer(xs, db, ls, past_len, w, wt_all, l, tabs_s, final_g, final,
                                                 pool_buf, conv_buf, (state_delta, l),
                                                 (page_table, cache_kv_latent, cache_krt))
        outs_s.append((ckv.reshape(db, ls, KV_LORA), kr.reshape(db, ls, QK_ROPE), pst, cst, sp))
    stack = lambda outs, i: jnp.stack([o[i] for o in outs])
    p_kv = kv_stack[0].reshape(depth, bp, lp, KV_LORA)
    p_kr = jnp.swapaxes(kv_stack[1], 2, 3)
    return (xp.reshape(bp, lp, d), xs.reshape(db, ls, d),
            p_kv, p_kr, *(stack(outs_p, i) for i in range(3)), *(stack(outs_s, i) for i in range(5)))
```

```python
import functools

import numpy as np
import jax
import jax.numpy as jnp
from jax import lax
from jax.experimental import pallas as pl
from jax.experimental.pallas import tpu as pltpu

F32 = jnp.float32
BF16 = jnp.bfloat16

D_MODEL = 1024
EPS = 1e-6
POOL_WINDOWS = (2, 4, 8, 16)
POOL_GROUP_DIM = 128
POOL_WIDTH = 512
POOL_BUF = 15
MLA_HEADS = 8
QK_NOPE = 64
QK_ROPE = 32
V_HEAD = 64
Q_LORA = 384
KV_LORA = 256
MLA_WIDTH = 512
MLA_SCALE = (QK_NOPE + QK_ROPE) ** -0.5
ROPE_THETA = 10000.0
DN_HEADS = 8
DN_DK = 64
DN_DV = 64
DN_WIDTH = 512
CONV_W = 4
CONV_CH = 1536
DN_CHUNK = 64
PAGE_SIZE = 128
N_BRANCH = 3

LANES = 128
SUBLANES = 8
HEAD_BLOCK = LANES
ROPE_LANE0 = QK_NOPE
ROPE_HALF = QK_ROPE // 2
POOL_HALO = 16
CONV_HALO = 8
VMEM_LIMIT = 48 * 1024 * 1024
NEG = -1e30
LOG2E = 1.4426950408889634
Q_SCALE = MLA_SCALE * LOG2E
PAGES_PER_GROUP = 64
PAGED_SLOTS = 3
PAGED_SUBBLOCKS = 16
PAGED_ROW_CHUNK = 16


def _cparams(*sem):
    return pltpu.CompilerParams(dimension_semantics=sem, vmem_limit_bytes=VMEM_LIMIT)


def _const_spec(shape):
    nd = len(shape)
    return pl.BlockSpec(shape, lambda *_: (0,) * nd, pipeline_mode=pl.Buffered(1))


def _dot(a, b):
    return jnp.dot(a, b, preferred_element_type=F32)


def _dot_nt(a, b):
    return lax.dot_general(a, b, (((1,), (1,)), ((), ())), preferred_element_type=F32)


def _silu(x):
    return x * jax.nn.sigmoid(x)


def _rms(x, g):
    return x * lax.rsqrt(jnp.mean(x * x, axis=-1, keepdims=True) + EPS) * g


IN_SPLITS = (POOL_WIDTH, POOL_WIDTH, Q_LORA, KV_LORA, QK_ROPE, MLA_WIDTH, CONV_CH, DN_WIDTH,
             DN_HEADS, DN_HEADS, N_BRANCH * D_MODEL)
IN_OFFSETS = tuple(int(v) for v in np.cumsum((0,) + IN_SPLITS))
IN_WIDTH = IN_OFFSETS[-1]
KR_SPLIT, BETA_SPLIT, ALPHA_SPLIT = 4, 8, 9
IN_OUTPUTS = ((IN_OFFSETS[0], POOL_WIDTH), (IN_OFFSETS[1], POOL_WIDTH), (IN_OFFSETS[2], Q_LORA),
              (IN_OFFSETS[3], KV_LORA), (None, LANES), (IN_OFFSETS[5], MLA_WIDTH),
              (IN_OFFSETS[6], CONV_CH), (IN_OFFSETS[7], DN_WIDTH),
              (IN_OFFSETS[10], N_BRANCH * D_MODEL))
IN_DOT_CHUNK = 512


def _inproj_kernel(x_ref, g_ref, w_ref, small_ref, *o_refs):
    xn = _rms(x_ref[...], g_ref[...]).astype(BF16)
    for o_ref, (row0, width) in zip(o_refs, IN_OUTPUTS):
        if row0 is None:
            o_ref[...] = _dot_nt(xn, small_ref[...])
            continue
        for c0 in range(0, width, IN_DOT_CHUNK):
            c1 = min(c0 + IN_DOT_CHUNK, width)
            o_ref[:, c0:c1] = _dot_nt(xn, w_ref[row0 + c0:row0 + c1, :])


def _inproj(x2d, norm_g, wt_all, layer, small_blk, tm):
    t = x2d.shape[0]
    row = lambda w: pl.BlockSpec((tm, w), lambda i: (i, 0))
    w_spec = pl.BlockSpec((None,) + wt_all.shape[1:], lambda i: (layer, 0, 0), pipeline_mode=pl.Buffered(1))
    return pl.pallas_call(
        _inproj_kernel,
        grid=(t // tm,),
        in_specs=[row(D_MODEL), _const_spec((1, D_MODEL)), w_spec, _const_spec(small_blk.shape)],
        out_specs=[row(w) for _, w in IN_OUTPUTS],
        out_shape=[jax.ShapeDtypeStruct((t, w), F32) for _, w in IN_OUTPUTS],
        compiler_params=_cparams("parallel"),
        name="inproj",
    )(x2d, norm_g, wt_all, small_blk)


def _pool_kernel(u_ref, z_ref, buf_ref, mix_ref, scale_ref, y_ref, st_ref, xx_ref, *, tl, start, n_l):
    li = pl.program_id(1)

    @pl.when(li == 0)
    def _():
        xx_ref[0:POOL_HALO, :] = buf_ref[...]

    u = u_ref[...]
    xx_ref[POOL_HALO:POOL_HALO + tl, :] = u
    row = lax.broadcasted_iota(jnp.int32, (tl, 1), 0)
    pos1 = start + li * tl + row + 1
    for gi, w in enumerate(POOL_WINDOWS):
        lanes = slice(POOL_GROUP_DIM * gi, POOL_GROUP_DIM * (gi + 1))
        s = xx_ref[:, lanes]
        k = 1
        while k < w:
            s = s + pltpu.roll(s, k, 0)
            k *= 2
        s = s[POOL_HALO:POOL_HALO + tl]
        cnt = jnp.minimum(pos1, w).astype(F32)
        d = s / cnt - u[:, lanes]
        y = _dot(d.astype(BF16), mix_ref[gi]) * scale_ref[:, lanes]
        y_ref[:, lanes] = (y * _silu(z_ref[:, lanes])).astype(y_ref.dtype)

    tail = xx_ref[tl:tl + POOL_HALO, :]

    @pl.when(li == n_l - 1)
    def _():
        st_ref[...] = tail

    xx_ref[0:POOL_HALO, :] = tail


def _pool(h_pool, z_pool, buf16, mix, scale, b, l, tl, start, y_dtype):
    n_l = l // tl
    row = lambda w: pl.BlockSpec((tl, w), lambda bi, li: (bi * n_l + li, 0))
    st = pl.BlockSpec((None, POOL_HALO, POOL_WIDTH), lambda bi, li: (bi, 0, 0))
    return pl.pallas_call(
        functools.partial(_pool_kernel, tl=tl, start=start, n_l=n_l),
        grid=(b, n_l),
        in_specs=[row(POOL_WIDTH), row(POOL_WIDTH), st, _const_spec(mix.shape), _const_spec(scale.shape)],
        out_specs=[row(POOL_WIDTH), st],
        out_shape=[jax.ShapeDtypeStruct((b * l, POOL_WIDTH), y_dtype),
                   jax.ShapeDtypeStruct((b, POOL_HALO, POOL_WIDTH), F32)],
        scratch_shapes=[pltpu.VMEM((POOL_HALO + tl, POOL_WIDTH), F32)],
        compiler_params=_cparams("parallel", "arbitrary"),
        name="pool_mixer",
    )(h_pool, z_pool, buf16, mix, scale)


def _rope_table_kernel(inv_ref, c_ref, s1_ref, s2_ref, *, tl, start):
    i = pl.program_id(0)
    shape = (tl, LANES)
    pos = (start + i * tl + lax.broadcasted_iota(jnp.int32, shape, 0)).astype(F32)
    lane = lax.broadcasted_iota(jnp.int32, shape, 1)
    ang = pos * inv_ref[...]
    cos, sin = jnp.cos(ang), jnp.sin(ang)
    first = (lane >= ROPE_LANE0) & (lane < ROPE_LANE0 + ROPE_HALF)
    second = (lane >= ROPE_LANE0 + ROPE_HALF) & (lane < ROPE_LANE0 + QK_ROPE)
    c_ref[...] = jnp.where(first | second, cos, 1.0)
    s1_ref[...] = jnp.where(first, -sin, 0.0)
    s2_ref[...] = jnp.where(second, sin, 0.0)


def _rope_tables(l, tl, start):
    half = ROPE_HALF
    inv = jnp.power(ROPE_THETA, -jnp.arange(half, dtype=F32) / half)
    inv_lane = jnp.zeros((1, LANES), F32)
    inv_lane = inv_lane.at[0, ROPE_LANE0:ROPE_LANE0 + half].set(inv)
    inv_lane = inv_lane.at[0, ROPE_LANE0 + half:ROPE_LANE0 + QK_ROPE].set(inv)
    blk = pl.BlockSpec((tl, LANES), lambda i: (i, 0))
    return pl.pallas_call(
        functools.partial(_rope_table_kernel, tl=tl, start=start),
        grid=(l // tl,),
        in_specs=[_const_spec((1, LANES))],
        out_specs=[blk, blk, blk],
        out_shape=[jax.ShapeDtypeStruct((l, LANES), F32)] * 3,
        compiler_params=_cparams("parallel"),
        name="rope_tables",
    )(inv_lane)


def _rope(x, c, s1, s2):
    return x * c + pltpu.roll(x, LANES - ROPE_HALF, 1) * s1 + pltpu.roll(x, ROPE_HALF, 1) * s2


def _mla_prep_kernel(hq_ref, hkv_ref, hkr_ref, c_ref, s1_ref, s2_ref, qg_ref, wuq_ref, kvg_ref, *rest,
                     absorbed, n_prev):
    c, s1, s2 = c_ref[...], s1_ref[...], s2_ref[...]
    cq = _rms(hq_ref[...], qg_ref[...]).astype(BF16)
    q = _dot(cq, wuq_ref[...])
    ckv = _rms(hkv_ref[...], kvg_ref[...])
    lane128 = lax.broadcasted_iota(jnp.int32, c.shape, 1)
    on_rope = (lane128 >= ROPE_LANE0) & (lane128 < ROPE_LANE0 + QK_ROPE)
    kr = jnp.where(on_rope, _rope(hkr_ref[...], c, s1, s2), 0.0)
    if n_prev:
        pckv_ref, pkr_ref, *rest = rest
    if absorbed:
        wcat_ref, ckv_ref, kr_ref, qcat_ref = rest
    else:
        wuk_ref, wuv_ref, ckv_ref, kr_ref, q_ref, k_ref, v_ref = rest
        ckv16 = ckv.astype(BF16)
        knope = _dot(ckv16, wuk_ref[...])
        lane = lax.broadcasted_iota(jnp.int32, (1, MLA_HEADS * HEAD_BLOCK), 1)
        odd = (lane // HEAD_BLOCK) % 2
        ones_col = (lane % HEAD_BLOCK == V_HEAD * (1 - odd)).astype(F32)
        v_ref[...] = (_dot(ckv16, wuv_ref[...]) + ones_col).astype(BF16)
    ckv_ref[n_prev] = ckv
    if absorbed:
        kr_ref[n_prev] = pltpu.roll(kr, LANES - ROPE_LANE0, 1)[:, 0:QK_ROPE]
    else:
        kr_ref[n_prev] = kr.T[ROPE_LANE0:ROPE_LANE0 + QK_ROPE, :]
    if n_prev:
        ckv_ref[0:n_prev] = pckv_ref[...]
        kr_ref[0:n_prev] = pkr_ref[...]
    for h in range(MLA_HEADS):
        blk = slice(HEAD_BLOCK * h, HEAD_BLOCK * (h + 1))
        qh = (_rope(q[:, blk], c, s1, s2) * Q_SCALE).astype(BF16)
        if absorbed:
            w = wcat_ref.shape[-1]
            qcat_ref[:, w * h:w * (h + 1)] = _dot(qh, wcat_ref[h]).astype(BF16)
        else:
            q_ref[:, blk] = qh
            k_ref[:, blk] = (knope[:, blk] + kr).astype(BF16)


def _mla_prep(h_q, h_kv, h_kr, tabs, q_norm_g, wuq_p, kv_norm_g, extra_w, b, l, tm, absorbed, prev=()):
    n_l = l // tm
    t = b * l
    n_prev = prev[0].shape[0] if prev else 0
    row = lambda w: pl.BlockSpec((tm, w), lambda bi, li: (bi * n_l + li, 0))
    stk = lambda n, w: pl.BlockSpec((n, tm, w), lambda bi, li: (0, bi * n_l + li, 0))
    tab = pl.BlockSpec((tm, LANES), lambda bi, li: (li, 0))
    hw = MLA_HEADS * HEAD_BLOCK
    in_specs = [row(Q_LORA), row(KV_LORA), row(HEAD_BLOCK), tab, tab, tab,
                _const_spec(q_norm_g.shape), _const_spec(wuq_p.shape), _const_spec(kv_norm_g.shape)]
    if absorbed:
        kr_spec = lambda n: stk(n, QK_ROPE)
        kr_shape = (n_prev + 1, t, QK_ROPE)
    else:
        kr_spec = lambda n: pl.BlockSpec((n, None, QK_ROPE, tm), lambda bi, li: (0, bi, 0, li))
        kr_shape = (n_prev + 1, b, QK_ROPE, l)
    if n_prev:
        in_specs += [stk(n_prev, KV_LORA), kr_spec(n_prev)]
    in_specs += [_const_spec(w.shape) for w in extra_w]
    out_specs = [stk(n_prev + 1, KV_LORA), kr_spec(n_prev + 1)]
    out_shape = [jax.ShapeDtypeStruct((n_prev + 1, t, KV_LORA), F32), jax.ShapeDtypeStruct(kr_shape, F32)]
    if absorbed:
        wc = MLA_HEADS * extra_w[0].shape[-1]
        out_specs += [row(wc)]
        out_shape += [jax.ShapeDtypeStruct((t, wc), BF16)]
    else:
        out_specs += [row(hw), row(hw), row(hw)]
        out_shape += [jax.ShapeDtypeStruct((t, hw), BF16)] * 3
    return pl.pallas_call(
        functools.partial(_mla_prep_kernel, absorbed=absorbed, n_prev=n_prev),
        grid=(b, n_l),
        in_specs=in_specs, out_specs=out_specs, out_shape=out_shape,
        compiler_params=_cparams("parallel", "parallel"),
        name="mla_prep_absorbed" if absorbed else "mla_prep",
    )(h_q, h_kv, h_kr, *tabs, q_norm_g, wuq_p, kv_norm_g, *prev, *extra_w)


ATTN_ROW_CHUNK = 32


def _softmax_update(m_b, l_p, s, row_chunk):
    r, n = s.shape[0], s.shape[1] // LANES
    cols = [slice(LANES * c, LANES * (c + 1)) for c in range(n)]
    folded = s[:, cols[0]]
    for cb in cols[1:]:
        folded = jnp.maximum(folded, s[:, cb])
    mx = jnp.max(folded, axis=-1, keepdims=True)
    m_new = jnp.maximum(m_b, jnp.broadcast_to(mx, m_b.shape))
    a_b = jnp.exp2(m_b - m_new)
    l_rows, p_rows = [], []
    for c0 in range(0, r, row_chunk):
        rows = slice(c0, c0 + row_chunk)
        m_c = m_new[rows]
        pieces = [jnp.exp2(s[rows, cb] - m_c) for cb in cols]
        if l_p is not None:
            tot = pieces[0]
            for pc in pieces[1:]:
                tot = tot + pc
            l_rows.append(a_b[rows] * l_p[rows] + tot)
        p_rows.append(jnp.concatenate([pc.astype(BF16) for pc in pieces], axis=1))
    cat = lambda parts: parts[0] if len(parts) == 1 else jnp.concatenate(parts, axis=0)
    return m_new, (None if l_p is None else cat(l_rows)), a_b, cat(p_rows)


def _attn_kernel(q_ref, k_ref, v_ref, z_ref, y_ref, *, ta):
    qi = pl.program_id(1)
    rc = min(ATTN_ROW_CHUNK, ta)
    low_half = lax.broadcasted_iota(jnp.int32, (ta, LANES), 1) < V_HEAD
    lane_id = lax.broadcasted_iota(jnp.int32, (ta, LANES), 1)
    causal = (lax.broadcasted_iota(jnp.int32, (ta, ta), 1)
              <= lax.broadcasted_iota(jnp.int32, (ta, ta), 0))
    for p in range(MLA_HEADS // 2):
        pair = slice(LANES * p, LANES * (p + 1))
        blks = [slice(HEAD_BLOCK * h, HEAD_BLOCK * (h + 1)) for h in (2 * p, 2 * p + 1)]
        qhs = [q_ref[:, blk] for blk in blks]

        def step(kb, carry, masked, pair=pair, blks=blks, qhs=qhs):
            r0 = pl.multiple_of(kb * ta, ta)
            scores = [_dot_nt(qh, k_ref[pl.ds(r0, ta), blk]) for qh, blk in zip(qhs, blks)]
            out = []
            for (m_b, acc), s, blk in zip(carry, scores, blks):
                if masked:
                    s = jnp.where(causal, s, NEG)
                m_b, _, a_b, pm = _softmax_update(m_b, None, s, rc)
                out.append((m_b, a_b * acc + _dot(pm, v_ref[pl.ds(r0, ta), blk])))
            return tuple(out)

        init = (jnp.full((ta, LANES), NEG, F32), jnp.zeros((ta, LANES), F32))
        carry = lax.fori_loop(0, qi, functools.partial(step, masked=False), (init, init))
        (_, acc0), (_, acc1) = step(qi, carry, True)
        den = lambda acc, at: jnp.sum(jnp.where(lane_id == at, acc, 0.0), axis=-1, keepdims=True)
        o = jnp.where(low_half, acc0 / den(acc0, V_HEAD), acc1 / den(acc1, 0))
        y_ref[:, pair] = (o * _silu(z_ref[:, pair])).astype(y_ref.dtype)


def _attn_prompt(q, k, v, z_mla, b, l, ta):
    n_q = l // ta
    hw = MLA_HEADS * HEAD_BLOCK
    row = lambda w: pl.BlockSpec((ta, w), lambda bi, qi: (bi * n_q + qi, 0))
    seq = lambda w: pl.BlockSpec((l, w), lambda bi, qi: (bi, 0))
    return pl.pallas_call(
        functools.partial(_attn_kernel, ta=ta),
        grid=(b, n_q),
        in_specs=[row(hw), seq(hw), seq(hw), row(MLA_WIDTH)],
        out_specs=row(MLA_WIDTH),
        out_shape=jax.ShapeDtypeStruct((b * l, MLA_WIDTH), BF16),
        compiler_params=_cparams("parallel", "arbitrary"),
        name="mla_attention_prompt",
    )(q, k, v, z_mla)


def _attn_paged_kernel(pt_ref, qcat_ref, nkv_ref, nkr_ref, wuv_ref, z_ref, kv_hbm, krt_hbm, y_ref,
                       kvbuf, krbuf, sem, *, layer, n_pages, n_grp, l_new):
    b = pl.program_id(0)
    n_seq = pl.num_programs(0)
    n_groups = n_pages // n_grp
    rows = MLA_HEADS * l_new
    ahead = PAGED_SLOTS - 1

    sub_pages = n_grp // PAGED_SUBBLOCKS

    def group_copies(seq, j, pages):
        slot = (seq * n_groups + j) % PAGED_SLOTS
        cps = []
        for g in pages:
            page = pt_ref[seq * n_pages + j * n_grp + g]
            keys = pl.ds(PAGE_SIZE * g, PAGE_SIZE)
            cps.append(pltpu.make_async_copy(kv_hbm.at[layer, page], kvbuf.at[slot, keys, :], sem.at[0, slot]))
            cps.append(pltpu.make_async_copy(krt_hbm.at[layer, page], krbuf.at[slot, g], sem.at[1, slot]))
        return cps

    def start_pages(seq, j, pages):
        for cp in group_copies(seq, j, pages):
            cp.start()

    def wait_group(seq, j):
        for cp in group_copies(seq, j, range(n_grp)):
            cp.wait()

    def start_ahead(j, sb):
        nxt = j + ahead
        pages = range(sub_pages * sb, sub_pages * (sb + 1))
        if nxt < n_groups:
            start_pages(b, nxt, pages)
        else:
            @pl.when(b + 1 < n_seq)
            def _():
                start_pages(b + 1, nxt - n_groups, pages)

    @pl.when(b == 0)
    def _():
        for j0 in range(ahead):
            start_pages(b, j0, range(n_grp))

    qc = qcat_ref[...]
    ql, qr = qc[:, :KV_LORA], qc[:, KV_LORA:KV_LORA + QK_ROPE]

    def update(state, s, kv16):
        m_b, l_p, acc = state
        m_b, l_p, a_b, pm = _softmax_update(m_b, l_p, s, PAGED_ROW_CHUNK)
        a_wide = jnp.concatenate([a_b] * (KV_LORA // LANES), axis=1)
        return m_b, l_p, a_wide * acc + _dot(pm, kv16)

    state = (jnp.full((rows, LANES), NEG, F32), jnp.zeros((rows, LANES), F32), jnp.zeros((rows, KV_LORA), F32))
    sub_keys = n_grp * PAGE_SIZE // PAGED_SUBBLOCKS
    for j in range(n_groups):
        wait_group(b, j)
        slot = (b * n_groups + j) % PAGED_SLOTS
        subs = []
        for sb in range(PAGED_SUBBLOCKS):
            keys = slice(sub_keys * sb, sub_keys * (sb + 1))
            kv16 = kvbuf[slot, keys, :].astype(BF16)
            kr16 = jnp.concatenate([krbuf[slot, g] for g in range(sub_pages * sb, sub_pages * (sb + 1))],
                                   axis=1).astype(BF16)
            subs.append((_dot_nt(ql, kv16) + _dot(qr, kr16), kv16))
            start_ahead(j, sb)
        for s, kv16 in subs:
            state = update(state, s, kv16)

    pad = PAGE_SIZE - l_new
    nkv = jnp.concatenate([nkv_ref[...], jnp.zeros((pad, KV_LORA), F32)], axis=0).astype(BF16)
    nkr = jnp.concatenate([nkr_ref[...], jnp.zeros((pad, QK_ROPE), F32)], axis=0).astype(BF16)
    tok = lax.broadcasted_iota(jnp.int32, (rows, PAGE_SIZE), 0) % l_new
    key = lax.broadcasted_iota(jnp.int32, (rows, PAGE_SIZE), 1)
    _, l_p, acc = update(state, jnp.where(key <= tok, _dot_nt(ql, nkv) + _dot_nt(qr, nkr), NEG), nkv)
    o = (acc / jnp.sum(l_p, axis=-1, keepdims=True)).astype(BF16)
    full = _dot(o, wuv_ref[...])
    rh = lax.broadcasted_iota(jnp.int32, full.shape, 0) // l_new
    ch = lax.broadcasted_iota(jnp.int32, full.shape, 1) // V_HEAD
    full = jnp.where(rh == ch, full, 0.0)
    out = full[0:l_new]
    for h in range(1, MLA_HEADS):
        out = out + full[l_new * h:l_new * (h + 1)]
    y_ref[...] = (out * _silu(z_ref[...])).astype(y_ref.dtype)


def _attn_paged(page_table, qcat, cache_kv, cache_krt, layer, new_kv, new_kr, wuv, z_mla, n_grp):
    db, n_pages = page_table.shape
    l_new = new_kv.shape[1]
    rows = MLA_HEADS * l_new
    assert n_pages % n_grp == 0 and n_pages // n_grp >= PAGED_SLOTS - 1 and n_grp % PAGED_SUBBLOCKS == 0
    wq = qcat.shape[-1]
    per_b = lambda r, w: pl.BlockSpec((None, r, w), lambda b, pt: (b, 0, 0))
    hbm = pl.BlockSpec(memory_space=pl.ANY)
    grid_spec = pltpu.PrefetchScalarGridSpec(
        num_scalar_prefetch=1, grid=(db,),
        in_specs=[per_b(rows, wq), per_b(l_new, KV_LORA), per_b(l_new, QK_ROPE),
                  pl.BlockSpec(wuv.shape, lambda b, pt: (0, 0)), per_b(l_new, MLA_WIDTH), hbm, hbm],
        out_specs=per_b(l_new, MLA_WIDTH),
        scratch_shapes=[pltpu.VMEM((PAGED_SLOTS, n_grp * PAGE_SIZE, KV_LORA), F32),
                        pltpu.VMEM((PAGED_SLOTS, n_grp, QK_ROPE, PAGE_SIZE), F32),
                        pltpu.SemaphoreType.DMA((2, PAGED_SLOTS))])
    return pl.pallas_call(
        functools.partial(_attn_paged_kernel, layer=layer, n_pages=n_pages, n_grp=n_grp, l_new=l_new),
        grid_spec=grid_spec,
        out_shape=jax.ShapeDtypeStruct((db, l_new, MLA_WIDTH), F32),
        compiler_params=_cparams("arbitrary"),
        name="mla_attention_paged",
    )(page_table.reshape(-1), qcat, new_kv, new_kr, wuv, z_mla, cache_kv, cache_krt)


PAIR = 2 * DN_CHUNK
N_PAIRS = DN_HEADS // 2
SLOT_GROUP = 4
SAMPLE_SEQS_PER_STEP = 4


def _delta_consts(r):
    hid = np.arange(PAIR) // DN_DK
    bd = hid[:, None] == hid[None, :]
    rr = np.arange(r)
    tri = (rr[:, None] // DN_CHUNK == rr[None, :] // DN_CHUNK) & (rr[:, None] >= rr[None, :])
    return tuple(jnp.asarray(a.astype(np.float32), dtype=BF16) for a in (bd, tri))


def _split3(x):
    hi = x.astype(BF16)
    r1 = x - hi.astype(F32)
    mid = r1.astype(BF16)
    lo = (r1 - mid.astype(F32)).astype(BF16)
    return hi, mid, lo


def _sel_dot(sel, x):
    hi, mid, lo = _split3(x)
    return _dot(sel, hi) + _dot(sel, mid) + _dot(sel, lo)


def _head_sums(x2, bd):
    x16 = x2.astype(BF16)
    return jnp.concatenate([_dot(x16[:, LANES * p:LANES * (p + 1)], bd) for p in range(N_PAIRS)], axis=1)


def _delta_kernel(u_ref, z_ref, ba_ref, cbuf_ref, cw_ref, alog_ref, dtb_ref, sp0_ref, gdn_ref,
                  bd_ref, tri_ref,
                  y_ref, cst_ref, spo_ref,
                  xx_ref, sp_ref, q_s, k_s, kb_s, qe_s, vb_s, kbe_s, gcl_s, gf_s, u_s, w_s, o_s, qkd_s, kdt_s,
                  *, ns, tl, n_l):
    li = pl.program_id(1)
    cps = max(tl, DN_CHUNK) // DN_CHUNK
    neumann_steps = max(int(np.ceil(np.log2(min(tl, DN_CHUNK)))) - 1, 0)
    seq_rows = cps * DN_CHUNK
    n_slots = ns * cps

    @pl.when(li == 0)
    def _():
        xx_ref[:, 0:CONV_HALO, :] = cbuf_ref[...]
        zero = jnp.zeros((DN_DK, DN_DV), F32)
        for s in range(ns):
            for p in range(N_PAIRS):
                top = jnp.concatenate([sp0_ref[s, 2 * p], zero], axis=1)
                bot = jnp.concatenate([zero, sp0_ref[s, 2 * p + 1]], axis=1)
                sp_ref[s, p] = jnp.concatenate([top, bot], axis=0)

    def pad_rows(a):
        if tl == seq_rows:
            return a
        return jnp.concatenate([a, jnp.zeros((seq_rows - tl, a.shape[1]), a.dtype)], axis=0)

    def per_seq(a):
        if ns == 1:
            return pad_rows(a)
        return jnp.concatenate([pad_rows(a[s * tl:(s + 1) * tl]) for s in range(ns)], axis=0)

    pieces = []
    for s in range(ns):
        xx_ref[s, CONV_HALO:CONV_HALO + tl, :] = u_ref[s * tl:(s + 1) * tl, :]
        ext = xx_ref[s]
        acc = None
        for j in range(CONV_W):
            back = CONV_W - 1 - j
            shifted = ext if back == 0 else pltpu.roll(ext, back, 0)
            term = shifted[CONV_HALO:CONV_HALO + tl] * cw_ref[j:j + 1, :]
            acc = term if acc is None else acc + term
        pieces.append(pad_rows(_silu(acc)))
    qkv = pieces[0] if ns == 1 else jnp.concatenate(pieces, axis=0)
    tail = xx_ref[:, tl:tl + CONV_HALO, :]

    @pl.when(li == n_l - 1)
    def _():
        cst_ref[...] = tail

    xx_ref[:, 0:CONV_HALO, :] = tail

    bd = bd_ref[...]
    q = qkv[:, 0:DN_WIDTH]
    k = qkv[:, DN_WIDTH:2 * DN_WIDTH]
    v = qkv[:, 2 * DN_WIDTH:3 * DN_WIDTH]
    qn = q * lax.rsqrt(_head_sums(q * q, bd) + EPS) * (DN_DK ** -0.5)
    kn = k * lax.rsqrt(_head_sums(k * k, bd) + EPS)
    ba = ba_ref[...]
    lane = lax.broadcasted_iota(jnp.int32, ba.shape, 1)
    beta = jnp.where(lane < DN_HEADS, jax.nn.sigmoid(ba), 0.0)
    xa = ba + dtb_ref[...]
    softplus = jnp.maximum(xa, 0.0) + jnp.log1p(jnp.exp(-jnp.abs(xa)))
    g = jnp.where((lane >= DN_HEADS) & (lane < 2 * DN_HEADS), -jnp.exp(alog_ref[...]) * softplus, 0.0)
    beta, g = per_seq(beta), per_seq(g)
    gc = _sel_dot(tri_ref[...], g)
    n_rows = gc.shape[0]
    first_r = lax.broadcasted_iota(jnp.int32, (n_rows, LANES), 1) < DN_DK

    def head_lanes(a, lane0):
        full = [jnp.broadcast_to(a[:, lane0 + h:lane0 + h + 1], (n_rows, LANES)) for h in range(DN_HEADS)]
        pairs = [jnp.where(first_r, full[2 * p], full[2 * p + 1]) for p in range(N_PAIRS)]
        return jnp.concatenate(pairs, axis=1), full

    beta_l, _ = head_lanes(beta, 0)
    gc_l, gc_full = head_lanes(gc, DN_HEADS)
    egc = jnp.exp(gc_l)
    kb = kn * beta_l
    q_s[...] = qn
    k_s[...] = kn
    kb_s[...] = kb
    qe_s[...] = qn * egc
    vb_s[...] = v * beta_l
    kbe_s[...] = kb * egc
    gcl_s[...] = gc_l
    for h in range(DN_HEADS):
        gf_s[:, LANES * h:LANES * (h + 1)] = gc_full[h]

    ri = lax.broadcasted_iota(jnp.int32, (PAIR, PAIR), 0)
    ci = lax.broadcasted_iota(jnp.int32, (PAIR, PAIR), 1)
    same = (ri // DN_CHUNK) == (ci // DN_CHUNK)
    incl = same & (ri >= ci)
    strict = same & (ri > ci)
    eye = (ri == ci).astype(F32)
    first = lax.broadcasted_iota(jnp.int32, (DN_CHUNK, LANES), 1) < DN_DK

    def stack(a):
        return jnp.concatenate([jnp.where(first, a, 0.0), jnp.where(first, 0.0, a)], axis=0)

    def fold(a):
        return a[0:DN_CHUNK] + a[DN_CHUNK:PAIR]

    rows = lambda c: slice(DN_CHUNK * c, DN_CHUNK * (c + 1))
    lanes = lambda p: slice(LANES * p, LANES * (p + 1))

    def solve_slots(slots):
        chains = [(c, p) for c in slots for p in range(N_PAIRS)]
        kst = [stack(k_s[rows(c), lanes(p)]).astype(BF16) for c, p in chains]
        a_mat = [_dot_nt(stack(kb_s[rows(c), lanes(p)]).astype(BF16), ks) for (c, p), ks in zip(chains, kst)]
        qk = [_dot_nt(stack(q_s[rows(c), lanes(p)]).astype(BF16), ks) for (c, p), ks in zip(chains, kst)]
        dec = []
        for c, p in chains:
            g_col = jnp.concatenate([gf_s[rows(c), lanes(2 * p)], gf_s[rows(c), lanes(2 * p + 1)]], axis=0)
            diff = g_col - g_col.T
            dec.append(jnp.where(incl, jnp.exp(jnp.where(incl, diff, 0.0)), 0.0))
        pw = [jnp.where(strict, -(a * d), 0.0) for a, d in zip(a_mat, dec)]
        for (c, p), x, d in zip(chains, qk, dec):
            qkd_s[c * N_PAIRS + p] = (x * d).astype(BF16)
        t_inv = [eye + n for n in pw]
        for _ in range(neumann_steps):
            pw16 = [x.astype(BF16) for x in pw]
            pw = [_dot(x, x) for x in pw16]
            t_inv = [t + _dot(t.astype(BF16), x.astype(BF16)) for t, x in zip(t_inv, pw)]
        for (c, p), t in zip(chains, t_inv):
            rhs = jnp.concatenate([stack(vb_s[rows(c), lanes(p)]), stack(kbe_s[rows(c), lanes(p)])], axis=1)
            uw = fold(_dot(t.astype(BF16), rhs.astype(BF16)))
            u_s[rows(c), lanes(p)] = uw[:, 0:LANES]
            w_s[rows(c), lanes(p)] = uw[:, LANES:2 * LANES]
            gcl = gcl_s[rows(c), lanes(p)]
            k_dec = k_s[rows(c), lanes(p)] * jnp.exp(gcl[DN_CHUNK - 1:DN_CHUNK, :] - gcl)
            kdt_s[c * N_PAIRS + p] = k_dec.T.astype(BF16)

    for c0 in range(0, n_slots, SLOT_GROUP):
        solve_slots(range(c0, min(c0 + SLOT_GROUP, n_slots)))

    pairs = range(N_PAIRS)
    for c in range(n_slots):
        sq = c // cps
        sps = [sp_ref[sq, p] for p in pairs]
        wq = [jnp.concatenate([w_s[rows(c), lanes(p)], qe_s[rows(c), lanes(p)]], axis=0).astype(BF16)
              for p in pairs]
        res = [_dot(x, sp.astype(BF16)) for x, sp in zip(wq, sps)]
        v_new = [u_s[rows(c), lanes(p)] - res[p][0:DN_CHUNK] for p in pairs]
        intra = [fold(_dot(qkd_s[c * N_PAIRS + p], stack(v_new[p]).astype(BF16))) for p in pairs]
        upd = [_dot(kdt_s[c * N_PAIRS + p], v_new[p].astype(BF16)) for p in pairs]
        for p in pairs:
            o_s[rows(c), lanes(p)] = res[p][DN_CHUNK:PAIR] + intra[p]
            g_last = gcl_s[DN_CHUNK * (c + 1) - 1:DN_CHUNK * (c + 1), lanes(p)]
            sp_ref[sq, p] = sps[p] * jnp.exp(g_last) + jnp.where(same, upd[p], 0.0)

    o = o_s[...]
    y = o * lax.rsqrt(_head_sums(o * o, bd) * (1.0 / DN_DV) + EPS) * gdn_ref[...]
    for s in range(ns):
        zs = z_ref[s * tl:(s + 1) * tl, :]
        y_ref[s * tl:(s + 1) * tl, :] = (y[s * seq_rows:s * seq_rows + tl] * _silu(zs)).astype(y_ref.dtype)

    @pl.when(li == n_l - 1)
    def _():
        for s in range(ns):
            for p in range(N_PAIRS):
                sp = sp_ref[s, p]
                spo_ref[s, 2 * p] = sp[0:DN_DK, 0:DN_DV]
                spo_ref[s, 2 * p + 1] = pltpu.roll(sp[DN_DK:PAIR], DN_DV, 1)[:, 0:DN_DV]


def _delta(h_qkv, z_dn, h_ba, cbuf8, cw8, alog_row, dtb_row, sp0_all, sp0_layer, gdn_row, b, l, tl, ns, y_dtype):
    n_l = l // tl
    assert ns == 1 or n_l == 1
    n_slots = ns * (max(tl, DN_CHUNK) // DN_CHUNK)
    r = n_slots * DN_CHUNK
    consts = _delta_consts(r)
    row = lambda w: pl.BlockSpec((ns * tl, w), lambda bi, li: (bi * n_l + li, 0))
    cst = pl.BlockSpec((ns, CONV_HALO, CONV_CH), lambda bi, li: (bi, 0, 0))
    spb = pl.BlockSpec((ns, DN_HEADS, DN_DK, DN_DV), lambda bi, li: (bi, 0, 0, 0))
    sp_in = pl.BlockSpec((None, ns, DN_HEADS, DN_DK, DN_DV), lambda bi, li: (sp0_layer, bi, 0, 0, 0))
    wide = lambda n: pltpu.VMEM((r, n), F32)
    return pl.pallas_call(
        functools.partial(_delta_kernel, ns=ns, tl=tl, n_l=n_l),
        grid=(b // ns, n_l),
        in_specs=[row(CONV_CH), row(DN_WIDTH), row(LANES), cst, _const_spec(cw8.shape),
                  _const_spec(alog_row.shape), _const_spec(dtb_row.shape), sp_in, _const_spec(gdn_row.shape)]
                 + [_const_spec(c.shape) for c in consts],
        out_specs=[row(DN_WIDTH), cst, spb],
        out_shape=[jax.ShapeDtypeStruct((b * l, DN_WIDTH), y_dtype),
                   jax.ShapeDtypeStruct((b, CONV_HALO, CONV_CH), F32),
                   jax.ShapeDtypeStruct((b, DN_HEADS, DN_DK, DN_DV), F32)],
        scratch_shapes=[pltpu.VMEM((ns, CONV_HALO + tl, CONV_CH), F32), pltpu.VMEM((ns, N_PAIRS, PAIR, PAIR), F32)]
                       + [wide(DN_WIDTH)] * 7 + [wide(DN_HEADS * LANES)] + [wide(DN_WIDTH)] * 3
                       + [pltpu.VMEM((n_slots * N_PAIRS, PAIR, PAIR), BF16),
                          pltpu.VMEM((n_slots * N_PAIRS, PAIR, DN_CHUNK), BF16)],
        compiler_params=_cparams("parallel", "arbitrary"),
        name="gated_deltanet",
    )(h_qkv, z_dn, h_ba, cbuf8, cw8, alog_row, dtb_row, sp0_all, gdn_row, *consts)


GATE_SLOTS = 3


def _merge_kernel(x_ref, ya_ref, yb_ref, yc_ref, wa_ref, wb_ref, wc_ref, wo_ref, fg_ref, g_hbm, o_ref,
                  gbuf, gsem, *, final, tm, n_steps):
    i = pl.program_id(0)
    ahead = GATE_SLOTS - 1

    def gate_copy(step):
        r0 = pl.multiple_of(step * tm, tm)
        slot = step % GATE_SLOTS
        return pltpu.make_async_copy(g_hbm.at[pl.ds(r0, tm), :], gbuf.at[slot], gsem.at[slot])

    @pl.when(i == 0)
    def _():
        for s0 in range(min(ahead, n_steps)):
            gate_copy(s0).start()

    @pl.when(i + ahead < n_steps)
    def _():
        gate_copy(i + ahead).start()

    gate_copy(i).wait()
    slot = i % GATE_SLOTS
    merged = None
    for j, (y_ref, w_ref) in enumerate(((ya_ref, wa_ref), (yb_ref, wb_ref), (yc_ref, wc_ref))):
        br = _dot(y_ref[...].astype(BF16), w_ref[...])
        t = jax.nn.sigmoid(gbuf[slot, :, D_MODEL * j:D_MODEL * (j + 1)]) * br
        merged = t if merged is None else merged + t
    x = x_ref[...] + _dot(merged.astype(BF16), wo_ref[...])
    o_ref[...] = _rms(x, fg_ref[...]) if final else x


def _merge(x2d, ya, yb, yc, h_gate, wa, wb, wc, wo, final_g, tm, final):
    t = x2d.shape[0]
    n_steps = t // tm
    row = lambda w: pl.BlockSpec((tm, w), lambda i: (i, 0))
    return pl.pallas_call(
        functools.partial(_merge_kernel, final=final, tm=tm, n_steps=n_steps),
        grid=(n_steps,),
        in_specs=[row(D_MODEL), row(POOL_WIDTH), row(MLA_WIDTH), row(DN_WIDTH)]
                 + [_const_spec(w.shape) for w in (wa, wb, wc, wo, final_g)]
                 + [pl.BlockSpec(memory_space=pl.ANY)],
        out_specs=row(D_MODEL),
        out_shape=jax.ShapeDtypeStruct((t, D_MODEL), F32),
        scratch_shapes=[pltpu.VMEM((GATE_SLOTS, tm, N_BRANCH * D_MODEL), F32),
                        pltpu.SemaphoreType.DMA((GATE_SLOTS,))],
        compiler_params=_cparams("arbitrary"),
        name="merge_out",
    )(x2d, ya, yb, yc, wa, wb, wc, wo, final_g, h_gate)


def _layer_weights(l, norm_g, wt_all, pool_mix, pool_scale, q_norm_g, w_uq, kv_norm_g, w_uk, w_uv,
                   conv_w, a_log, dt_bias, dn_norm_g, w_br_pool, w_br_mla, w_br_dn, w_out):
    o = IN_OFFSETS
    kr_rows = wt_all[l, o[KR_SPLIT]:o[KR_SPLIT + 1]]
    ba_rows = wt_all[l, o[BETA_SPLIT]:o[ALPHA_SPLIT + 1]]
    gap = jnp.zeros((ROPE_LANE0 - 2 * DN_HEADS, wt_all.shape[-1]), wt_all.dtype)
    tail = jnp.zeros((LANES - ROPE_LANE0 - QK_ROPE, wt_all.shape[-1]), wt_all.dtype)
    small_blk = jnp.concatenate([ba_rows, gap, kr_rows, tail], axis=0)

    dq = QK_NOPE + QK_ROPE
    wuq = w_uq[l].reshape(Q_LORA, MLA_HEADS, dq)
    wuq_p = jnp.pad(wuq, ((0, 0), (0, 0), (0, HEAD_BLOCK - dq))).reshape(Q_LORA, -1).astype(BF16)
    wuk_p = jnp.pad(w_uk[l], ((0, 0), (0, 0), (0, HEAD_BLOCK - QK_NOPE))).reshape(KV_LORA, -1).astype(BF16)
    wuv_f = w_uv[l].reshape(KV_LORA, MLA_WIDTH).astype(BF16)
    wuv_e = jnp.pad(w_uv[l][:, 0::2], ((0, 0), (0, 0), (0, HEAD_BLOCK - V_HEAD)))
    wuv_o = jnp.pad(w_uv[l][:, 1::2], ((0, 0), (0, 0), (HEAD_BLOCK - V_HEAD, 0)))
    wuv_p = jnp.stack([wuv_e, wuv_o], axis=2).reshape(KV_LORA, -1).astype(BF16)
    wcat = jnp.zeros((MLA_HEADS, HEAD_BLOCK, KV_LORA + LANES), F32)
    wcat = wcat.at[:, 0:QK_NOPE, 0:KV_LORA].set(jnp.transpose(w_uk[l], (1, 2, 0)))
    sel = jnp.eye(QK_ROPE, dtype=F32)
    wcat = wcat.at[:, ROPE_LANE0:ROPE_LANE0 + QK_ROPE, KV_LORA:KV_LORA + QK_ROPE].set(sel)
    lanes16 = slice(DN_HEADS, 2 * DN_HEADS)
    return dict(
        norm_g=norm_g[l][None, :], small_blk=small_blk,
        pool_mix=pool_mix[l].astype(BF16), pool_scale=pool_scale[l][None, :],
        q_norm_g=q_norm_g[l][None, :], wuq_p=wuq_p, kv_norm_g=kv_norm_g[l][None, :],
        wuk_p=wuk_p, wuv_f=wuv_f, wuv_p=wuv_p, wcat=wcat.astype(BF16),
        cw8=jnp.pad(conv_w[l], ((0, SUBLANES - CONV_W), (0, 0))),
        alog_row=jnp.zeros((1, LANES), F32).at[0, lanes16].set(a_log[l]),
        dtb_row=jnp.zeros((1, LANES), F32).at[0, lanes16].set(dt_bias[l]),
        gdn_row=jnp.tile(dn_norm_g[l], DN_HEADS)[None, :],
        wa=w_br_pool[l].astype(BF16), wb=w_br_mla[l].astype(BF16), wc=w_br_dn[l].astype(BF16),
        wo=w_out[l].astype(BF16))


def _tile(n, target):
    t = min(n, target)
    while n % t:
        t -= SUBLANES
    return t


def _group_layer(x2d, b, l, start, w, wt_all, layer, tabs, final_g, final, pool_buf, conv_buf, sp0, paged,
                 prev_kv=()):
    t = b * l
    small = l < 2 * SUBLANES
    y_dtype = F32 if small else BF16
    h_pool, z_pool, h_q, h_kv, h_small, z_mla, h_qkv, z_dn, h_gate = _inproj(
        x2d, w["norm_g"], wt_all, layer, w["small_blk"], _tile(t, 256))
    h_kr = h_ba = h_small

    ya, pool_st = _pool(h_pool, z_pool, pool_buf, w["pool_mix"], w["pool_scale"], b, l, _tile(l, 512), start, y_dtype)

    tm = _tile(l, 512)
    if paged is None:
        ckv, k_r, q, k, v = _mla_prep(h_q, h_kv, h_kr, tabs, w["q_norm_g"], w["wuq_p"], w["kv_norm_g"],
                                         (w["wuk_p"], w["wuv_p"]), b, l, tm, absorbed=False, prev=prev_kv)
        yb = _attn_prompt(q, k, v, z_mla, b, l, _tile(l, 512))
    else:
        page_table, cache_kv, cache_krt = paged
        tabs_all = [jnp.tile(tab, (b, 1)) for tab in tabs]
        ckv, k_r, qcat = _mla_prep(h_q, h_kv, h_kr, tabs_all, w["q_norm_g"], w["wuq_p"], w["kv_norm_g"],
                                      (w["wcat"],), 1, t, t, absorbed=True)
        ckv, k_r = ckv[0], k_r[0]
        wq = qcat.shape[-1] // MLA_HEADS
        qcat = qcat.reshape(b, l, MLA_HEADS, wq).transpose(0, 2, 1, 3).reshape(b, MLA_HEADS * l, wq)
        n_pages = page_table.shape[1]
        n_grp = PAGES_PER_GROUP
        while n_pages % (2 * n_grp):
            n_grp //= 2
        yb = _attn_paged(page_table, qcat, cache_kv, cache_krt, layer, ckv.reshape(b, l, KV_LORA),
                         k_r.reshape(b, l, QK_ROPE), w["wuv_f"], z_mla.reshape(b, l, MLA_WIDTH), n_grp)
        yb = yb.reshape(t, MLA_WIDTH)

    if l >= DN_CHUNK:
        tl, ns = _tile(l, 256), 1
    else:
        tl, ns = l, SAMPLE_SEQS_PER_STEP
        while b % ns:
            ns //= 2
    sp0_all, sp0_layer = sp0
    yc, conv_st, sp = _delta(h_qkv, z_dn, h_ba, conv_buf, w["cw8"], w["alog_row"], w["dtb_row"], sp0_all, sp0_layer,
                             w["gdn_row"], b, l, tl, ns, y_dtype)

    x_out = _merge(x2d, ya, yb, yc, h_gate, w["wa"], w["wb"], w["wc"], w["wo"], final_g, _tile(t, 512), final)
    return x_out, ckv, k_r, pool_st[:, 1:], conv_st[:, CONV_HALO - (CONV_W - 1):], sp


def kernel(x_prompt, x_sample, cache_kv_latent, cache_k_rope, state_pool, state_conv, state_delta,
           page_table, norm_g, w_in, pool_mix, pool_scale, q_norm_g, w_uq, kv_norm_g, w_uk, w_uv,
           conv_w, a_log, dt_bias, dn_norm_g, w_br_pool, w_br_mla, w_br_dn, w_out, final_norm_g):
    bp, lp, d = x_prompt.shape
    db, ls, _ = x_sample.shape
    depth = w_in.shape[0]
    past_len = page_table.shape[1] * PAGE_SIZE
    final_g = final_norm_g[None, :]
    cache_krt = jnp.swapaxes(cache_k_rope, 2, 3)
    wt_all = jnp.swapaxes(w_in, 1, 2).astype(BF16)

    tabs_p = _rope_tables(lp, _tile(lp, 512), 0)
    tabs_s = _rope_tables(ls, ls, past_len)
    zero_pool = jnp.zeros((bp, POOL_HALO, POOL_WIDTH), F32)
    zero_conv = jnp.zeros((bp, CONV_HALO, CONV_CH), F32)
    zero_sp = (jnp.zeros((1, bp, DN_HEADS, DN_DK, DN_DV), F32), 0)

    xp = x_prompt.reshape(bp * lp, d)
    xs = x_sample.reshape(db * ls, d)
    outs_p, outs_s = [], []
    kv_stack = ()
    for l in range(depth):
        w = _layer_weights(l, norm_g, wt_all, pool_mix, pool_scale, q_norm_g, w_uq, kv_norm_g, w_uk, w_uv,
                           conv_w, a_log, dt_bias, dn_norm_g, w_br_pool, w_br_mla, w_br_dn, w_out)
        final = l == depth - 1
        xp, ckv, kr, pst, cst, sp = _group_layer(xp, bp, lp, 0, w, wt_all, l, tabs_p, final_g, final,
                                                 zero_pool, zero_conv, zero_sp, None, kv_stack)
        kv_stack = (ckv, kr)
        outs_p.append((pst, cst, sp))
        pool_buf = jnp.pad(state_pool[l], ((0, 0), (POOL_HALO - POOL_BUF, 0), (0, 0)))
        conv_buf = jnp.pad(state_conv[l], ((0, 0), (CONV_HALO - (CONV_W - 1), 0), (0, 0)))
        xs, ckv, kr, pst, cst, sp = _group_layer(xs, db, ls, past_len, w, wt_all, l, tabs_s, final_g, final,
                                                 pool_buf, conv_buf, (state_delta, l),
                                                 (page_table, cache_kv_latent, cache_krt))
        outs_s.append((ckv.reshape(db, ls, KV_LORA), kr.reshape(db, ls, QK_ROPE), pst, cst, sp))
    stack = lambda outs, i: jnp.stack([o[i] for o in outs])
    p_kv = kv_stack[0].reshape(depth, bp, lp, KV_LORA)
    p_kr = jnp.swapaxes(kv_stack[1], 2, 3)
    return (xp.reshape(bp, lp, d), xs.reshape(db, ls, d),
            p_kv, p_kr, *(stack(outs_p, i) for i in range(3)), *(stack(outs_s, i) for i in range(5)))
```
